```python
import math
import jax
import jax.numpy as jnp
from jax import lax
import numpy as np

D_MODEL = 1024
BATCH = 8
SEQ = 4096
DEPTH = 2

GRID_W = 64
CTX_LEN = 256
N_EVEN = (DEPTH + 1) // 2
N_ODD = DEPTH // 2
MOD_CHUNKS = 6

DN_HEADS = 4
DN_DK = 128
DN_DV = 128
DN_SHORT_CONV = 5
DN_CHUNK = 64
QK_W = DN_HEADS * DN_DK
V_W = DN_HEADS * DN_DV
QKV_W = 2 * QK_W + V_W
AB_W = 2 * 2 * DN_HEADS
Z_W = V_W
DN_W = V_W
FN_GROUPS = 4
FN_GROUP_W = 128
FN_W = FN_GROUPS * FN_GROUP_W
HYB_IN_W = QKV_W + AB_W + Z_W + FN_W
MIX_W = DN_W + FN_W

CONF_D = D_MODEL
CONF_K = 31

MOE_EXPERTS = 32
MOE_GROUPS = 8
EXPERTS_PER_GROUP = MOE_EXPERTS // MOE_GROUPS
MOE_GROUP_SCORE_K = 2
MOE_TOP_K = 2
MOE_FF = 512

NORM_EPS = 1e-6
POS_BASE = 10000.0

kernel_name = 'hybrid_deltanet_fnet_conformer_moe_dit'


def rmsnorm(x, g):
    xf = x.astype(jnp.float32)
    y = xf * lax.rsqrt(jnp.mean(xf * xf, axis=-1, keepdims=True) + NORM_EPS)
    return (y * g.astype(jnp.float32)).astype(x.dtype)


def layernorm(x, g, b):
    xf = x.astype(jnp.float32)
    mu = jnp.mean(xf, axis=-1, keepdims=True)
    var = jnp.mean(jnp.square(xf - mu), axis=-1, keepdims=True)
    y = (xf - mu) * lax.rsqrt(var + NORM_EPS)
    return (y * g.astype(jnp.float32) + b.astype(jnp.float32)).astype(x.dtype)


def l2norm(t):
    tf = t.astype(jnp.float32)
    return tf * lax.rsqrt(jnp.sum(tf * tf, axis=-1, keepdims=True) + NORM_EPS)


def adaln(cond, w, b):
    return jnp.split(jax.nn.silu(cond) @ w + b, MOD_CHUNKS, axis=-1)


def grid_sincos(rows, cols, dim):
    quarter = dim // 4
    omega = 1.0 / jnp.power(POS_BASE, jnp.arange(quarter, dtype=jnp.float32) / quarter)

    def axis_emb(n):
        ang = jnp.arange(n, dtype=jnp.float32)[:, None] * omega[None, :]
        return jnp.concatenate([jnp.sin(ang), jnp.cos(ang)], axis=-1)

    er = jnp.broadcast_to(axis_emb(rows)[:, None, :], (rows, cols, dim // 2))
    ec = jnp.broadcast_to(axis_emb(cols)[None, :, :], (rows, cols, dim // 2))
    return jnp.concatenate([er, ec], axis=-1).reshape(rows * cols, dim)


def depthwise_conv(x, w):
    k = w.shape[0]
    return lax.conv_general_dilated(
        x, w[:, None, :].astype(x.dtype), window_strides=(1,),
        padding=[(k // 2, k - 1 - k // 2)],
        dimension_numbers=('NWC', 'WIO', 'NWC'),
        feature_group_count=x.shape[-1])


def chunk_gated_delta(q, k, v, g, beta, s0):
    b, seq_len, h, dk = k.shape
    dv = v.shape[-1]
    n = seq_len // DN_CHUNK
    f32 = jnp.float32

    def blocks(t):
        t = t.astype(f32).reshape((b, n, DN_CHUNK, h) + t.shape[3:])
        return jnp.moveaxis(jnp.moveaxis(t, 3, 2), 1, 0)

    k, v, g, beta = blocks(k), blocks(v), blocks(g), blocks(beta)
    cum_g = jnp.cumsum(g, axis=-1)
    pos = jnp.arange(DN_CHUNK)
    incl = pos[:, None] >= pos[None, :]
    strict = pos[:, None] > pos[None, :]
    decay = jnp.exp(jnp.where(incl, cum_g[..., :, None] - cum_g[..., None, :], -jnp.inf))
    k_beta = k * beta[..., None]
    a_mat = jnp.einsum('nbhcd,nbhmd->nbhcm', k_beta, k) * jnp.where(strict, decay, 0.0)
    rhs = jnp.concatenate([v * beta[..., None], k_beta * jnp.exp(cum_g)[..., None]], axis=-1)
    sol = lax.linalg.triangular_solve(a_mat + jnp.eye(DN_CHUNK, dtype=f32), rhs,
                                      left_side=True, lower=True, unit_diagonal=True)
    u, w = sol[..., :dv], sol[..., dv:]
    k_tail = k * jnp.exp(cum_g[..., -1:] - cum_g)[..., None]
    chunk_decay = jnp.exp(cum_g[..., -1])
    xs = (u, w, k_tail, chunk_decay)
    with_output = q is not None
    if with_output:
        q = blocks(q) * (dk ** -0.5)
        q_dec = q * jnp.exp(cum_g)[..., None]
        qk = jnp.einsum('nbhcd,nbhmd->nbhcm', q, k) * decay
        xs = xs + (q_dec, qk)

    def step(state, xc):
        v_new = xc[0] - jnp.einsum('bhcd,bhdv->bhcv', xc[1], state)
        new_state = state * xc[3][..., None, None] + jnp.einsum('bhcd,bhcv->bhdv', xc[2], v_new)
        if with_output:
            o_c = (jnp.einsum('bhcd,bhdv->bhcv', xc[4], state)
                   + jnp.einsum('bhcm,bhmv->bhcv', xc[5], v_new))
            return new_state, o_c
        return new_state, None

    s_final, o = lax.scan(step, s0.astype(f32), xs)
    if with_output:
        o = jnp.moveaxis(jnp.moveaxis(o, 0, 1), 2, 3).reshape(b, seq_len, h, dv)
    return o, s_final


def dn_gates(p_ab, a_log, dt_bias):
    ab = p_ab.astype(jnp.float32).reshape(p_ab.shape[:2] + (2, 2, DN_HEADS))
    g = -jnp.exp(a_log.astype(jnp.float32)) * jax.nn.softplus(ab[:, :, 0] + dt_bias.astype(jnp.float32))
    beta = jax.nn.sigmoid(ab[:, :, 1])
    return g, beta


def dn_bidir(q, k, v, g, beta, s0_fwd, s0_bwd):
    def flip(t):
        return None if t is None else jnp.flip(t, axis=1)
    o_f, s_f = chunk_gated_delta(q, k, v, g[:, :, 0], beta[:, :, 0], s0_fwd)
    o_b, s_b = chunk_gated_delta(flip(q), flip(k), flip(v), flip(g[:, :, 1]), flip(beta[:, :, 1]), s0_bwd)
    o = None if q is None else o_f + flip(o_b)
    return o, s_f, s_b


def hybrid_mixer(h, w_in, conv_w, a_log, dt_bias, onorm_g, w_out, s0_fwd, s0_bwd):
    b, seq_len, _ = h.shape
    p = h @ w_in
    qkv = jax.nn.silu(depthwise_conv(p[..., :QKV_W], conv_w))
    q = l2norm(qkv[..., :QK_W].reshape(b, seq_len, DN_HEADS, DN_DK))
    k = l2norm(qkv[..., QK_W:2 * QK_W].reshape(b, seq_len, DN_HEADS, DN_DK))
    v = qkv[..., 2 * QK_W:].reshape(b, seq_len, DN_HEADS, DN_DV)
    g, beta = dn_gates(p[..., QKV_W:QKV_W + AB_W], a_log, dt_bias)
    o, s_f, s_b = dn_bidir(q, k, v, g, beta, s0_fwd, s0_bwd)
    z_lo = QKV_W + AB_W
    f_lo = z_lo + Z_W
    z = p[..., z_lo:f_lo].reshape(b, seq_len, DN_HEADS, DN_DV).astype(jnp.float32)
    dn_out = (rmsnorm(o, onorm_g) * jax.nn.silu(z)).reshape(b, seq_len, DN_W).astype(h.dtype)
    f = p[..., f_lo:].reshape(b, seq_len, FN_GROUPS, FN_GROUP_W).astype(jnp.float32)
    fn_out = jnp.fft.fft2(f, axes=(1, 3), norm='ortho').real.reshape(b, seq_len, FN_W).astype(h.dtype)
    return jnp.concatenate([dn_out, fn_out], axis=-1) @ w_out, s_f, s_b


def dn_context_states(hc, w_in, conv_w, a_log, dt_bias):
    b, seq_len, _ = hc.shape
    p = hc @ w_in[:, QK_W:QKV_W + AB_W]
    kv = jax.nn.silu(depthwise_conv(p[..., :QK_W + V_W], conv_w[:, QK_W:]))
    k = l2norm(kv[..., :QK_W].reshape(b, seq_len, DN_HEADS, DN_DK))
    v = kv[..., QK_W:].reshape(b, seq_len, DN_HEADS, DN_DV)
    g, beta = dn_gates(p[..., QK_W + V_W:], a_log, dt_bias)
    zero = jnp.zeros((b, DN_HEADS, DN_DK, DN_DV), jnp.float32)
    _, s_f, s_b = dn_bidir(None, k, v, g, beta, zero, zero)
    return s_f, s_b


def conformer_conv(h, w1, b1, dw_w, dw_b, ln_g, ln_b, w2, b2):
    u = h @ w1 + b1
    val, gate = jnp.split(u, 2, axis=-1)
    u = val * jax.nn.sigmoid(gate)
    u = depthwise_conv(u, dw_w) + dw_b
    u = jax.nn.silu(layernorm(u, ln_g, ln_b))
    return u @ w2 + b2


def moe_ffn(h, router_w, router_bias, w_gate, w_up, w_down):
    b, seq_len, d = h.shape
    t = h.reshape(b * seq_len, d)
    scores = jax.nn.sigmoid(t.astype(jnp.float32) @ router_w.astype(jnp.float32))
    sel = scores + router_bias.astype(jnp.float32)
    group_score = jnp.sum(lax.top_k(sel.reshape(-1, MOE_GROUPS, EXPERTS_PER_GROUP), MOE_GROUP_SCORE_K)[0], axis=-1)
    best_group = jnp.argmax(group_score, axis=-1)
    in_group = (jnp.arange(MOE_EXPERTS) // EXPERTS_PER_GROUP)[None, :] == best_group[:, None]
    _, top_idx = lax.top_k(jnp.where(in_group, sel, -jnp.inf), MOE_TOP_K)
    top_w = jnp.take_along_axis(scores, top_idx, axis=-1)
    top_w = top_w / jnp.sum(top_w, axis=-1, keepdims=True)
    combine = jnp.sum(jax.nn.one_hot(top_idx, MOE_EXPERTS, dtype=jnp.float32) * top_w[..., None], axis=1)
    y = jnp.zeros((b * seq_len, d), jnp.float32)
    for e in range(MOE_EXPERTS):
        hid = jax.nn.silu(t @ w_gate[e]) * (t @ w_up[e])
        y = y + combine[:, e:e + 1] * (hid @ w_down[e])
    return y.astype(h.dtype).reshape(b, seq_len, d)


def setup_inputs(seed: int = 0) -> dict:
    key = jax.random.key(seed)
    ks = jax.random.split(key, 28)
    f32 = jnp.float32
    d = D_MODEL

    def nrm(k, shape, scale):
        return jax.random.normal(k, shape, f32) * scale

    dt = jnp.exp(jax.random.uniform(ks[11], (N_EVEN, 2, DN_HEADS), f32, math.log(1e-3), math.log(1e-1)))
    return {
        'x': nrm(ks[0], (BATCH, SEQ, d), 1.0),
        'c': nrm(ks[1], (BATCH, d), 1.0),
        'ctx': nrm(ks[2], (BATCH, CTX_LEN, d), 1.0),
        'c_ctx': nrm(ks[3], (d,), 1.0),
        'ada_w': nrm(ks[4], (DEPTH, d, MOD_CHUNKS * d), 0.5 * d ** -0.5),
        'ada_b': nrm(ks[5], (DEPTH, MOD_CHUNKS * d), 0.02),
        'norm1_g': 1.0 + nrm(ks[6], (DEPTH, d), 0.05),
        'norm2_g': 1.0 + nrm(ks[7], (DEPTH, d), 0.05),
        'hyb_w_in': nrm(ks[8], (N_EVEN, d, HYB_IN_W), d ** -0.5),
        'dn_conv_w': nrm(ks[9], (N_EVEN, DN_SHORT_CONV, QKV_W), DN_SHORT_CONV ** -0.5),
        'dn_a_log': jnp.log(jax.random.uniform(ks[10], (N_EVEN, 2, DN_HEADS), f32, 1.0, 16.0)),
        'dn_dt_bias': dt + jnp.log(-jnp.expm1(-dt)),
        'dn_onorm_g': 1.0 + nrm(ks[12], (N_EVEN, DN_DV), 0.05),
        'hyb_w_out': nrm(ks[13], (N_EVEN, MIX_W, d), MIX_W ** -0.5),
        'conf_w1': nrm(ks[14], (N_ODD, d, 2 * CONF_D), d ** -0.5),
        'conf_b1': nrm(ks[15], (N_ODD, 2 * CONF_D), 0.02),
        'conf_dw_w': nrm(ks[16], (N_ODD, CONF_K, CONF_D), CONF_K ** -0.5),
        'conf_dw_b': nrm(ks[17], (N_ODD, CONF_D), 0.02),
        'conf_ln_g': 1.0 + nrm(ks[18], (N_ODD, CONF_D), 0.05),
        'conf_ln_b': nrm(ks[19], (N_ODD, CONF_D), 0.02),
        'conf_w2': nrm(ks[20], (N_ODD, CONF_D, d), CONF_D ** -0.5),
        'conf_b2': nrm(ks[21], (N_ODD, d), 0.02),
        'router_w': nrm(ks[22], (d, MOE_EXPERTS), d ** -0.5),
        'router_bias': nrm(ks[23], (MOE_EXPERTS,), 0.01),
        'moe_w_gate': nrm(ks[24], (DEPTH, MOE_EXPERTS, d, MOE_FF), d ** -0.5),
        'moe_w_up': nrm(ks[25], (DEPTH, MOE_EXPERTS, d, MOE_FF), d ** -0.5),
        'moe_w_down': nrm(ks[26], (DEPTH, MOE_EXPERTS, MOE_FF, d), MOE_FF ** -0.5),
        'final_g': 1.0 + nrm(ks[27], (d,), 0.05),
    }


def reference(x, c, ctx, c_ctx, ada_w, ada_b, norm1_g, norm2_g, hyb_w_in, dn_conv_w, dn_a_log,
              dn_dt_bias, dn_onorm_g, hyb_w_out, conf_w1, conf_b1, conf_dw_w, conf_dw_b, conf_ln_g,
              conf_ln_b, conf_w2, conf_b2, router_w, router_bias, moe_w_gate, moe_w_up, moe_w_down,
              final_g):
    b, seq_len, d = x.shape
    rows = seq_len // GRID_W
    x = x + grid_sincos(rows, GRID_W, d).astype(x.dtype)[None]
    zero_state = jnp.zeros((b, DN_HEADS, DN_DK, DN_DV), jnp.float32)
    for l in range(DEPTH):
        i = l // 2
        even = l % 2 == 0
        ctx_continues = any(j % 2 == 0 for j in range(l + 1, DEPTH))
        sh1, sc1, g1, sh2, sc2, g2 = [m[:, None, :] for m in adaln(c, ada_w[l], ada_b[l])]
        h = rmsnorm(x, norm1_g[l]) * (1 + sc1) + sh1
        if even or ctx_continues:
            cm = adaln(c_ctx, ada_w[l], ada_b[l])
            hc = rmsnorm(ctx, norm1_g[l]) * (1 + cm[1]) + cm[0]
        if even:
            if ctx_continues:
                ctx_mix, s_f, s_b = hybrid_mixer(hc, hyb_w_in[i], dn_conv_w[i], dn_a_log[i], dn_dt_bias[i],
                                                 dn_onorm_g[i], hyb_w_out[i], zero_state, zero_state)
            else:
                s_f, s_b = dn_context_states(hc, hyb_w_in[i], dn_conv_w[i], dn_a_log[i], dn_dt_bias[i])
            mix, _, _ = hybrid_mixer(h, hyb_w_in[i], dn_conv_w[i], dn_a_log[i], dn_dt_bias[i],
                                     dn_onorm_g[i], hyb_w_out[i], s_f, s_b)
        else:
            mix = conformer_conv(h, conf_w1[i], conf_b1[i], conf_dw_w[i], conf_dw_b[i], conf_ln_g[i],
                                 conf_ln_b[i], conf_w2[i], conf_b2[i])
            if ctx_continues:
                ctx_mix = conformer_conv(hc, conf_w1[i], conf_b1[i], conf_dw_w[i], conf_dw_b[i], conf_ln_g[i],
                                         conf_ln_b[i], conf_w2[i], conf_b2[i])
        x = x + g1 * mix
        x = x + g2 * moe_ffn(rmsnorm(x, norm2_g[l]) * (1 + sc2) + sh2, router_w, router_bias,
                             moe_w_gate[l], moe_w_up[l], moe_w_down[l])
        if ctx_continues:
            ctx = ctx + cm[2] * ctx_mix
            ctx = ctx + cm[5] * moe_ffn(rmsnorm(ctx, norm2_g[l]) * (1 + cm[4]) + cm[3], router_w, router_bias,
                                        moe_w_gate[l], moe_w_up[l], moe_w_down[l])
    return rmsnorm(x, final_g)
```

```python
import functools
import math

import numpy as np
import jax
import jax.numpy as jnp
from jax import lax
from jax.experimental import pallas as pl
from jax.experimental.pallas import tpu as pltpu

F32 = jnp.float32
BF16 = jnp.bfloat16
I32 = jnp.int32

D = 1024
GRID_W = 64
HEADS = 4
DK = 128
DV = 128
CHUNK = 64
QK_W = HEADS * DK
V_W = HEADS * DV
QKV_W = 2 * QK_W + V_W
AB_W = 16
Z_W = V_W
FN_W = 512
DN_CONV_K = 5
CONF_K = 31
N_EXP = 32
N_GROUPS = 8
EXP_PER_GROUP = 4
FF = 512
EPS = 1e-6
POS_BASE = 10000.0

LANES = 128
HALO = 16
TM = 256
TME = 256
HB = HEADS * CHUNK
VMEM_LIMIT = 56 * 1024 * 1024


def _cparams(sem):
    return pltpu.CompilerParams(dimension_semantics=sem, vmem_limit_bytes=VMEM_LIMIT)


def _sigmoid(v):
    return jax.nn.sigmoid(v)


def _silu(v):
    return v * jax.nn.sigmoid(v)


def _dot(a, b):
    return jnp.dot(a, b, preferred_element_type=F32)


def _dot_nt(a, b):
    return lax.dot_general(a, b, (((1,), (1,)), ((), ())), preferred_element_type=F32)


def _split3(a):
    a1 = a.astype(BF16)
    r1 = a - a1.astype(F32)
    a2 = r1.astype(BF16)
    a3 = (r1 - a2.astype(F32)).astype(BF16)
    return a1, a2, a3


def _rms(v, eps=EPS):
    return v * lax.rsqrt(jnp.mean(v * v, axis=-1, keepdims=True) + eps)


def _adaln_body(c_ref, w_ref, b_ref, o_ref):
    s = _silu(c_ref[...])
    o_ref[0] = jnp.dot(s, w_ref[0], preferred_element_type=F32,
                       precision=lax.Precision.HIGHEST) + b_ref[0]


def _adaln(cpad, ada_w, ada_b):
    depth, d, n = ada_w.shape
    tn = 1536
    rows = cpad.shape[0]
    return pl.pallas_call(
        _adaln_body,
        out_shape=jax.ShapeDtypeStruct((depth, rows, n), F32),
        grid=(depth, n // tn),
        in_specs=[
            pl.BlockSpec((rows, d), lambda l, j: (0, 0)),
            pl.BlockSpec((1, d, tn), lambda l, j: (l, 0, j)),
            pl.BlockSpec((1, 1, tn), lambda l, j: (l, 0, j)),
        ],
        out_specs=pl.BlockSpec((1, rows, tn), lambda l, j: (l, 0, j)),
        compiler_params=_cparams(("arbitrary", "arbitrary")),
        name="adaln",
    )(cpad, ada_w, ada_b.reshape(depth, 1, n))


def _inproj_body(nct, ctx_ref, x_ref, pos_ref, ma_ref, mb_ref, wqkv_ref, wz_ref, wf_ref, wab_ref,
                 dftw_ref, gpar_ref, qkv_ref, z_ref, fc_ref, fs_ref, gate_ref):
    i = pl.program_id(1)
    is_ctx = i < nct
    xin = jnp.where(is_ctx, ctx_ref[0], x_ref[0] + pos_ref[...])
    a = jnp.where(is_ctx, ma_ref[0, 0:1, :], ma_ref[0, 1:2, :])
    b = jnp.where(is_ctx, mb_ref[0, 0:1, :], mb_ref[0, 1:2, :])
    h = (_rms(xin) * a + b).astype(BF16)
    qkv_ref[0] = _dot(h, wqkv_ref[...]).astype(BF16)
    z_ref[0] = _dot(h, wz_ref[...]).astype(BF16)
    f = _dot(h, wf_ref[...]).astype(BF16)
    fcs = _dot(f, dftw_ref[...])
    fc_ref[0] = fcs[:, :FN_W].astype(BF16)
    fs_ref[0] = fcs[:, FN_W:].astype(BF16)
    ab = _dot(h, wab_ref[...])
    lane = lax.broadcasted_iota(I32, ab.shape, 1)
    pre = ab + gpar_ref[1:2, :]
    softplus = jnp.maximum(pre, 0.0) + jnp.log1p(jnp.exp(-jnp.abs(pre)))
    gate_ref[0] = jnp.where(lane < 2 * HEADS, gpar_ref[0:1, :] * softplus, _sigmoid(ab))


def _inproj(ctx, x, pos, mod_a, mod_b, wqkv, wz, wf, wab, dftw, gpar):
    bsz, seq, d = x.shape
    nct = ctx.shape[1] // TM
    nt = nct + seq // TM
    lc = nt * TM
    xi = lambda b, i: (b, jnp.maximum(i - nct, 0), 0)
    full = lambda shape: pl.BlockSpec(shape, lambda b, i: (0,) * len(shape))
    return pl.pallas_call(
        functools.partial(_inproj_body, nct),
        out_shape=(
            jax.ShapeDtypeStruct((bsz, lc, QKV_W), BF16),
            jax.ShapeDtypeStruct((bsz, seq, Z_W), BF16),
            jax.ShapeDtypeStruct((bsz, seq, FN_W), BF16),
            jax.ShapeDtypeStruct((bsz, seq, FN_W), BF16),
            jax.ShapeDtypeStruct((bsz, lc, LANES), F32),
        ),
        grid=(bsz, nt),
        in_specs=[
            pl.BlockSpec((1, TM, d), lambda b, i: (b, jnp.minimum(i, nct - 1), 0)),
            pl.BlockSpec((1, TM, d), xi),
            pl.BlockSpec((TM, d), lambda b, i: (jnp.maximum(i - nct, 0), 0)),
            pl.BlockSpec((1, 2, d), lambda b, i: (b, 0, 0)),
            pl.BlockSpec((1, 2, d), lambda b, i: (b, 0, 0)),
            full((d, QKV_W)), full((d, Z_W)), full((d, FN_W)), full((d, LANES)),
            full((FN_W, 2 * FN_W)), full((8, LANES)),
        ],
        out_specs=(
            pl.BlockSpec((1, TM, QKV_W), lambda b, i: (b, i, 0)),
            pl.BlockSpec((1, TM, Z_W), xi),
            pl.BlockSpec((1, TM, FN_W), xi),
            pl.BlockSpec((1, TM, FN_W), xi),
            pl.BlockSpec((1, TM, LANES), lambda b, i: (b, i, 0)),
        ),
        compiler_params=_cparams(("arbitrary", "arbitrary")),
        name="inproj",
    )(ctx, x, pos, mod_a, mod_b, wqkv, wz, wf, wab, dftw, gpar)


def _dnconv_body(nct, nt, cur_ref, prev_ref, next_ref, w_ref, o_ref, ext_ref):
    i = pl.program_id(1)
    first = jnp.logical_or(i == 0, i == nct)
    last = jnp.logical_or(i == nct - 1, i == nt - 1)
    ext_ref[0:HALO, :] = jnp.where(first, 0.0, prev_ref[0].astype(F32))
    ext_ref[HALO:HALO + TM, :] = cur_ref[0].astype(F32)
    ext_ref[HALO + TM:2 * HALO + TM, :] = jnp.where(last, 0.0, next_ref[0].astype(F32))
    pad = DN_CONV_K // 2
    for cb in range(QKV_W // LANES):
        cs = slice(cb * LANES, (cb + 1) * LANES)
        acc = None
        for j in range(DN_CONV_K):
            lo = HALO - pad + j
            term = ext_ref[lo:lo + TM, cs] * w_ref[j:j + 1, cs]
            acc = term if acc is None else acc + term
        y = _silu(acc)
        if cb < 2 * HEADS:
            y = y * lax.rsqrt(jnp.sum(y * y, axis=-1, keepdims=True) + EPS)
        o_ref[0, :, cs] = y.astype(BF16)


def _dnconv(qkv_pre, conv_w8, nct):
    bsz, lc, w = qkv_pre.shape
    nt = lc // TM
    r = TM // HALO
    return pl.pallas_call(
        functools.partial(_dnconv_body, nct, nt),
        out_shape=jax.ShapeDtypeStruct((bsz, lc, w), BF16),
        grid=(bsz, nt),
        in_specs=[
            pl.BlockSpec((1, TM, w), lambda b, i: (b, i, 0)),
            pl.BlockSpec((1, HALO, w), lambda b, i: (b, jnp.maximum(i * r - 1, 0), 0)),
            pl.BlockSpec((1, HALO, w), lambda b, i: (b, jnp.minimum((i + 1) * r, nt * r - 1), 0)),
            pl.BlockSpec((8, w), lambda b, i: (0, 0)),
        ],
        out_specs=pl.BlockSpec((1, TM, w), lambda b, i: (b, i, 0)),
        scratch_shapes=[pltpu.VMEM((TM + 2 * HALO, w), F32)],
        compiler_params=_cparams(("arbitrary", "arbitrary")),
        name="dnconv",
    )(qkv_pre, qkv_pre, qkv_pre, conv_w8)


def _dn_dir(d, qkv, grow, mask_incl, mask_strict, tri, trit, bdones, eye, lvl_ref, s_ref):
    neg_inf = float("-inf")
    stack = lambda off: jnp.concatenate(
        [qkv[:, off + h * DK: off + (h + 1) * DK] for h in range(HEADS)], axis=0)
    q_s = stack(0)
    k_s = stack(QK_W)
    v_s = stack(2 * QK_W)
    g3 = _split3(grow)
    cg_row = sum(_dot(t, trit) for t in g3)
    lhs = jnp.concatenate([tri, bdones, eye], axis=0)
    cols = sum(_dot_nt(lhs, t) for t in g3)
    cgc = cols[0:HB, d:d + 1]
    totc = cols[HB:2 * HB, d:d + 1]
    betac = cols[2 * HB:3 * HB, 2 + d:3 + d]
    cgr = cg_row[d:d + 1, :]
    dmat = jnp.exp(jnp.where(mask_incl > 0.5, cgc - cgr, neg_inf))
    kk = _dot_nt(k_s, k_s)
    qk = _dot_nt(q_s, k_s)
    a = jnp.where(mask_strict > 0.5, kk * dmat, 0.0) * betac
    x = eye.astype(F32) - a * lvl_ref[0]
    for lvl in range(1, int(math.log2(CHUNK))):
        xb = x.astype(BF16)
        y = _dot((a * lvl_ref[lvl]).astype(BF16), xb)
        x = x - _dot(xb, y.astype(BF16))
    ecg = jnp.exp(cgc)
    k32 = k_s.astype(F32)
    rhs = jnp.concatenate([(betac * v_s.astype(F32)).astype(BF16),
                           ((betac * ecg) * k32).astype(BF16)], axis=1)
    uw = _dot(x.astype(BF16), rhs)
    u = uw[:, :DV]
    w = uw[:, DV:].astype(BF16)
    ktail = (k32 * jnp.exp(totc - cgc)).astype(BF16)
    scale = DK ** -0.5
    qdec = (q_s.astype(F32) * (ecg * scale)).astype(BF16)
    qkd = (qk * dmat * scale).astype(BF16)
    vnew = []
    o_inter = []
    for h in range(HEADS):
        rs = slice(h * CHUNK, (h + 1) * CHUNK)
        s = s_ref[d, h]
        sb = s.astype(BF16)
        vn = u[rs] - _dot(w[rs], sb)
        vnew.append(vn)
        o_inter.append(_dot(qdec[rs], sb))
        cd = jnp.exp(totc[h * CHUNK:h * CHUNK + 1, :])
        upd = lax.dot_general(ktail[rs], vn.astype(BF16), (((0,), (0,)), ((), ())),
                              preferred_element_type=F32)
        s_ref[d, h] = s * cd + upd
    vn_s = jnp.concatenate(vnew, axis=0).astype(BF16)
    o_s = jnp.concatenate(o_inter, axis=0) + _dot(qkd, vn_s)
    return jnp.concatenate([o_s[h * CHUNK:(h + 1) * CHUNK] for h in range(HEADS)], axis=1)


def _dn_body(qf_ref, qb_ref, gf_ref, gb_ref, mli_ref, mls_ref, mui_ref, mus_ref, bdl_ref, bdu_ref,
             bdo_ref, eye_ref, lvl_ref, of_ref, ob_ref, s_ref):
    @pl.when(pl.program_id(1) == 0)
    def _():
        s_ref[...] = jnp.zeros_like(s_ref)

    bdl = bdl_ref[...]
    bdu = bdu_ref[...]
    bdo = bdo_ref[...]
    eye = eye_ref[...]
    of_ref[0] = _dn_dir(0, qf_ref[0], gf_ref[0, 0], mli_ref[...], mls_ref[...], bdl, bdu, bdo, eye,
                        lvl_ref, s_ref).astype(BF16)
    ob_ref[0] = _dn_dir(1, qb_ref[0], gb_ref[0, 0], mui_ref[...], mus_ref[...], bdu, bdl, bdo, eye,
                        lvl_ref, s_ref).astype(BF16)


def _dn_consts():
    r = np.arange(HB)
    same = (r[:, None] // CHUNK) == (r[None, :] // CHUNK)
    li = same & (r[:, None] >= r[None, :])
    ls = same & (r[:, None] > r[None, :])
    ui = same & (r[:, None] <= r[None, :])
    us = same & (r[:, None] < r[None, :])
    f = lambda m: jnp.asarray(m.astype(np.float32))
    h = lambda m: jnp.asarray(m.astype(np.float32), dtype=BF16)
    nlev = int(math.log2(CHUNK))
    lv = [((r[:, None] >> (l + 1)) == (r[None, :] >> (l + 1))) & ((r[:, None] >> l) != (r[None, :] >> l))
          for l in range(nlev)]
    return (f(li), f(ls), f(ui), f(us), h(li), h(ui), h(same), h(np.eye(HB, dtype=bool)),
            jnp.asarray(np.stack(lv).astype(np.float32)))


def _deltanet(qkv, grow, seq, ncc):
    bsz, lc, w = qkv.shape
    nchunks = lc // CHUNK
    nx = seq // CHUNK
    fwd = lambda s: s
    bwd = lambda s: jnp.where(s < ncc, ncc - 1 - s, ncc + nchunks - 1 - s)
    ofi = lambda b, s: (b, jnp.maximum(s - ncc, 0), 0)
    obi = lambda b, s: (b, jnp.minimum(nchunks - 1 - s, nx - 1), 0)
    cst = lambda shape: pl.BlockSpec(shape, lambda b, s: (0, 0))
    consts = _dn_consts()
    return pl.pallas_call(
        _dn_body,
        out_shape=(jax.ShapeDtypeStruct((bsz, seq, V_W), BF16),
                   jax.ShapeDtypeStruct((bsz, seq, V_W), BF16)),
        grid=(bsz, nchunks),
        in_specs=[
            pl.BlockSpec((1, CHUNK, w), lambda b, s: (b, fwd(s), 0)),
            pl.BlockSpec((1, CHUNK, w), lambda b, s: (b, bwd(s), 0)),
            pl.BlockSpec((1, 1, 8, HB), lambda b, s: (b, fwd(s), 0, 0)),
            pl.BlockSpec((1, 1, 8, HB), lambda b, s: (b, bwd(s), 0, 0)),
        ] + [cst((HB, HB))] * 8 + [pl.BlockSpec((int(math.log2(CHUNK)), HB, HB), lambda b, s: (0, 0, 0))],
        out_specs=(pl.BlockSpec((1, CHUNK, V_W), ofi), pl.BlockSpec((1, CHUNK, V_W), obi)),
        scratch_shapes=[pltpu.VMEM((2, HEADS, DK, DV), F32)],
        compiler_params=_cparams(("arbitrary", "arbitrary")),
        name="deltanet",
    )(qkv, qkv, grow, grow, *consts)


def _fft1_body(ns2, fc_ref, fs_ref, m_ref, cos_ref, sin_ref, y_ref):
    n1 = fc_ref.shape[1]
    xin = jnp.concatenate([fc_ref[0], fs_ref[0]], axis=0)
    y = _dot(m_ref[...], xin)
    yr, yi = y[:n1], y[n1:]
    for q in range(ns2):
        cs = slice(q * FN_W, (q + 1) * FN_W)
        c = jnp.concatenate([cos_ref[q]] * (FN_W // LANES), axis=1)
        s = jnp.concatenate([sin_ref[q]] * (FN_W // LANES), axis=1)
        y_ref[0, 0, :, cs] = (yr[:, cs] * c + yi[:, cs] * s).astype(BF16)
        y_ref[0, 1, :, cs] = (yi[:, cs] * c - yr[:, cs] * s).astype(BF16)


def _fft2_body(ns1, y_ref, cs_ref, o_ref):
    for q in range(ns1):
        rhs = jnp.concatenate([y_ref[0, 0, q], y_ref[0, 1, q]], axis=0)
        o_ref[0, :, q * FN_W:(q + 1) * FN_W] = _dot(cs_ref[...], rhs).astype(BF16)


def _seq_dft(fc, fs):
    bsz, seq, w = fc.shape
    n2 = GRID_W
    n1 = seq // n2
    k1 = np.arange(n1)
    ang1 = 2 * np.pi * np.outer(k1, k1) / n1
    c1, s1 = np.cos(ang1), np.sin(ang1)
    m1 = jnp.asarray(np.block([[c1, -s1], [-s1, -c1]]), dtype=BF16)
    tw = 2 * np.pi * np.outer(np.arange(n2), k1) / seq
    cos_t = jnp.asarray(np.repeat(np.cos(tw)[:, :, None], LANES, axis=2), dtype=F32)
    sin_t = jnp.asarray(np.repeat(np.sin(tw)[:, :, None], LANES, axis=2), dtype=F32)
    k2 = np.arange(n2)
    ang2 = 2 * np.pi * np.outer(k2, k2) / n2
    cs2 = jnp.asarray(np.concatenate([np.cos(ang2), np.sin(ang2)], axis=1) / math.sqrt(seq), dtype=BF16)

    ns2 = 4
    y = pl.pallas_call(
        functools.partial(_fft1_body, ns2),
        out_shape=jax.ShapeDtypeStruct((bsz, 2, n1, n2 * w), BF16),
        grid=(bsz, n2 // ns2),
        in_specs=[
            pl.BlockSpec((1, n1, ns2 * w), lambda b, j: (b, 0, j)),
            pl.BlockSpec((1, n1, ns2 * w), lambda b, j: (b, 0, j)),
            pl.BlockSpec((2 * n1, 2 * n1), lambda b, j: (0, 0)),
            pl.BlockSpec((ns2, n1, LANES), lambda b, j: (j, 0, 0)),
            pl.BlockSpec((ns2, n1, LANES), lambda b, j: (j, 0, 0)),
        ],
        out_specs=pl.BlockSpec((1, 2, n1, ns2 * w), lambda b, j: (b, 0, 0, j)),
        compiler_params=_cparams(("arbitrary", "arbitrary")),
        name="seqdft_stage1",
    )(fc.reshape(bsz, n1, n2 * w), fs.reshape(bsz, n1, n2 * w), m1, cos_t, sin_t)

    ns1 = min(8, n1)
    out = pl.pallas_call(
        functools.partial(_fft2_body, ns1),
        out_shape=jax.ShapeDtypeStruct((bsz, n2, n1 * w), BF16),
        grid=(bsz, n1 // ns1),
        in_specs=[
            pl.BlockSpec((1, 2, ns1, n2, w), lambda b, j: (b, 0, j, 0, 0)),
            pl.BlockSpec((n2, 2 * n2), lambda b, j: (0, 0)),
        ],
        out_specs=pl.BlockSpec((1, n2, ns1 * w), lambda b, j: (b, 0, j)),
        compiler_params=_cparams(("arbitrary", "arbitrary")),
        name="seqdft_stage2",
    )(y.reshape(bsz, 2, n1, n2, w), cs2)
    return out.reshape(bsz, seq, w)


def _route_tail(x_new, pv_ref, rwt_ref, rb_ref, upper_ref, x_out_ref, h2_ref, mi_ref, mf_ref, cnt_ref,
                carry_ref):
    first = jnp.logical_and(pl.program_id(0) == 0, pl.program_id(1) == 0)

    @pl.when(first)
    def _():
        carry_ref[...] = jnp.zeros_like(carry_ref)

    x_out_ref[0] = x_new
    h2 = _rms(x_new) * pv_ref[0, 1:2, :] + pv_ref[0, 2:3, :]
    h2_ref[...] = h2
    logits = _dot_nt(rwt_ref[...], h2.astype(BF16))
    scores = _sigmoid(logits)
    sel = scores + rb_ref[:, 0:1]
    a = [sel[N_GROUPS * j:N_GROUPS * (j + 1)] for j in range(EXP_PER_GROUP)]
    sc = [scores[N_GROUPS * j:N_GROUPS * (j + 1)] for j in range(EXP_PER_GROUP)]
    gs = None
    for j0 in range(EXP_PER_GROUP):
        for j1 in range(j0 + 1, EXP_PER_GROUP):
            pair = a[j0] + a[j1]
            gs = pair if gs is None else jnp.maximum(gs, pair)
    gidx = lax.broadcasted_iota(I32, gs.shape, 0).astype(F32)
    gmax = jnp.max(gs, axis=0, keepdims=True)
    best = jnp.min(jnp.where(gs == gmax, gidx, float(N_GROUPS)), axis=0, keepdims=True)
    inb = gidx == best
    v = [jnp.sum(jnp.where(inb, a[j], 0.0), axis=0, keepdims=True) for j in range(EXP_PER_GROUP)]
    s = [jnp.sum(jnp.where(inb, sc[j], 0.0), axis=0, keepdims=True) for j in range(EXP_PER_GROUP)]
    v1, i1, s1 = v[0], jnp.zeros_like(v[0]), s[0]
    for j in range(1, EXP_PER_GROUP):
        better = v[j] > v1
        v1 = jnp.where(better, v[j], v1)
        i1 = jnp.where(better, float(j), i1)
        s1 = jnp.where(better, s[j], s1)
    v2 = jnp.full_like(v1, float("-inf"))
    i2 = jnp.zeros_like(v1)
    s2 = jnp.zeros_like(v1)
    for j in range(EXP_PER_GROUP):
        better = jnp.logical_and(i1 != float(j), v[j] > v2)
        v2 = jnp.where(better, v[j], v2)
        i2 = jnp.where(better, float(j), i2)
        s2 = jnp.where(better, s[j], s2)
    denom = s1 + s2
    rid = lax.broadcasted_iota(I32, logits.shape, 0).astype(F32)
    oh1 = rid == (i1 * N_GROUPS + best)
    oh2 = rid == (i2 * N_GROUPS + best)
    oh = jnp.where(jnp.logical_or(oh1, oh2), 1.0, 0.0)
    prefix = _dot(oh.astype(BF16), upper_ref[...])
    base = carry_ref[:, 0:1] + prefix
    rank1 = jnp.sum(jnp.where(oh1, base, 0.0), axis=0, keepdims=True)
    rank2 = jnp.sum(jnp.where(oh2, base, 0.0), axis=0, keepdims=True)
    carry_new = carry_ref[...] + jnp.sum(oh, axis=1, keepdims=True)
    carry_ref[...] = carry_new
    cnt_ref[...] = carry_new
    mi_ref[...] = jnp.zeros_like(mi_ref)
    mi_ref[0:1, :] = (best * EXP_PER_GROUP + i1).astype(I32)
    mi_ref[1:2, :] = (best * EXP_PER_GROUP + i2).astype(I32)
    mi_ref[2:3, :] = rank1.astype(I32)
    mi_ref[3:4, :] = rank2.astype(I32)
    mf_ref[...] = jnp.zeros_like(mf_ref)
    mf_ref[0:1, :] = s1 / denom
    mf_ref[1:2, :] = s2 / denom


def _route_out_shapes(bsz, seq):
    t = bsz * seq
    return (
        jax.ShapeDtypeStruct((bsz, seq, D), F32),
        jax.ShapeDtypeStruct((t, D), F32),
        jax.ShapeDtypeStruct((8, t), I32),
        jax.ShapeDtypeStruct((8, t), F32),
        jax.ShapeDtypeStruct((N_EXP, LANES), F32),
    )


def _route_out_specs(nt):
    flat = lambda b, i: (b * nt + i, 0)
    return (
        pl.BlockSpec((1, TM, D), lambda b, i: (b, i, 0)),
        pl.BlockSpec((TM, D), flat),
        pl.BlockSpec((8, TM), lambda b, i: (0, b * nt + i)),
        pl.BlockSpec((8, TM), lambda b, i: (0, b * nt + i)),
        pl.BlockSpec((N_EXP, LANES), lambda b, i: (0, 0)),
    )


def _route_in_specs():
    return [
        pl.BlockSpec((N_EXP, D), lambda b, i: (0, 0)),
        pl.BlockSpec((N_EXP, LANES), lambda b, i: (0, 0)),
        pl.BlockSpec((TM, TM), lambda b, i: (0, 0)),
    ]


def _mix0_body(of_ref, ob_ref, z_ref, fn_ref, x_ref, pos_ref, wdn_ref, wfn_ref, on_ref, pv_ref,
               rwt_ref, rb_ref, upper_ref, x_out_ref, h2_ref, mi_ref, mf_ref, cnt_ref, carry_ref):
    o = of_ref[0].astype(F32) + ob_ref[0].astype(F32)
    z = z_ref[0].astype(F32)
    parts = []
    for h in range(HEADS):
        cs = slice(h * DV, (h + 1) * DV)
        parts.append(_rms(o[:, cs]) * on_ref[0:1, cs] * _silu(z[:, cs]))
    dn = jnp.concatenate(parts, axis=1).astype(BF16)
    mix = _dot(dn, wdn_ref[...]) + _dot(fn_ref[0], wfn_ref[...])
    x_new = x_ref[0] + pos_ref[...] + pv_ref[0, 0:1, :] * mix
    _route_tail(x_new, pv_ref, rwt_ref, rb_ref, upper_ref, x_out_ref, h2_ref, mi_ref, mf_ref, cnt_ref,
                carry_ref)


def _mix0(o_f, o_b, z, fn, x, pos, wdn, wfn, onorm8, pvec, rwt, rb, upper):
    bsz, seq, d = x.shape
    nt = seq // TM
    tok = lambda w: pl.BlockSpec((1, TM, w), lambda b, i: (b, i, 0))
    cst = lambda shape: pl.BlockSpec(shape, lambda b, i: (0,) * len(shape))
    return pl.pallas_call(
        _mix0_body,
        out_shape=_route_out_shapes(bsz, seq),
        grid=(bsz, nt),
        in_specs=[tok(V_W), tok(V_W), tok(Z_W), tok(FN_W), tok(d),
                  pl.BlockSpec((TM, d), lambda b, i: (i, 0)),
                  cst((V_W, d)), cst((FN_W, d)), cst((8, V_W)),
                  pl.BlockSpec((1, 8, d), lambda b, i: (b, 0, 0))] + _route_in_specs(),
        out_specs=_route_out_specs(nt),
        scratch_shapes=[pltpu.VMEM((N_EXP, LANES), F32)],
        compiler_params=_cparams(("arbitrary", "arbitrary")),
        name="mix0_router",
    )(o_f, o_b, z, fn, x, pos, wdn, wfn, onorm8, pvec, rwt, rb, upper)


TD = 512


def _dispatch_body(dest_ref, h2_ref, xs_in_ref, xs_ref, sem):
    del xs_in_ref
    base = pl.program_id(0) * TD

    def issue(t, carry):
        for sl in range(2):
            dst = dest_ref[0, 0, sl * TD + t]
            pltpu.make_async_copy(h2_ref.at[pl.ds(base + t, 1)], xs_ref.at[pl.ds(dst, 1)], sem).start()
        return carry

    lax.fori_loop(0, TD, issue, 0)

    def drain(t, carry):
        for sl in range(2):
            pltpu.make_async_copy(h2_ref.at[pl.ds(0, 1)], xs_ref.at[pl.ds(0, 1)], sem).wait()
        return carry

    lax.fori_loop(0, TD, drain, 0)


def _dispatch(dest_tiles, h2, xs_zero):
    t, d = h2.shape
    return pl.pallas_call(
        _dispatch_body,
        out_shape=jax.ShapeDtypeStruct(xs_zero.shape, xs_zero.dtype),
        grid=(t // TD,),
        in_specs=[
            pl.BlockSpec((1, 1, 2 * TD), lambda i: (i, 0, 0), memory_space=pltpu.SMEM),
            pl.BlockSpec(memory_space=pl.ANY),
            pl.BlockSpec(memory_space=pl.ANY),
        ],
        out_specs=pl.BlockSpec(memory_space=pl.ANY),
        scratch_shapes=[pltpu.SemaphoreType.DMA],
        input_output_aliases={2: 0},
        compiler_params=_cparams(("arbitrary",)),
        name="moe_dispatch",
    )(dest_tiles, h2, xs_zero)


def _expert_body(te_ref, nu_ref, xs_ref, wg_ref, wu_ref, wd_ref, ys_ref):
    del te_ref
    used = pl.program_id(0) < nu_ref[0]

    @pl.when(used)
    def _():
        xb = xs_ref[...].astype(BF16)
        hg = _dot(xb, wg_ref[0].astype(BF16))
        hu = _dot(xb, wu_ref[0].astype(BF16))
        hid = (_silu(hg) * hu).astype(BF16)
        ys_ref[...] = _dot(hid, wd_ref[0].astype(BF16))

    @pl.when(jnp.logical_not(used))
    def _():
        ys_ref[...] = jnp.zeros_like(ys_ref)


def _experts(tile_expert, n_used, xs, w_gate, w_up, w_down):
    rows, d = xs.shape
    ntiles = rows // TME
    return pl.pallas_call(
        _expert_body,
        out_shape=jax.ShapeDtypeStruct((rows, d), F32),
        grid_spec=pltpu.PrefetchScalarGridSpec(
            num_scalar_prefetch=2,
            grid=(ntiles,),
            in_specs=[
                pl.BlockSpec((TME, d), lambda i, te, nu: (jnp.minimum(i, nu[0] - 1), 0)),
                pl.BlockSpec((1, d, FF), lambda i, te, nu: (te[i], 0, 0)),
                pl.BlockSpec((1, d, FF), lambda i, te, nu: (te[i], 0, 0)),
                pl.BlockSpec((1, FF, d), lambda i, te, nu: (te[i], 0, 0)),
            ],
            out_specs=pl.BlockSpec((TME, d), lambda i, te, nu: (i, 0)),
        ),
        compiler_params=_cparams(("arbitrary",)),
        name="moe_experts",
    )(tile_expert, n_used, xs, w_gate, w_up, w_down)


def _gather_pairs(dest_ref, ys_ref, buf_ref, sem):
    def issue(t, carry):
        for sl in range(2):
            src = dest_ref[0, 0, sl * TM + t]
            pltpu.make_async_copy(ys_ref.at[pl.ds(src, 1)], buf_ref.at[sl, pl.ds(t, 1)], sem).start()
        return carry

    lax.fori_loop(0, TM, issue, 0)

    def drain(t, carry):
        for sl in range(2):
            pltpu.make_async_copy(ys_ref.at[pl.ds(0, 1)], buf_ref.at[sl, pl.ds(0, 1)], sem).wait()
        return carry

    lax.fori_loop(0, TM, drain, 0)


def _moe_combine(wc_ref, buf_ref):
    return wc_ref[:, 0:1] * buf_ref[0] + wc_ref[:, 1:2] * buf_ref[1]


def _routing_plan(meta_i, counts, tile_tokens):
    t = meta_i.shape[1]
    cnt_perm = counts[:, 0].astype(I32)
    e = jnp.arange(N_EXP)
    cnt = cnt_perm[(e % EXP_PER_GROUP) * N_GROUPS + e // EXP_PER_GROUP]
    tiles_e = (cnt + TME - 1) // TME
    cum_tiles = jnp.cumsum(tiles_e)
    offs = (cum_tiles - tiles_e) * TME
    dest = offs[meta_i[0:2]] + meta_i[2:4]
    ntiles = (2 * t) // TME + N_EXP
    n_used = cum_tiles[-1]
    te = jnp.searchsorted(cum_tiles, jnp.arange(ntiles, dtype=I32), side="right").astype(I32)
    last_e = jnp.max(jnp.where(tiles_e > 0, e, 0)).astype(I32)
    te = jnp.minimum(te, last_e)

    def tiles(n):
        return dest.reshape(2, t // n, n).transpose(1, 0, 2).reshape(t // n, 1, 2 * n)

    return tiles(TD), tiles(tile_tokens), te, n_used.reshape(1).astype(I32), ntiles * TME


def _moe(h2, meta_i, counts, w_gate, w_up, w_down):
    dest_td, dest_tm, te, n_used, rows = _routing_plan(meta_i, counts, TM)
    xs = _dispatch(dest_td, h2, jnp.zeros((rows, h2.shape[1]), F32))
    ys = _experts(te, n_used, xs, w_gate, w_up, w_down)
    return ys, dest_tm


def _conf_in_body(x_ref, dest_ref, wc_ref, ys_ref, pv_ref, w1_ref, b1_ref, x_out_ref, u_ref, buf_ref, sem):
    _gather_pairs(dest_ref, ys_ref, buf_ref, sem)
    x2 = x_ref[0] + pv_ref[0, 0:1, :] * _moe_combine(wc_ref, buf_ref)
    x_out_ref[0] = x2
    h = (_rms(x2) * pv_ref[0, 1:2, :] + pv_ref[0, 2:3, :]).astype(BF16)
    u = _dot(h, w1_ref[...]) + b1_ref[0:1, :]
    u_ref[0] = (u[:, :D] * _sigmoid(u[:, D:])).astype(BF16)


def _conf_in(x1, dest_tm, wcol, ys, pvec, w1, b1):
    bsz, seq, d = x1.shape
    nt = seq // TM
    return pl.pallas_call(
        _conf_in_body,
        out_shape=(jax.ShapeDtypeStruct((bsz, seq, d), F32), jax.ShapeDtypeStruct((bsz, seq, d), BF16)),
        grid=(bsz, nt),
        in_specs=[
            pl.BlockSpec((1, TM, d), lambda b, i: (b, i, 0)),
            pl.BlockSpec((1, 1, 2 * TM), lambda b, i: (b * nt + i, 0, 0), memory_space=pltpu.SMEM),
            pl.BlockSpec((TM, 8), lambda b, i: (b * nt + i, 0)),
            pl.BlockSpec(memory_space=pl.ANY),
            pl.BlockSpec((1, 8, d), lambda b, i: (b, 0, 0)),
            pl.BlockSpec((d, 2 * d), lambda b, i: (0, 0)),
            pl.BlockSpec((8, 2 * d), lambda b, i: (0, 0)),
        ],
        out_specs=(pl.BlockSpec((1, TM, d), lambda b, i: (b, i, 0)),
                   pl.BlockSpec((1, TM, d), lambda b, i: (b, i, 0))),
        scratch_shapes=[pltpu.VMEM((2, TM, d), F32), pltpu.SemaphoreType.DMA],
        compiler_params=_cparams(("arbitrary", "arbitrary")),
        name="combine_conf_in",
    )(x1, dest_tm, wcol, ys, pvec, w1, b1)


def _conf_mix_body(nt, cur_ref, prev_ref, next_ref, dw_ref, cv_ref, x_ref, w2_ref, pv_ref, rwt_ref, rb_ref,
                   upper_ref, x_out_ref, h2_ref, mi_ref, mf_ref, cnt_ref, ext_ref, conv_ref, carry_ref):
    i = pl.program_id(1)
    ext_ref[0:HALO, :] = jnp.where(i == 0, 0.0, prev_ref[0].astype(F32))
    ext_ref[HALO:HALO + TM, :] = cur_ref[0].astype(F32)
    ext_ref[HALO + TM:2 * HALO + TM, :] = jnp.where(i == nt - 1, 0.0, next_ref[0].astype(F32))
    pad = CONF_K // 2
    for cb in range(D // LANES):
        cs = slice(cb * LANES, (cb + 1) * LANES)
        acc = None
        for j in range(CONF_K):
            lo = HALO - pad + j
            term = ext_ref[lo:lo + TM, cs] * dw_ref[j:j + 1, cs]
            acc = term if acc is None else acc + term
        conv_ref[:, cs] = acc + cv_ref[0:1, cs]
    u = conv_ref[...]
    mu = jnp.mean(u, axis=-1, keepdims=True)
    uc = u - mu
    var = jnp.mean(uc * uc, axis=-1, keepdims=True)
    y = uc * lax.rsqrt(var + EPS) * cv_ref[1:2, :] + cv_ref[2:3, :]
    mix = _dot(_silu(y).astype(BF16), w2_ref[...]) + cv_ref[3:4, :]
    x_new = x_ref[0] + pv_ref[0, 0:1, :] * mix
    _route_tail(x_new, pv_ref, rwt_ref, rb_ref, upper_ref, x_out_ref, h2_ref, mi_ref, mf_ref, cnt_ref,
                carry_ref)


def _conf_mix(u, dw32, cvec, x2, w2, pvec, rwt, rb, upper):
    bsz, seq, d = x2.shape
    nt = seq // TM
    r = TM // HALO
    cst = lambda shape: pl.BlockSpec(shape, lambda b, i: (0,) * len(shape))
    return pl.pallas_call(
        functools.partial(_conf_mix_body, nt),
        out_shape=_route_out_shapes(bsz, seq),
        grid=(bsz, nt),
        in_specs=[
            pl.BlockSpec((1, TM, d), lambda b, i: (b, i, 0)),
            pl.BlockSpec((1, HALO, d), lambda b, i: (b, jnp.maximum(i * r - 1, 0), 0)),
            pl.BlockSpec((1, HALO, d), lambda b, i: (b, jnp.minimum((i + 1) * r, nt * r - 1), 0)),
            cst((32, d)), cst((8, d)),
            pl.BlockSpec((1, TM, d), lambda b, i: (b, i, 0)),
            cst((d, d)),
            pl.BlockSpec((1, 8, d), lambda b, i: (b, 0, 0)),
        ] + _route_in_specs(),
        out_specs=_route_out_specs(nt),
        scratch_shapes=[pltpu.VMEM((TM + 2 * HALO, d), F32), pltpu.VMEM((TM, d), F32),
                        pltpu.VMEM((N_EXP, LANES), F32)],
        compiler_params=_cparams(("arbitrary", "arbitrary")),
        name="conf_mix_router",
    )(u, u, u, dw32, cvec, x2, w2, pvec, rwt, rb, upper)


def _final_body(x_ref, dest_ref, wc_ref, ys_ref, pv_ref, o_ref, buf_ref, sem):
    _gather_pairs(dest_ref, ys_ref, buf_ref, sem)
    x4 = x_ref[0] + pv_ref[0, 0:1, :] * _moe_combine(wc_ref, buf_ref)
    o_ref[0] = _rms(x4) * pv_ref[0, 1:2, :]


def _final(x3, dest_tm, wcol, ys, pvec):
    bsz, seq, d = x3.shape
    nt = seq // TM
    return pl.pallas_call(
        _final_body,
        out_shape=jax.ShapeDtypeStruct((bsz, seq, d), F32),
        grid=(bsz, nt),
        in_specs=[
            pl.BlockSpec((1, TM, d), lambda b, i: (b, i, 0)),
            pl.BlockSpec((1, 1, 2 * TM), lambda b, i: (b * nt + i, 0, 0), memory_space=pltpu.SMEM),
            pl.BlockSpec((TM, 8), lambda b, i: (b * nt + i, 0)),
            pl.BlockSpec(memory_space=pl.ANY),
            pl.BlockSpec((1, 8, d), lambda b, i: (b, 0, 0)),
        ],
        out_specs=pl.BlockSpec((1, TM, d), lambda b, i: (b, i, 0)),
        scratch_shapes=[pltpu.VMEM((2, TM, d), F32), pltpu.SemaphoreType.DMA],
        compiler_params=_cparams(("arbitrary", "arbitrary")),
        name="combine_final",
    )(x3, dest_tm, wcol, ys, pvec)


def _grid_sincos(rows, cols, dim):
    quarter = dim // 4
    omega = 1.0 / jnp.power(POS_BASE, jnp.arange(quarter, dtype=F32) / quarter)

    def axis_emb(n):
        ang = jnp.arange(n, dtype=F32)[:, None] * omega[None, :]
        return jnp.concatenate([jnp.sin(ang), jnp.cos(ang)], axis=-1)

    er = jnp.broadcast_to(axis_emb(rows)[:, None, :], (rows, cols, dim // 2))
    ec = jnp.broadcast_to(axis_emb(cols)[None, :, :], (rows, cols, dim // 2))
    return jnp.concatenate([er, ec], axis=-1).reshape(rows * cols, dim)


def _pad_rows(a, rows):
    return jnp.pad(a, ((0, rows - a.shape[0]),) + ((0, 0),) * (a.ndim - 1))


def _per_batch(*vecs):
    st = jnp.stack(vecs, axis=1)
    return jnp.pad(st, ((0, 0), (0, 8 - st.shape[1]), (0, 0)))


def _width_dft():
    k = np.arange(LANES)
    ang = 2 * np.pi * np.outer(k, k) / LANES
    groups = FN_W // LANES
    c = np.kron(np.eye(groups), np.cos(ang)) / math.sqrt(LANES)
    s = np.kron(np.eye(groups), np.sin(ang)) / math.sqrt(LANES)
    return jnp.asarray(np.concatenate([c, s], axis=1), dtype=BF16)


def kernel(x, c, ctx, c_ctx, ada_w, ada_b, norm1_g, norm2_g, hyb_w_in, dn_conv_w, dn_a_log, dn_dt_bias,
           dn_onorm_g, hyb_w_out, conf_w1, conf_b1, conf_dw_w, conf_dw_b, conf_ln_g, conf_ln_b, conf_w2,
           conf_b2, router_w, router_bias, moe_w_gate, moe_w_up, moe_w_down, final_g):
    bsz, seq, d = x.shape
    ctx_len = ctx.shape[1]
    assert d == D and seq % TM == 0 and ctx_len % TM == 0 and (bsz * seq) % TD == 0
    nct = ctx_len // TM
    t = bsz * seq

    pos = _grid_sincos(seq // GRID_W, GRID_W, d)

    cpad = _pad_rows(jnp.concatenate([c, c_ctx[None, :]], axis=0), 16)
    mods = _adaln(cpad, ada_w, ada_b)
    m = lambda l, k: mods[l, :, k * d:(k + 1) * d]
    a1 = norm1_g[0][None, :] * (1.0 + m(0, 1))
    mod_a = jnp.stack([jnp.broadcast_to(a1[bsz], (bsz, d)), a1[:bsz]], axis=1)
    mod_b = jnp.stack([jnp.broadcast_to(m(0, 0)[bsz], (bsz, d)), m(0, 0)[:bsz]], axis=1)

    w_in = hyb_w_in[0]
    z_lo = QKV_W + AB_W
    f_lo = z_lo + Z_W
    wqkv = w_in[:, :QKV_W].astype(BF16)
    wab = jnp.pad(w_in[:, QKV_W:z_lo], ((0, 0), (0, LANES - AB_W))).astype(BF16)
    wz = w_in[:, z_lo:f_lo].astype(BF16)
    wf = w_in[:, f_lo:].astype(BF16)
    gpar = jnp.zeros((8, LANES), F32)
    gpar = gpar.at[0, :2 * HEADS].set(-jnp.exp(dn_a_log[0].reshape(-1)))
    gpar = gpar.at[1, :2 * HEADS].set(dn_dt_bias[0].reshape(-1))
    qkv_pre, z, fc, fs, gates = _inproj(ctx, x, pos, mod_a, mod_b, wqkv, wz, wf, wab, _width_dft(), gpar)

    qkv = _dnconv(qkv_pre, _pad_rows(dn_conv_w[0], 8), nct)
    lc = qkv.shape[1]
    nchunks = lc // CHUNK
    grow = gates[:, :, :AB_W].reshape(bsz, nchunks, CHUNK, 4, HEADS)
    grow = grow.transpose(0, 1, 3, 4, 2).reshape(bsz, nchunks, 4, HB)
    grow = jnp.pad(grow, ((0, 0), (0, 0), (0, 4), (0, 0)))
    o_f, o_b = _deltanet(qkv, grow, seq, ctx_len // CHUNK)

    fn = _seq_dft(fc, fs)

    perm = (np.arange(N_EXP) % N_GROUPS) * EXP_PER_GROUP + np.arange(N_EXP) // N_GROUPS
    rwt = router_w.T[perm].astype(BF16)
    rb = jnp.broadcast_to(router_bias[perm][:, None], (N_EXP, LANES)).astype(F32)
    upper = jnp.asarray(np.triu(np.ones((TM, TM), np.float32), 1), dtype=BF16)
    w_out = hyb_w_out[0].astype(BF16)
    onorm8 = _pad_rows(jnp.tile(dn_onorm_g[0], HEADS)[None, :], 8)
    pv0 = _per_batch(m(0, 2)[:bsz], norm2_g[0][None, :] * (1.0 + m(0, 4)[:bsz]), m(0, 3)[:bsz])
    x1, h2, meta_i, meta_f, counts = _mix0(o_f, o_b, z, fn, x, pos, w_out[:V_W], w_out[V_W:], onorm8, pv0,
                                           rwt, rb, upper)
    ys, dest_tm = _moe(h2, meta_i, counts, moe_w_gate[0], moe_w_up[0], moe_w_down[0])
    wcol = meta_f.T

    pv1 = _per_batch(m(0, 5)[:bsz], norm1_g[1][None, :] * (1.0 + m(1, 1)[:bsz]), m(1, 0)[:bsz])
    x2, u = _conf_in(x1, dest_tm, wcol, ys, pv1, conf_w1[0].astype(BF16), _pad_rows(conf_b1[0][None, :], 8))
    cvec = _pad_rows(jnp.stack([conf_dw_b[0], conf_ln_g[0], conf_ln_b[0], conf_b2[0]]), 8)
    pv2 = _per_batch(m(1, 2)[:bsz], norm2_g[1][None, :] * (1.0 + m(1, 4)[:bsz]), m(1, 3)[:bsz])
    x3, h2, meta_i, meta_f, counts = _conf_mix(u, _pad_rows(conf_dw_w[0], 32), cvec, x2,
                                               conf_w2[0].astype(BF16), pv2, rwt, rb, upper)
    ys, dest_tm = _moe(h2, meta_i, counts, moe_w_gate[1], moe_w_up[1], moe_w_down[1])
    pv3 = _per_batch(m(1, 5)[:bsz], jnp.broadcast_to(final_g[None, :], (bsz, d)))
    return _final(x3, dest_tm, meta_f.T, ys, pv3)
```

```python
import functools
import math

import numpy as np
import jax
import jax.numpy as jnp
from jax import lax
from jax.experimental import pallas as pl
from jax.experimental.pallas import tpu as pltpu

F32 = jnp.float32
BF16 = jnp.bfloat16
I32 = jnp.int32

D = 1024
GRID_W = 64
HEADS = 4
DK = 128
DV = 128
CHUNK = 64
QK_W = HEADS * DK
V_W = HEADS * DV
QKV_W = 2 * QK_W + V_W
AB_W = 16
Z_W = V_W
FN_W = 512
DN_CONV_K = 5
CONF_K = 31
N_EXP = 32
N_GROUPS = 8
EXP_PER_GROUP = 4
FF = 512
EPS = 1e-6
POS_BASE = 10000.0

LANES = 128
HALO = 16
TM = 256
TME = 256
HB = HEADS * CHUNK
DN_GROUP = 4
CONF_SPAN = 24
VMEM_LIMIT = 56 * 1024 * 1024


def _cparams(sem):
    return pltpu.CompilerParams(dimension_semantics=sem, vmem_limit_bytes=VMEM_LIMIT)


def _sigmoid(v):
    return jax.nn.sigmoid(v)


def _silu(v):
    return v * jax.nn.sigmoid(v)


def _dot(a, b):
    return jnp.dot(a, b, preferred_element_type=F32)


def _dot_nt(a, b):
    return lax.dot_general(a, b, (((1,), (1,)), ((), ())), preferred_element_type=F32)


def _split3(a):
    a1 = a.astype(BF16)
    r1 = a - a1.astype(F32)
    a2 = r1.astype(BF16)
    a3 = (r1 - a2.astype(F32)).astype(BF16)
    return a1, a2, a3


def _rms(v, eps=EPS):
    return v * lax.rsqrt(jnp.mean(v * v, axis=-1, keepdims=True) + eps)


def _adaln_body(c_ref, w_ref, b_ref, o_ref):
    s = _silu(c_ref[...])
    o_ref[0] = jnp.dot(s, w_ref[0], preferred_element_type=F32,
                       precision=lax.Precision.HIGHEST) + b_ref[0]


def _adaln(cpad, ada_w, ada_b):
    depth, d, n = ada_w.shape
    tn = 1536
    rows = cpad.shape[0]
    return pl.pallas_call(
        _adaln_body,
        out_shape=jax.ShapeDtypeStruct((depth, rows, n), F32),
        grid=(depth, n // tn),
        in_specs=[
            pl.BlockSpec((rows, d), lambda l, j: (0, 0)),
            pl.BlockSpec((1, d, tn), lambda l, j: (l, 0, j)),
            pl.BlockSpec((1, 1, tn), lambda l, j: (l, 0, j)),
        ],
        out_specs=pl.BlockSpec((1, rows, tn), lambda l, j: (l, 0, j)),
        compiler_params=_cparams(("arbitrary", "arbitrary")),
        name="adaln",
    )(cpad, ada_w, ada_b.reshape(depth, 1, n))


def _inproj_body(nct, ctx_ref, x_ref, pos_ref, ma_ref, mb_ref, wqkv_ref, wz_ref, wf_ref, wab_ref,
                 dftw_ref, gpar_ref, qkv_ref, z_ref, fc_ref, fs_ref, gate_ref):
    i = pl.program_id(1)
    is_ctx = i < nct
    xin = jnp.where(is_ctx, ctx_ref[0], x_ref[0] + pos_ref[...])
    a = jnp.where(is_ctx, ma_ref[0, 0:1, :], ma_ref[0, 1:2, :])
    b = jnp.where(is_ctx, mb_ref[0, 0:1, :], mb_ref[0, 1:2, :])
    h = (_rms(xin) * a + b).astype(BF16)
    qkv_ref[0] = _dot(h, wqkv_ref[...]).astype(BF16)
    z_ref[0] = _dot(h, wz_ref[...]).astype(BF16)
    f = _dot(h, wf_ref[...]).astype(BF16)
    fcs = _dot(f, dftw_ref[...])
    fc_ref[0] = fcs[:, :FN_W].astype(BF16)
    fs_ref[0] = fcs[:, FN_W:].astype(BF16)
    ab = _dot(h, wab_ref[...])
    lane = lax.broadcasted_iota(I32, ab.shape, 1)
    pre = ab + gpar_ref[1:2, :]
    softplus = jnp.maximum(pre, 0.0) + jnp.log1p(jnp.exp(-jnp.abs(pre)))
    gate_ref[0] = jnp.where(lane < 2 * HEADS, gpar_ref[0:1, :] * softplus, _sigmoid(ab))


def _inproj(ctx, x, pos, mod_a, mod_b, wqkv, wz, wf, wab, dftw, gpar):
    bsz, seq, d = x.shape
    nct = ctx.shape[1] // TM
    nt = nct + seq // TM
    lc = nt * TM
    xi = lambda b, i: (b, jnp.maximum(i - nct, 0), 0)
    full = lambda shape: pl.BlockSpec(shape, lambda b, i: (0,) * len(shape))
    return pl.pallas_call(
        functools.partial(_inproj_body, nct),
        out_shape=(
            jax.ShapeDtypeStruct((bsz, lc, QKV_W), BF16),
            jax.ShapeDtypeStruct((bsz, seq, Z_W), BF16),
            jax.ShapeDtypeStruct((bsz, seq, FN_W), BF16),
            jax.ShapeDtypeStruct((bsz, seq, FN_W), BF16),
            jax.ShapeDtypeStruct((bsz, lc, LANES), F32),
        ),
        grid=(bsz, nt),
        in_specs=[
            pl.BlockSpec((1, TM, d), lambda b, i: (b, jnp.minimum(i, nct - 1), 0)),
            pl.BlockSpec((1, TM, d), xi),
            pl.BlockSpec((TM, d), lambda b, i: (jnp.maximum(i - nct, 0), 0)),
            pl.BlockSpec((1, 2, d), lambda b, i: (b, 0, 0)),
            pl.BlockSpec((1, 2, d), lambda b, i: (b, 0, 0)),
            full((d, QKV_W)), full((d, Z_W)), full((d, FN_W)), full((d, LANES)),
            full((FN_W, 2 * FN_W)), full((8, LANES)),
        ],
        out_specs=(
            pl.BlockSpec((1, TM, QKV_W), lambda b, i: (b, i, 0)),
            pl.BlockSpec((1, TM, Z_W), xi),
            pl.BlockSpec((1, TM, FN_W), xi),
            pl.BlockSpec((1, TM, FN_W), xi),
            pl.BlockSpec((1, TM, LANES), lambda b, i: (b, i, 0)),
        ),
        compiler_params=_cparams(("arbitrary", "arbitrary")),
        name="inproj",
    )(ctx, x, pos, mod_a, mod_b, wqkv, wz, wf, wab, dftw, gpar)


def _dnconv_body(nct, nt, cur_ref, prev_ref, next_ref, w_ref, o_ref, ext_ref):
    i = pl.program_id(1)
    first = jnp.logical_or(i == 0, i == nct)
    last = jnp.logical_or(i == nct - 1, i == nt - 1)
    ext_ref[0:HALO, :] = jnp.where(first, 0.0, prev_ref[0].astype(F32))
    ext_ref[HALO:HALO + TM, :] = cur_ref[0].astype(F32)
    ext_ref[HALO + TM:2 * HALO + TM, :] = jnp.where(last, 0.0, next_ref[0].astype(F32))
    pad = DN_CONV_K // 2
    for cb in range(QKV_W // LANES):
        cs = slice(cb * LANES, (cb + 1) * LANES)
        acc = None
        for j in range(DN_CONV_K):
            lo = HALO - pad + j
            term = ext_ref[lo:lo + TM, cs] * w_ref[j:j + 1, cs]
            acc = term if acc is None else acc + term
        y = _silu(acc)
        if cb < 2 * HEADS:
            y = y * lax.rsqrt(jnp.sum(y * y, axis=-1, keepdims=True) + EPS)
        o_ref[0, :, cs] = y.astype(BF16)


def _dnconv(qkv_pre, conv_w8, nct):
    bsz, lc, w = qkv_pre.shape
    nt = lc // TM
    r = TM // HALO
    return pl.pallas_call(
        functools.partial(_dnconv_body, nct, nt),
        out_shape=jax.ShapeDtypeStruct((bsz, lc, w), BF16),
        grid=(bsz, nt),
        in_specs=[
            pl.BlockSpec((1, TM, w), lambda b, i: (b, i, 0)),
            pl.BlockSpec((1, HALO, w), lambda b, i: (b, jnp.maximum(i * r - 1, 0), 0)),
            pl.BlockSpec((1, HALO, w), lambda b, i: (b, jnp.minimum((i + 1) * r, nt * r - 1), 0)),
            pl.BlockSpec((8, w), lambda b, i: (0, 0)),
        ],
        out_specs=pl.BlockSpec((1, TM, w), lambda b, i: (b, i, 0)),
        scratch_shapes=[pltpu.VMEM((TM + 2 * HALO, w), F32)],
        compiler_params=_cparams(("arbitrary", "arbitrary")),
        name="dnconv",
    )(qkv_pre, qkv_pre, qkv_pre, conv_w8)


def _dn_setup(d, qkv, grow, mask_incl, mask_strict, tri, trit, bdones, eye, lvl0):
    neg_inf = float("-inf")
    stack = lambda off: jnp.concatenate(
        [qkv[:, off + h * DK: off + (h + 1) * DK] for h in range(HEADS)], axis=0)
    q_s = stack(0)
    k_s = stack(QK_W)
    v_s = stack(2 * QK_W)
    g3 = _split3(grow)
    cg_row = sum(_dot(t, trit) for t in g3)
    lhs = jnp.concatenate([tri, bdones, eye], axis=0)
    cols = sum(_dot_nt(lhs, t) for t in g3)
    cgc = cols[0:HB, d:d + 1]
    totc = cols[HB:2 * HB, d:d + 1]
    betac = cols[2 * HB:3 * HB, 2 + d:3 + d]
    cgr = cg_row[d:d + 1, :]
    dmat = jnp.exp(jnp.where(mask_incl > 0.5, cgc - cgr, neg_inf))
    kk = _dot_nt(k_s, k_s)
    qk = _dot_nt(q_s, k_s)
    a = jnp.where(mask_strict > 0.5, kk * dmat, 0.0) * betac
    ecg = jnp.exp(cgc)
    k32 = k_s.astype(F32)
    scale = DK ** -0.5
    return dict(
        d=d, a=a, x=eye.astype(F32) - a * lvl0, totc=totc,
        rhs=jnp.concatenate([(betac * v_s.astype(F32)).astype(BF16),
                             ((betac * ecg) * k32).astype(BF16)], axis=1),
        ktail=(k32 * jnp.exp(totc - cgc)).astype(BF16),
        qdec=(q_s.astype(F32) * (ecg * scale)).astype(BF16),
        qkd=(qk * dmat * scale).astype(BF16))


def _dn_body(qf_ref, qb_ref, gf_ref, gb_ref, mli_ref, mls_ref, mui_ref, mus_ref, bdl_ref, bdu_ref,
             bdo_ref, eye_ref, lvl_ref, of_ref, ob_ref, s_ref):
    @pl.when(pl.program_id(1) == 0)
    def _():
        s_ref[...] = jnp.zeros_like(s_ref)

    bdl = bdl_ref[...]
    bdu = bdu_ref[...]
    bdo = bdo_ref[...]
    eye = eye_ref[...]
    chains = []
    for g in range(qf_ref.shape[0]):
        c = _dn_setup(0, qf_ref[g], gf_ref[g, 0], mli_ref[...], mls_ref[...], bdl, bdu, bdo, eye, lvl_ref[0])
        c.update(g=g, out=of_ref)
        chains.append(c)
        c = _dn_setup(1, qb_ref[g], gb_ref[g, 0], mui_ref[...], mus_ref[...], bdu, bdl, bdo, eye, lvl_ref[0])
        c.update(g=g, out=ob_ref)
        chains.append(c)
    for lvl in range(1, int(math.log2(CHUNK))):
        for c in chains:
            c["xb"] = c["x"].astype(BF16)
            c["y"] = _dot((c["a"] * lvl_ref[lvl]).astype(BF16), c["xb"]).astype(BF16)
        for c in chains:
            c["x"] = c["x"] - _dot(c["xb"], c["y"])
    for c in chains:
        uw = _dot(c["x"].astype(BF16), c["rhs"])
        c["u"] = uw[:, :DV]
        c["w"] = uw[:, DV:].astype(BF16)
    for c in chains:
        c["vn"] = []
        c["oi"] = []
    for h in range(HEADS):
        rs = slice(h * CHUNK, (h + 1) * CHUNK)
        for c in chains:
            s = s_ref[c["g"], c["d"], h]
            sb = s.astype(BF16)
            c["s"] = s
            c["vn"].append(c["u"][rs] - _dot(c["w"][rs], sb))
            c["oi"].append(_dot(c["qdec"][rs], sb))
        for c in chains:
            cd = jnp.exp(c["totc"][h * CHUNK:h * CHUNK + 1, :])
            upd = lax.dot_general(c["ktail"][rs], c["vn"][h].astype(BF16), (((0,), (0,)), ((), ())),
                                  preferred_element_type=F32)
            s_ref[c["g"], c["d"], h] = c["s"] * cd + upd
    for c in chains:
        vn_s = jnp.concatenate(c["vn"], axis=0).astype(BF16)
        o_s = jnp.concatenate(c["oi"], axis=0) + _dot(c["qkd"], vn_s)
        c["out"][c["g"]] = jnp.concatenate(
            [o_s[h * CHUNK:(h + 1) * CHUNK] for h in range(HEADS)], axis=1).astype(BF16)


def _dn_consts():
    r = np.arange(HB)
    same = (r[:, None] // CHUNK) == (r[None, :] // CHUNK)
    li = same & (r[:, None] >= r[None, :])
    ls = same & (r[:, None] > r[None, :])
    ui = same & (r[:, None] <= r[None, :])
    us = same & (r[:, None] < r[None, :])
    f = lambda m: jnp.asarray(m.astype(np.float32))
    h = lambda m: jnp.asarray(m.astype(np.float32), dtype=BF16)
    nlev = int(math.log2(CHUNK))
    lv = [((r[:, None] >> (l + 1)) == (r[None, :] >> (l + 1))) & ((r[:, None] >> l) != (r[None, :] >> l))
          for l in range(nlev)]
    return (f(li), f(ls), f(ui), f(us), h(li), h(ui), h(same), h(np.eye(HB, dtype=bool)),
            jnp.asarray(np.stack(lv).astype(np.float32)))


def _deltanet(qkv, grow, seq, ncc):
    bsz, lc, w = qkv.shape
    nchunks = lc // CHUNK
    nx = seq // CHUNK
    fwd = lambda s: s
    bwd = lambda s: jnp.where(s < ncc, ncc - 1 - s, ncc + nchunks - 1 - s)
    ofi = lambda b, s: (b, jnp.maximum(s - ncc, 0), 0)
    obi = lambda b, s: (b, jnp.minimum(nchunks - 1 - s, nx - 1), 0)
    cst = lambda shape: pl.BlockSpec(shape, lambda b, s: (0, 0))
    consts = _dn_consts()
    grp = math.gcd(bsz, DN_GROUP)
    return pl.pallas_call(
        _dn_body,
        out_shape=(jax.ShapeDtypeStruct((bsz, seq, V_W), BF16),
                   jax.ShapeDtypeStruct((bsz, seq, V_W), BF16)),
        grid=(bsz // grp, nchunks),
        in_specs=[
            pl.BlockSpec((grp, CHUNK, w), lambda b, s: (b, fwd(s), 0)),
            pl.BlockSpec((grp, CHUNK, w), lambda b, s: (b, bwd(s), 0)),
            pl.BlockSpec((grp, 1, 8, HB), lambda b, s: (b, fwd(s), 0, 0)),
            pl.BlockSpec((grp, 1, 8, HB), lambda b, s: (b, bwd(s), 0, 0)),
        ] + [cst((HB, HB))] * 8 + [pl.BlockSpec((int(math.log2(CHUNK)), HB, HB), lambda b, s: (0, 0, 0))],
        out_specs=(pl.BlockSpec((grp, CHUNK, V_W), ofi), pl.BlockSpec((grp, CHUNK, V_W), obi)),
        scratch_shapes=[pltpu.VMEM((grp, 2, HEADS, DK, DV), F32)],
        compiler_params=_cparams(("arbitrary", "arbitrary")),
        name="deltanet",
    )(qkv, qkv, grow, grow, *consts)


def _fft1_body(ns2, fc_ref, fs_ref, m_ref, cos_ref, sin_ref, y_ref):
    n1 = fc_ref.shape[1]
    xin = jnp.concatenate([fc_ref[0], fs_ref[0]], axis=0)
    y = _dot(m_ref[...], xin)
    yr, yi = y[:n1], y[n1:]
    for q in range(ns2):
        cs = slice(q * FN_W, (q + 1) * FN_W)
        c = jnp.concatenate([cos_ref[q]] * (FN_W // LANES), axis=1)
        s = jnp.concatenate([sin_ref[q]] * (FN_W // LANES), axis=1)
        y_ref[0, 0, :, cs] = (yr[:, cs] * c + yi[:, cs] * s).astype(BF16)
        y_ref[0, 1, :, cs] = (yi[:, cs] * c - yr[:, cs] * s).astype(BF16)


def _fft2_body(ns1, y_ref, cs_ref, o_ref):
    for q in range(ns1):
        rhs = jnp.concatenate([y_ref[0, 0, q], y_ref[0, 1, q]], axis=0)
        o_ref[0, :, q * FN_W:(q + 1) * FN_W] = _dot(cs_ref[...], rhs).astype(BF16)


def _seq_dft(fc, fs):
    bsz, seq, w = fc.shape
    n2 = GRID_W
    n1 = seq // n2
    k1 = np.arange(n1)
    ang1 = 2 * np.pi * np.outer(k1, k1) / n1
    c1, s1 = np.cos(ang1), np.sin(ang1)
    m1 = jnp.asarray(np.block([[c1, -s1], [-s1, -c1]]), dtype=BF16)
    tw = 2 * np.pi * np.outer(np.arange(n2), k1) / seq
    cos_t = jnp.asarray(np.repeat(np.cos(tw)[:, :, None], LANES, axis=2), dtype=F32)
    sin_t = jnp.asarray(np.repeat(np.sin(tw)[:, :, None], LANES, axis=2), dtype=F32)
    k2 = np.arange(n2)
    ang2 = 2 * np.pi * np.outer(k2, k2) / n2
    cs2 = jnp.asarray(np.concatenate([np.cos(ang2), np.sin(ang2)], axis=1) / math.sqrt(seq), dtype=BF16)

    ns2 = 4
    y = pl.pallas_call(
        functools.partial(_fft1_body, ns2),
        out_shape=jax.ShapeDtypeStruct((bsz, 2, n1, n2 * w), BF16),
        grid=(bsz, n2 // ns2),
        in_specs=[
            pl.BlockSpec((1, n1, ns2 * w), lambda b, j: (b, 0, j)),
            pl.BlockSpec((1, n1, ns2 * w), lambda b, j: (b, 0, j)),
            pl.BlockSpec((2 * n1, 2 * n1), lambda b, j: (0, 0)),
            pl.BlockSpec((ns2, n1, LANES), lambda b, j: (j, 0, 0)),
            pl.BlockSpec((ns2, n1, LANES), lambda b, j: (j, 0, 0)),
        ],
        out_specs=pl.BlockSpec((1, 2, n1, ns2 * w), lambda b, j: (b, 0, 0, j)),
        compiler_params=_cparams(("arbitrary", "arbitrary")),
        name="seqdft_stage1",
    )(fc.reshape(bsz, n1, n2 * w), fs.reshape(bsz, n1, n2 * w), m1, cos_t, sin_t)

    ns1 = min(8, n1)
    out = pl.pallas_call(
        functools.partial(_fft2_body, ns1),
        out_shape=jax.ShapeDtypeStruct((bsz, n2, n1 * w), BF16),
        grid=(bsz, n1 // ns1),
        in_specs=[
            pl.BlockSpec((1, 2, ns1, n2, w), lambda b, j: (b, 0, j, 0, 0)),
            pl.BlockSpec((n2, 2 * n2), lambda b, j: (0, 0)),
        ],
        out_specs=pl.BlockSpec((1, n2, ns1 * w), lambda b, j: (b, 0, j)),
        compiler_params=_cparams(("arbitrary", "arbitrary")),
        name="seqdft_stage2",
    )(y.reshape(bsz, 2, n1, n2, w), cs2)
    return out.reshape(bsz, seq, w)


def _route_tail(x_new, pv_ref, rwt_ref, rb_ref, upper_ref, x_out_ref, h2_ref, mi_ref, mf_ref, cnt_ref,
                carry_ref):
    first = jnp.logical_and(pl.program_id(0) == 0, pl.program_id(1) == 0)

    @pl.when(first)
    def _():
        carry_ref[...] = jnp.zeros_like(carry_ref)

    x_out_ref[0] = x_new
    h2 = _rms(x_new) * pv_ref[0, 1:2, :] + pv_ref[0, 2:3, :]
    h2_ref[...] = h2
    logits = _dot_nt(rwt_ref[...], h2.astype(BF16))
    scores = _sigmoid(logits)
    sel = scores + rb_ref[:, 0:1]
    a = [sel[N_GROUPS * j:N_GROUPS * (j + 1)] for j in range(EXP_PER_GROUP)]
    sc = [scores[N_GROUPS * j:N_GROUPS * (j + 1)] for j in range(EXP_PER_GROUP)]
    gs = None
    for j0 in range(EXP_PER_GROUP):
        for j1 in range(j0 + 1, EXP_PER_GROUP):
            pair = a[j0] + a[j1]
            gs = pair if gs is None else jnp.maximum(gs, pair)
    gidx = lax.broadcasted_iota(I32, gs.shape, 0).astype(F32)
    gmax = jnp.max(gs, axis=0, keepdims=True)
    best = jnp.min(jnp.where(gs == gmax, gidx, float(N_GROUPS)), axis=0, keepdims=True)
    inb = gidx == best
    v = [jnp.sum(jnp.where(inb, a[j], 0.0), axis=0, keepdims=True) for j in range(EXP_PER_GROUP)]
    s = [jnp.sum(jnp.where(inb, sc[j], 0.0), axis=0, keepdims=True) for j in range(EXP_PER_GROUP)]
    v1, i1, s1 = v[0], jnp.zeros_like(v[0]), s[0]
    for j in range(1, EXP_PER_GROUP):
        better = v[j] > v1
        v1 = jnp.where(better, v[j], v1)
        i1 = jnp.where(better, float(j), i1)
        s1 = jnp.where(better, s[j], s1)
    v2 = jnp.full_like(v1, float("-inf"))
    i2 = jnp.zeros_like(v1)
    s2 = jnp.zeros_like(v1)
    for j in range(EXP_PER_GROUP):
        better = jnp.logical_and(i1 != float(j), v[j] > v2)
        v2 = jnp.where(better, v[j], v2)
        i2 = jnp.where(better, float(j), i2)
        s2 = jnp.where(better, s[j], s2)
    denom = s1 + s2
    rid = lax.broadcasted_iota(I32, logits.shape, 0).astype(F32)
    oh1 = rid == (i1 * N_GROUPS + best)
    oh2 = rid == (i2 * N_GROUPS + best)
    oh = jnp.where(jnp.logical_or(oh1, oh2), 1.0, 0.0)
    prefix = _dot(oh.astype(BF16), upper_ref[...])
    base = carry_ref[:, 0:1] + prefix
    rank1 = jnp.sum(jnp.where(oh1, base, 0.0), axis=0, keepdims=True)
    rank2 = jnp.sum(jnp.where(oh2, base, 0.0), axis=0, keepdims=True)
    carry_new = carry_ref[...] + jnp.sum(oh, axis=1, keepdims=True)
    carry_ref[...] = carry_new
    cnt_ref[...] = carry_new
    mi_ref[...] = jnp.zeros_like(mi_ref)
    mi_ref[0:1, :] = (best * EXP_PER_GROUP + i1).astype(I32)
    mi_ref[1:2, :] = (best * EXP_PER_GROUP + i2).astype(I32)
    mi_ref[2:3, :] = rank1.astype(I32)
    mi_ref[3:4, :] = rank2.astype(I32)
    mf_ref[...] = jnp.zeros_like(mf_ref)
    mf_ref[0:1, :] = s1 / denom
    mf_ref[1:2, :] = s2 / denom


def _route_out_shapes(bsz, seq):
    t = bsz * seq
    return (
        jax.ShapeDtypeStruct((bsz, seq, D), F32),
        jax.ShapeDtypeStruct((t, D), F32),
        jax.ShapeDtypeStruct((8, t), I32),
        jax.ShapeDtypeStruct((8, t), F32),
        jax.ShapeDtypeStruct((N_EXP, LANES), F32),
    )


def _route_out_specs(nt):
    flat = lambda b, i: (b * nt + i, 0)
    return (
        pl.BlockSpec((1, TM, D), lambda b, i: (b, i, 0)),
        pl.BlockSpec((TM, D), flat),
        pl.BlockSpec((8, TM), lambda b, i: (0, b * nt + i)),
        pl.BlockSpec((8, TM), lambda b, i: (0, b * nt + i)),
        pl.BlockSpec((N_EXP, LANES), lambda b, i: (0, 0)),
    )


def _route_in_specs():
    return [
        pl.BlockSpec((N_EXP, D), lambda b, i: (0, 0)),
        pl.BlockSpec((N_EXP, LANES), lambda b, i: (0, 0)),
        pl.BlockSpec((TM, TM), lambda b, i: (0, 0)),
    ]


def _mix0_body(of_ref, ob_ref, z_ref, fn_ref, x_ref, pos_ref, wdn_ref, wfn_ref, on_ref, pv_ref,
               rwt_ref, rb_ref, upper_ref, x_out_ref, h2_ref, mi_ref, mf_ref, cnt_ref, carry_ref):
    o = of_ref[0].astype(F32) + ob_ref[0].astype(F32)
    z = z_ref[0].astype(F32)
    parts = []
    for h in range(HEADS):
        cs = slice(h * DV, (h + 1) * DV)
        parts.append(_rms(o[:, cs]) * on_ref[0:1, cs] * _silu(z[:, cs]))
    dn = jnp.concatenate(parts, axis=1).astype(BF16)
    mix = _dot(dn, wdn_ref[...]) + _dot(fn_ref[0], wfn_ref[...])
    x_new = x_ref[0] + pos_ref[...] + pv_ref[0, 0:1, :] * mix
    _route_tail(x_new, pv_ref, rwt_ref, rb_ref, upper_ref, x_out_ref, h2_ref, mi_ref, mf_ref, cnt_ref,
                carry_ref)


def _mix0(o_f, o_b, z, fn, x, pos, wdn, wfn, onorm8, pvec, rwt, rb, upper):
    bsz, seq, d = x.shape
    nt = seq // TM
    tok = lambda w: pl.BlockSpec((1, TM, w), lambda b, i: (b, i, 0))
    cst = lambda shape: pl.BlockSpec(shape, lambda b, i: (0,) * len(shape))
    return pl.pallas_call(
        _mix0_body,
        out_shape=_route_out_shapes(bsz, seq),
        grid=(bsz, nt),
        in_specs=[tok(V_W), tok(V_W), tok(Z_W), tok(FN_W), tok(d),
                  pl.BlockSpec((TM, d), lambda b, i: (i, 0)),
                  cst((V_W, d)), cst((FN_W, d)), cst((8, V_W)),
                  pl.BlockSpec((1, 8, d), lambda b, i: (b, 0, 0))] + _route_in_specs(),
        out_specs=_route_out_specs(nt),
        scratch_shapes=[pltpu.VMEM((N_EXP, LANES), F32)],
        compiler_params=_cparams(("arbitrary", "arbitrary")),
        name="mix0_router",
    )(o_f, o_b, z, fn, x, pos, wdn, wfn, onorm8, pvec, rwt, rb, upper)


TD = 512
UNROLL = 8


def _dispatch_body(dest_ref, h2_ref, xs_in_ref, xs_ref, sem):
    del xs_in_ref

    def issue(g, carry):
        for u in range(UNROLL):
            t = g * UNROLL + u
            for sl in range(2):
                dst = dest_ref[0, 0, sl * TD + t]
                pltpu.make_async_copy(h2_ref.at[pl.ds(t, 1)], xs_ref.at[pl.ds(dst, 1)], sem).start()
        return carry

    lax.fori_loop(0, TD // UNROLL, issue, 0)
    for sl in range(2):
        pltpu.make_async_copy(h2_ref, xs_ref.at[pl.ds(0, TD)], sem).wait()


def _dispatch(dest_tiles, h2, xs_zero):
    t, d = h2.shape
    return pl.pallas_call(
        _dispatch_body,
        out_shape=jax.ShapeDtypeStruct(xs_zero.shape, xs_zero.dtype),
        grid=(t // TD,),
        in_specs=[
            pl.BlockSpec((1, 1, 2 * TD), lambda i: (i, 0, 0), memory_space=pltpu.SMEM),
            pl.BlockSpec((TD, d), lambda i: (i, 0)),
            pl.BlockSpec(memory_space=pl.ANY),
        ],
        out_specs=pl.BlockSpec(memory_space=pl.ANY),
        scratch_shapes=[pltpu.SemaphoreType.DMA],
        input_output_aliases={2: 0},
        compiler_params=_cparams(("arbitrary",)),
        name="moe_dispatch",
    )(dest_tiles, h2, xs_zero)


def _expert_body(te_ref, nu_ref, xs_ref, wg_ref, wu_ref, wd_ref, ys_ref, wgb_ref, wub_ref, wdb_ref):
    i = pl.program_id(0)
    used = i < nu_ref[0]
    new_expert = jnp.logical_or(i == 0, te_ref[i] != te_ref[jnp.maximum(i - 1, 0)])

    @pl.when(jnp.logical_and(used, new_expert))
    def _():
        wgb_ref[...] = wg_ref[0].astype(BF16)
        wub_ref[...] = wu_ref[0].astype(BF16)
        wdb_ref[...] = wd_ref[0].astype(BF16)

    @pl.when(used)
    def _():
        xb = xs_ref[...].astype(BF16)
        hg = _dot(xb, wgb_ref[...])
        hu = _dot(xb, wub_ref[...])
        hid = (_silu(hg) * hu).astype(BF16)
        ys_ref[...] = _dot(hid, wdb_ref[...])

    @pl.when(jnp.logical_not(used))
    def _():
        ys_ref[...] = jnp.zeros_like(ys_ref)


def _experts(tile_expert, n_used, xs, w_gate, w_up, w_down):
    rows, d = xs.shape
    ntiles = rows // TME
    return pl.pallas_call(
        _expert_body,
        out_shape=jax.ShapeDtypeStruct((rows, d), F32),
        grid_spec=pltpu.PrefetchScalarGridSpec(
            num_scalar_prefetch=2,
            grid=(ntiles,),
            in_specs=[
                pl.BlockSpec((TME, d), lambda i, te, nu: (jnp.minimum(i, nu[0] - 1), 0)),
                pl.BlockSpec((1, d, FF), lambda i, te, nu: (te[i], 0, 0)),
                pl.BlockSpec((1, d, FF), lambda i, te, nu: (te[i], 0, 0)),
                pl.BlockSpec((1, FF, d), lambda i, te, nu: (te[i], 0, 0)),
            ],
            out_specs=pl.BlockSpec((TME, d), lambda i, te, nu: (i, 0)),
            scratch_shapes=[pltpu.VMEM((d, FF), BF16), pltpu.VMEM((d, FF), BF16), pltpu.VMEM((FF, d), BF16)],
        ),
        compiler_params=_cparams(("arbitrary",)),
        name="moe_experts",
    )(tile_expert, n_used, xs, w_gate, w_up, w_down)


def _gather_pairs(dest_ref, ys_ref, buf_ref, sem):
    def issue(g, carry):
        for u in range(UNROLL):
            t = g * UNROLL + u
            for sl in range(2):
                src = dest_ref[0, 0, sl * TM + t]
                pltpu.make_async_copy(ys_ref.at[pl.ds(src, 1)], buf_ref.at[sl, pl.ds(t, 1)], sem).start()
        return carry

    lax.fori_loop(0, TM // UNROLL, issue, 0)
    for sl in range(2):
        pltpu.make_async_copy(ys_ref.at[pl.ds(0, TM)], buf_ref.at[sl], sem).wait()


def _moe_combine(wc_ref, buf_ref):
    return wc_ref[:, 0:1] * buf_ref[0] + wc_ref[:, 1:2] * buf_ref[1]


def _routing_plan(meta_i, counts, tile_tokens):
    t = meta_i.shape[1]
    cnt_perm = counts[:, 0].astype(I32)
    e = jnp.arange(N_EXP)
    cnt = cnt_perm[(e % EXP_PER_GROUP) * N_GROUPS + e // EXP_PER_GROUP]
    tiles_e = (cnt + TME - 1) // TME
    cum_tiles = jnp.cumsum(tiles_e)
    offs = (cum_tiles - tiles_e) * TME
    sel = meta_i[0:2][..., None] == e
    dest = jnp.sum(jnp.where(sel, offs, 0), axis=-1) + meta_i[2:4]
    ntiles = (2 * t) // TME + N_EXP
    n_used = cum_tiles[-1]
    te = jnp.sum(cum_tiles[None, :] <= jnp.arange(ntiles, dtype=I32)[:, None], axis=-1).astype(I32)
    last_e = jnp.max(jnp.where(tiles_e > 0, e, 0)).astype(I32)
    te = jnp.minimum(te, last_e)

    def tiles(n):
        return dest.reshape(2, t // n, n).transpose(1, 0, 2).reshape(t // n, 1, 2 * n)

    return tiles(TD), tiles(tile_tokens), te, n_used.reshape(1).astype(I32), ntiles * TME


def _moe(h2, meta_i, counts, w_gate, w_up, w_down):
    dest_td, dest_tm, te, n_used, rows = _routing_plan(meta_i, counts, TM)
    xs = _dispatch(dest_td, h2, jnp.zeros((rows, h2.shape[1]), F32))
    ys = _experts(te, n_used, xs, w_gate, w_up, w_down)
    return ys, dest_tm


def _conf_in_body(x_ref, dest_ref, wc_ref, ys_ref, pv_ref, w1_ref, b1_ref, x_out_ref, u_ref, buf_ref, sem):
    _gather_pairs(dest_ref, ys_ref, buf_ref, sem)
    x2 = x_ref[0] + pv_ref[0, 0:1, :] * _moe_combine(wc_ref, buf_ref)
    x_out_ref[0] = x2
    h = (_rms(x2) * pv_ref[0, 1:2, :] + pv_ref[0, 2:3, :]).astype(BF16)
    u = _dot(h, w1_ref[...]) + b1_ref[0:1, :]
    u_ref[0] = (u[:, :D] * _sigmoid(u[:, D:])).astype(BF16)


def _conf_in(x1, dest_tm, wcol, ys, pvec, w1, b1):
    bsz, seq, d = x1.shape
    nt = seq // TM
    return pl.pallas_call(
        _conf_in_body,
        out_shape=(jax.ShapeDtypeStruct((bsz, seq, d), F32), jax.ShapeDtypeStruct((bsz, seq, d), BF16)),
        grid=(bsz, nt),
        in_specs=[
            pl.BlockSpec((1, TM, d), lambda b, i: (b, i, 0)),
            pl.BlockSpec((1, 1, 2 * TM), lambda b, i: (b * nt + i, 0, 0), memory_space=pltpu.SMEM),
            pl.BlockSpec((TM, 8), lambda b, i: (b * nt + i, 0)),
            pl.BlockSpec(memory_space=pl.ANY),
            pl.BlockSpec((1, 8, d), lambda b, i: (b, 0, 0)),
            pl.BlockSpec((d, 2 * d), lambda b, i: (0, 0)),
            pl.BlockSpec((8, 2 * d), lambda b, i: (0, 0)),
        ],
        out_specs=(pl.BlockSpec((1, TM, d), lambda b, i: (b, i, 0)),
                   pl.BlockSpec((1, TM, d), lambda b, i: (b, i, 0))),
        scratch_shapes=[pltpu.VMEM((2, TM, d), F32), pltpu.SemaphoreType.DMA],
        compiler_params=_cparams(("arbitrary", "arbitrary")),
        name="combine_conf_in",
    )(x1, dest_tm, wcol, ys, pvec, w1, b1)


def _conf_mix_body(nt, cur_ref, prev_ref, next_ref, dw_ref, cv_ref, x_ref, w2_ref, pv_ref, rwt_ref, rb_ref,
                   upper_ref, x_out_ref, h2_ref, mi_ref, mf_ref, cnt_ref, ext_ref, sh_ref, conv_ref,
                   carry_ref):
    i = pl.program_id(1)
    ext_ref[0:HALO, :] = jnp.where(i == 0, 0.0, prev_ref[0].astype(F32))
    ext_ref[HALO:HALO + TM, :] = cur_ref[0].astype(F32)
    ext_ref[HALO + TM:2 * HALO + TM, :] = jnp.where(i == nt - 1, 0.0, next_ref[0].astype(F32))
    pad = CONF_K // 2
    span = TM + CONF_SPAN
    for cb in range(D // LANES):
        cs = slice(cb * LANES, (cb + 1) * LANES)
        for r in range(1, 8):
            sh_ref[r - 1, :, cs] = ext_ref[r:r + span, cs]
        acc = None
        for j in range(CONF_K):
            lo = HALO - pad + j
            r, base = lo % 8, lo - lo % 8
            src = ext_ref[base:base + TM, cs] if r == 0 else sh_ref[r - 1, base:base + TM, cs]
            term = src * dw_ref[j:j + 1, cs]
            acc = term if acc is None else acc + term
        conv_ref[:, cs] = acc + cv_ref[0:1, cs]
    u = conv_ref[...]
    mu = jnp.mean(u, axis=-1, keepdims=True)
    uc = u - mu
    var = jnp.mean(uc * uc, axis=-1, keepdims=True)
    y = uc * lax.rsqrt(var + EPS) * cv_ref[1:2, :] + cv_ref[2:3, :]
    mix = _dot(_silu(y).astype(BF16), w2_ref[...]) + cv_ref[3:4, :]
    x_new = x_ref[0] + pv_ref[0, 0:1, :] * mix
    _route_tail(x_new, pv_ref, rwt_ref, rb_ref, upper_ref, x_out_ref, h2_ref, mi_ref, mf_ref, cnt_ref,
                carry_ref)


def _conf_mix(u, dw32, cvec, x2, w2, pvec, rwt, rb, upper):
    bsz, seq, d = x2.shape
    nt = seq // TM
    r = TM // HALO
    cst = lambda shape: pl.BlockSpec(shape, lambda b, i: (0,) * len(shape))
    return pl.pallas_call(
        functools.partial(_conf_mix_body, nt),
        out_shape=_route_out_shapes(bsz, seq),
        grid=(bsz, nt),
        in_specs=[
            pl.BlockSpec((1, TM, d), lambda b, i: (b, i, 0)),
            pl.BlockSpec((1, HALO, d), lambda b, i: (b, jnp.maximum(i * r - 1, 0), 0)),
            pl.BlockSpec((1, HALO, d), lambda b, i: (b, jnp.minimum((i + 1) * r, nt * r - 1), 0)),
            cst((32, d)), cst((8, d)),
            pl.BlockSpec((1, TM, d), lambda b, i: (b, i, 0)),
            cst((d, d)),
            pl.BlockSpec((1, 8, d), lambda b, i: (b, 0, 0)),
        ] + _route_in_specs(),
        out_specs=_route_out_specs(nt),
        scratch_shapes=[pltpu.VMEM((TM + 2 * HALO, d), F32), pltpu.VMEM((7, TM + CONF_SPAN, d), F32),
                        pltpu.VMEM((TM, d), F32),
                        pltpu.VMEM((N_EXP, LANES), F32)],
        compiler_params=_cparams(("arbitrary", "arbitrary")),
        name="conf_mix_router",
    )(u, u, u, dw32, cvec, x2, w2, pvec, rwt, rb, upper)


def _final_body(x_ref, dest_ref, wc_ref, ys_ref, pv_ref, o_ref, buf_ref, sem):
    _gather_pairs(dest_ref, ys_ref, buf_ref, sem)
    x4 = x_ref[0] + pv_ref[0, 0:1, :] * _moe_combine(wc_ref, buf_ref)
    o_ref[0] = _rms(x4) * pv_ref[0, 1:2, :]


def _final(x3, dest_tm, wcol, ys, pvec):
    bsz, seq, d = x3.shape
    nt = seq // TM
    return pl.pallas_call(
        _final_body,
        out_shape=jax.ShapeDtypeStruct((bsz, seq, d), F32),
        grid=(bsz, nt),
        in_specs=[
            pl.BlockSpec((1, TM, d), lambda b, i: (b, i, 0)),
            pl.BlockSpec((1, 1, 2 * TM), lambda b, i: (b * nt + i, 0, 0), memory_space=pltpu.SMEM),
            pl.BlockSpec((TM, 8), lambda b, i: (b * nt + i, 0)),
            pl.BlockSpec(memory_space=pl.ANY),
            pl.BlockSpec((1, 8, d), lambda b, i: (b, 0, 0)),
        ],
        out_specs=pl.BlockSpec((1, TM, d), lambda b, i: (b, i, 0)),
        scratch_shapes=[pltpu.VMEM((2, TM, d), F32), pltpu.SemaphoreType.DMA],
        compiler_params=_cparams(("arbitrary", "arbitrary")),
        name="combine_final",
    )(x3, dest_tm, wcol, ys, pvec)


def _grid_sincos(rows, cols, dim):
    quarter = dim // 4
    omega = 1.0 / jnp.power(POS_BASE, jnp.arange(quarter, dtype=F32) / quarter)

    def axis_emb(n):
        ang = jnp.arange(n, dtype=F32)[:, None] * omega[None, :]
        return jnp.concatenate([jnp.sin(ang), jnp.cos(ang)], axis=-1)

    er = jnp.broadcast_to(axis_emb(rows)[:, None, :], (rows, cols, dim // 2))
    ec = jnp.broadcast_to(axis_emb(cols)[None, :, :], (rows, cols, dim // 2))
    return jnp.concatenate([er, ec], axis=-1).reshape(rows * cols, dim)


def _pad_rows(a, rows):
    return jnp.pad(a, ((0, rows - a.shape[0]),) + ((0, 0),) * (a.ndim - 1))


def _per_batch(*vecs):
    st = jnp.stack(vecs, axis=1)
    return jnp.pad(st, ((0, 0), (0, 8 - st.shape[1]), (0, 0)))


def _width_dft():
    k = np.arange(LANES)
    ang = 2 * np.pi * np.outer(k, k) / LANES
    groups = FN_W // LANES
    c = np.kron(np.eye(groups), np.cos(ang)) / math.sqrt(LANES)
    s = np.kron(np.eye(groups), np.sin(ang)) / math.sqrt(LANES)
    return jnp.asarray(np.concatenate([c, s], axis=1), dtype=BF16)


def kernel(x, c, ctx, c_ctx, ada_w, ada_b, norm1_g, norm2_g, hyb_w_in, dn_conv_w, dn_a_log, dn_dt_bias,
           dn_onorm_g, hyb_w_out, conf_w1, conf_b1, conf_dw_w, conf_dw_b, conf_ln_g, conf_ln_b, conf_w2,
           conf_b2, router_w, router_bias, moe_w_gate, moe_w_up, moe_w_down, final_g):
    bsz, seq, d = x.shape
    ctx_len = ctx.shape[1]
    assert d == D and seq % TM == 0 and ctx_len % TM == 0 and (bsz * seq) % TD == 0
    nct = ctx_len // TM
    t = bsz * seq

    pos = _grid_sincos(seq // GRID_W, GRID_W, d)

    cpad = _pad_rows(jnp.concatenate([c, c_ctx[None, :]], axis=0), 16)
    mods = _adaln(cpad, ada_w, ada_b)
    m = lambda l, k: mods[l, :, k * d:(k + 1) * d]
    a1 = norm1_g[0][None, :] * (1.0 + m(0, 1))
    mod_a = jnp.stack([jnp.broadcast_to(a1[bsz], (bsz, d)), a1[:bsz]], axis=1)
    mod_b = jnp.stack([jnp.broadcast_to(m(0, 0)[bsz], (bsz, d)), m(0, 0)[:bsz]], axis=1)

    w_in = hyb_w_in[0]
    z_lo = QKV_W + AB_W
    f_lo = z_lo + Z_W
    wqkv = w_in[:, :QKV_W].astype(BF16)
    wab = jnp.pad(w_in[:, QKV_W:z_lo], ((0, 0), (0, LANES - AB_W))).astype(BF16)
    wz = w_in[:, z_lo:f_lo].astype(BF16)
    wf = w_in[:, f_lo:].astype(BF16)
    gpar = jnp.zeros((8, LANES), F32)
    gpar = gpar.at[0, :2 * HEADS].set(-jnp.exp(dn_a_log[0].reshape(-1)))
    gpar = gpar.at[1, :2 * HEADS].set(dn_dt_bias[0].reshape(-1))
    qkv_pre, z, fc, fs, gates = _inproj(ctx, x, pos, mod_a, mod_b, wqkv, wz, wf, wab, _width_dft(), gpar)

    qkv = _dnconv(qkv_pre, _pad_rows(dn_conv_w[0], 8), nct)
    lc = qkv.shape[1]
    nchunks = lc // CHUNK
    grow = gates[:, :, :AB_W].reshape(bsz, nchunks, CHUNK, 4, HEADS)
    grow = grow.transpose(0, 1, 3, 4, 2).reshape(bsz, nchunks, 4, HB)
    grow = jnp.pad(grow, ((0, 0), (0, 0), (0, 4), (0, 0)))
    o_f, o_b = _deltanet(qkv, grow, seq, ctx_len // CHUNK)

    fn = _seq_dft(fc, fs)

    perm = (np.arange(N_EXP) % N_GROUPS) * EXP_PER_GROUP + np.arange(N_EXP) // N_GROUPS
    rwt = router_w.T[perm].astype(BF16)
    rb = jnp.broadcast_to(router_bias[perm][:, None], (N_EXP, LANES)).astype(F32)
    upper = jnp.asarray(np.triu(np.ones((TM, TM), np.float32), 1), dtype=BF16)
    w_out = hyb_w_out[0].astype(BF16)
    onorm8 = _pad_rows(jnp.tile(dn_onorm_g[0], HEADS)[None, :], 8)
    pv0 = _per_batch(m(0, 2)[:bsz], norm2_g[0][None, :] * (1.0 + m(0, 4)[:bsz]), m(0, 3)[:bsz])
    x1, h2, meta_i, meta_f, counts = _mix0(o_f, o_b, z, fn, x, pos, w_out[:V_W], w_out[V_W:], onorm8, pv0,
                                           rwt, rb, upper)
    ys, dest_tm = _moe(h2, meta_i, counts, moe_w_gate[0], moe_w_up[0], moe_w_down[0])
    wcol = meta_f.T

    pv1 = _per_batch(m(0, 5)[:bsz], norm1_g[1][None, :] * (1.0 + m(1, 1)[:bsz]), m(1, 0)[:bsz])
    x2, u = _conf_in(x1, dest_tm, wcol, ys, pv1, conf_w1[0].astype(BF16), _pad_rows(conf_b1[0][None, :], 8))
    cvec = _pad_rows(jnp.stack([conf_dw_b[0], conf_ln_g[0], conf_ln_b[0], conf_b2[0]]), 8)
    pv2 = _per_batch(m(1, 2)[:bsz], norm2_g[1][None, :] * (1.0 + m(1, 4)[:bsz]), m(1, 3)[:bsz])
    x3, h2, meta_i, meta_f, counts = _conf_mix(u, _pad_rows(conf_dw_w[0], 32), cvec, x2,
                                               conf_w2[0].astype(BF16), pv2, rwt, rb, upper)
    ys, dest_tm = _moe(h2, meta_i, counts, moe_w_gate[1], moe_w_up[1], moe_w_down[1])
    pv3 = _per_batch(m(1, 5)[:bsz], jnp.broadcast_to(final_g[None, :], (bsz, d)))
    return _final(x3, dest_tm, meta_f.T, ys, pv3)
```

```python
import functools
import math

import numpy as np
import jax
import jax.numpy as jnp
from jax import lax
from jax.experimental import pallas as pl
from jax.experimental.pallas import tpu as pltpu

F32 = jnp.float32
BF16 = jnp.bfloat16
I32 = jnp.int32

D = 1024
GRID_W = 64
HEADS = 4
DK = 128
DV = 128
CHUNK = 64
QK_W = HEADS * DK
V_W = HEADS * DV
QKV_W = 2 * QK_W + V_W
AB_W = 16
Z_W = V_W
FN_W = 512
DN_CONV_K = 5
CONF_K = 31
N_EXP = 32
N_GROUPS = 8
EXP_PER_GROUP = 4
FF = 512
EPS = 1e-6
POS_BASE = 10000.0

LANES = 128
HALO = 16
TM = 256
TME = 512
HB = HEADS * CHUNK
DN_GROUP = 4
CONF_SPAN = 24
VMEM_LIMIT = 56 * 1024 * 1024


def _cparams(sem):
    return pltpu.CompilerParams(dimension_semantics=sem, vmem_limit_bytes=VMEM_LIMIT)


def _sigmoid(v):
    return jax.nn.sigmoid(v)


def _silu(v):
    return v * jax.nn.sigmoid(v)


def _dot(a, b):
    return jnp.dot(a, b, preferred_element_type=F32)


def _dot_nt(a, b):
    return lax.dot_general(a, b, (((1,), (1,)), ((), ())), preferred_element_type=F32)


def _split3(a):
    a1 = a.astype(BF16)
    r1 = a - a1.astype(F32)
    a2 = r1.astype(BF16)
    a3 = (r1 - a2.astype(F32)).astype(BF16)
    return a1, a2, a3


def _rms(v, eps=EPS):
    return v * lax.rsqrt(jnp.mean(v * v, axis=-1, keepdims=True) + eps)


def _adaln_body(c_ref, w_ref, b_ref, o_ref):
    s = _silu(c_ref[...])
    o_ref[0] = jnp.dot(s, w_ref[0], preferred_element_type=F32,
                       precision=lax.Precision.HIGHEST) + b_ref[0]


def _adaln(cpad, ada_w, ada_b):
    depth, d, n = ada_w.shape
    tn = 1536
    rows = cpad.shape[0]
    return pl.pallas_call(
        _adaln_body,
        out_shape=jax.ShapeDtypeStruct((depth, rows, n), F32),
        grid=(depth, n // tn),
        in_specs=[
            pl.BlockSpec((rows, d), lambda l, j: (0, 0)),
            pl.BlockSpec((1, d, tn), lambda l, j: (l, 0, j)),
            pl.BlockSpec((1, 1, tn), lambda l, j: (l, 0, j)),
        ],
        out_specs=pl.BlockSpec((1, rows, tn), lambda l, j: (l, 0, j)),
        compiler_params=_cparams(("arbitrary", "arbitrary")),
        name="adaln",
    )(cpad, ada_w, ada_b.reshape(depth, 1, n))


def _inproj_body(nct, ctx_ref, x_ref, pos_ref, ma_ref, mb_ref, wqkv_ref, wz_ref, wf_ref, wab_ref,
                 dftw_ref, gpar_ref, qkv_ref, z_ref, fc_ref, fs_ref, gate_ref):
    i = pl.program_id(1)
    is_ctx = i < nct
    xin = jnp.where(is_ctx, ctx_ref[0], x_ref[0] + pos_ref[...])
    a = jnp.where(is_ctx, ma_ref[0, 0:1, :], ma_ref[0, 1:2, :])
    b = jnp.where(is_ctx, mb_ref[0, 0:1, :], mb_ref[0, 1:2, :])
    h = (_rms(xin) * a + b).astype(BF16)
    qkv_ref[0] = _dot(h, wqkv_ref[...]).astype(BF16)
    z_ref[0] = _dot(h, wz_ref[...]).astype(BF16)
    f = _dot(h, wf_ref[...]).astype(BF16)
    fcs = _dot(f, dftw_ref[...])
    fc_ref[0] = fcs[:, :FN_W].astype(BF16)
    fs_ref[0] = fcs[:, FN_W:].astype(BF16)
    ab = _dot(h, wab_ref[...])
    lane = lax.broadcasted_iota(I32, ab.shape, 1)
    pre = ab + gpar_ref[1:2, :]
    softplus = jnp.maximum(pre, 0.0) + jnp.log1p(jnp.exp(-jnp.abs(pre)))
    gate_ref[0] = jnp.where(lane < 2 * HEADS, gpar_ref[0:1, :] * softplus, _sigmoid(ab))


def _inproj(ctx, x, pos, mod_a, mod_b, wqkv, wz, wf, wab, dftw, gpar):
    bsz, seq, d = x.shape
    nct = ctx.shape[1] // TM
    nt = nct + seq // TM
    lc = nt * TM
    xi = lambda b, i: (b, jnp.maximum(i - nct, 0), 0)
    full = lambda shape: pl.BlockSpec(shape, lambda b, i: (0,) * len(shape))
    return pl.pallas_call(
        functools.partial(_inproj_body, nct),
        out_shape=(
            jax.ShapeDtypeStruct((bsz, lc, QKV_W), BF16),
            jax.ShapeDtypeStruct((bsz, seq, Z_W), BF16),
            jax.ShapeDtypeStruct((bsz, seq, FN_W), BF16),
            jax.ShapeDtypeStruct((bsz, seq, FN_W), BF16),
            jax.ShapeDtypeStruct((bsz, lc, LANES), F32),
        ),
        grid=(bsz, nt),
        in_specs=[
            pl.BlockSpec((1, TM, d), lambda b, i: (b, jnp.minimum(i, nct - 1), 0)),
            pl.BlockSpec((1, TM, d), xi),
            pl.BlockSpec((TM, d), lambda b, i: (jnp.maximum(i - nct, 0), 0)),
            pl.BlockSpec((1, 2, d), lambda b, i: (b, 0, 0)),
            pl.BlockSpec((1, 2, d), lambda b, i: (b, 0, 0)),
            full((d, QKV_W)), full((d, Z_W)), full((d, FN_W)), full((d, LANES)),
            full((FN_W, 2 * FN_W)), full((8, LANES)),
        ],
        out_specs=(
            pl.BlockSpec((1, TM, QKV_W), lambda b, i: (b, i, 0)),
            pl.BlockSpec((1, TM, Z_W), xi),
            pl.BlockSpec((1, TM, FN_W), xi),
            pl.BlockSpec((1, TM, FN_W), xi),
            pl.BlockSpec((1, TM, LANES), lambda b, i: (b, i, 0)),
        ),
        compiler_params=_cparams(("arbitrary", "arbitrary")),
        name="inproj",
    )(ctx, x, pos, mod_a, mod_b, wqkv, wz, wf, wab, dftw, gpar)


def _dnconv_body(nct, nt, cur_ref, prev_ref, next_ref, w_ref, o_ref, ext_ref):
    i = pl.program_id(1)
    first = jnp.logical_or(i == 0, i == nct)
    last = jnp.logical_or(i == nct - 1, i == nt - 1)
    ext_ref[0:HALO, :] = jnp.where(first, 0.0, prev_ref[0].astype(F32))
    ext_ref[HALO:HALO + TM, :] = cur_ref[0].astype(F32)
    ext_ref[HALO + TM:2 * HALO + TM, :] = jnp.where(last, 0.0, next_ref[0].astype(F32))
    pad = DN_CONV_K // 2
    for cb in range(QKV_W // LANES):
        cs = slice(cb * LANES, (cb + 1) * LANES)
        acc = None
        for j in range(DN_CONV_K):
            lo = HALO - pad + j
            term = ext_ref[lo:lo + TM, cs] * w_ref[j:j + 1, cs]
            acc = term if acc is None else acc + term
        y = _silu(acc)
        if cb < 2 * HEADS:
            y = y * lax.rsqrt(jnp.sum(y * y, axis=-1, keepdims=True) + EPS)
        o_ref[0, :, cs] = y.astype(BF16)


def _dnconv(qkv_pre, conv_w8, nct):
    bsz, lc, w = qkv_pre.shape
    nt = lc // TM
    r = TM // HALO
    return pl.pallas_call(
        functools.partial(_dnconv_body, nct, nt),
        out_shape=jax.ShapeDtypeStruct((bsz, lc, w), BF16),
        grid=(bsz, nt),
        in_specs=[
            pl.BlockSpec((1, TM, w), lambda b, i: (b, i, 0)),
            pl.BlockSpec((1, HALO, w), lambda b, i: (b, jnp.maximum(i * r - 1, 0), 0)),
            pl.BlockSpec((1, HALO, w), lambda b, i: (b, jnp.minimum((i + 1) * r, nt * r - 1), 0)),
            pl.BlockSpec((8, w), lambda b, i: (0, 0)),
        ],
        out_specs=pl.BlockSpec((1, TM, w), lambda b, i: (b, i, 0)),
        scratch_shapes=[pltpu.VMEM((TM + 2 * HALO, w), F32)],
        compiler_params=_cparams(("arbitrary", "arbitrary")),
        name="dnconv",
    )(qkv_pre, qkv_pre, qkv_pre, conv_w8)


def _dn_setup(d, qkv, grow, mask_incl, mask_strict, trit, bdones, eye, lvl0):
    neg_inf = float("-inf")
    stack = lambda off: jnp.concatenate(
        [qkv[:, off + h * DK: off + (h + 1) * DK] for h in range(HEADS)], axis=0)
    q_s = stack(0)
    k_s = stack(QK_W)
    v_s = stack(2 * QK_W)
    g3 = _split3(grow)
    cg_row = sum(_dot(t, trit) for t in g3)
    tot_row = sum(_dot(t, bdones) for t in g3)
    rows = jnp.concatenate([cg_row, tot_row, grow, jnp.zeros((LANES - 24, HB), F32)], axis=0)
    cols = rows.T
    cgc = cols[:, d:d + 1]
    totc = cols[:, 8 + d:9 + d]
    betac = cols[:, 18 + d:19 + d]
    cgr = cg_row[d:d + 1, :]
    dmat = jnp.exp(jnp.where(mask_incl > 0.5, cgc - cgr, neg_inf))
    kk = _dot_nt(k_s, k_s)
    qk = _dot_nt(q_s, k_s)
    a = jnp.where(mask_strict > 0.5, kk * dmat, 0.0) * betac
    ecg = jnp.exp(cgc)
    k32 = k_s.astype(F32)
    scale = DK ** -0.5
    return dict(
        d=d, a=a, x=eye.astype(F32) - a * lvl0, totc=totc,
        rhs=jnp.concatenate([(betac * v_s.astype(F32)).astype(BF16),
                             ((betac * ecg) * k32).astype(BF16)], axis=1),
        ktail=(k32 * jnp.exp(totc - cgc)).astype(BF16),
        qdec=(q_s.astype(F32) * (ecg * scale)).astype(BF16),
        qkd=(qk * dmat * scale).astype(BF16))


def _dn_body(qf_ref, qb_ref, gf_ref, gb_ref, mli_ref, mls_ref, mui_ref, mus_ref, bdl_ref, bdu_ref,
             bdo_ref, eye_ref, lvl_ref, of_ref, ob_ref, s_ref):
    @pl.when(pl.program_id(1) == 0)
    def _():
        s_ref[...] = jnp.zeros_like(s_ref)

    bdl = bdl_ref[...]
    bdu = bdu_ref[...]
    bdo = bdo_ref[...]
    eye = eye_ref[...]
    chains = []
    for g in range(qf_ref.shape[0]):
        c = _dn_setup(0, qf_ref[g], gf_ref[g, 0], mli_ref[...], mls_ref[...], bdu, bdo, eye, lvl_ref[0])
        c.update(g=g, out=of_ref)
        chains.append(c)
        c = _dn_setup(1, qb_ref[g], gb_ref[g, 0], mui_ref[...], mus_ref[...], bdl, bdo, eye, lvl_ref[0])
        c.update(g=g, out=ob_ref)
        chains.append(c)
    for lvl in range(1, int(math.log2(CHUNK))):
        for c in chains:
            c["xb"] = c["x"].astype(BF16)
            c["y"] = _dot((c["a"] * lvl_ref[lvl]).astype(BF16), c["xb"]).astype(BF16)
        for c in chains:
            c["x"] = c["x"] - _dot(c["xb"], c["y"])
    for c in chains:
        uw = _dot(c["x"].astype(BF16), c["rhs"])
        c["u"] = uw[:, :DV]
        c["w"] = uw[:, DV:].astype(BF16)
    for c in chains:
        c["vn"] = []
        c["oi"] = []
    for h in range(HEADS):
        rs = slice(h * CHUNK, (h + 1) * CHUNK)
        for c in chains:
            s = s_ref[c["g"], c["d"], h]
            sb = s.astype(BF16)
            c["s"] = s
            c["vn"].append(c["u"][rs] - _dot(c["w"][rs], sb))
            c["oi"].append(_dot(c["qdec"][rs], sb))
        for c in chains:
            cd = jnp.exp(c["totc"][h * CHUNK:h * CHUNK + 1, :])
            upd = lax.dot_general(c["ktail"][rs], c["vn"][h].astype(BF16), (((0,), (0,)), ((), ())),
                                  preferred_element_type=F32)
            s_ref[c["g"], c["d"], h] = c["s"] * cd + upd
    for c in chains:
        vn_s = jnp.concatenate(c["vn"], axis=0).astype(BF16)
        o_s = jnp.concatenate(c["oi"], axis=0) + _dot(c["qkd"], vn_s)
        c["out"][c["g"]] = jnp.concatenate(
            [o_s[h * CHUNK:(h + 1) * CHUNK] for h in range(HEADS)], axis=1).astype(BF16)


def _dn_consts():
    r = np.arange(HB)
    same = (r[:, None] // CHUNK) == (r[None, :] // CHUNK)
    li = same & (r[:, None] >= r[None, :])
    ls = same & (r[:, None] > r[None, :])
    ui = same & (r[:, None] <= r[None, :])
    us = same & (r[:, None] < r[None, :])
    f = lambda m: jnp.asarray(m.astype(np.float32))
    h = lambda m: jnp.asarray(m.astype(np.float32), dtype=BF16)
    nlev = int(math.log2(CHUNK))
    lv = [((r[:, None] >> (l + 1)) == (r[None, :] >> (l + 1))) & ((r[:, None] >> l) != (r[None, :] >> l))
          for l in range(nlev)]
    return (f(li), f(ls), f(ui), f(us), h(li), h(ui), h(same), h(np.eye(HB, dtype=bool)),
            jnp.asarray(np.stack(lv).astype(np.float32)))


def _deltanet(qkv, grow, seq, ncc):
    bsz, lc, w = qkv.shape
    nchunks = lc // CHUNK
    nx = seq // CHUNK
    fwd = lambda s: s
    bwd = lambda s: jnp.where(s < ncc, ncc - 1 - s, ncc + nchunks - 1 - s)
    ofi = lambda b, s: (b, jnp.maximum(s - ncc, 0), 0)
    obi = lambda b, s: (b, jnp.minimum(nchunks - 1 - s, nx - 1), 0)
    cst = lambda shape: pl.BlockSpec(shape, lambda b, s: (0, 0))
    consts = _dn_consts()
    grp = math.gcd(bsz, DN_GROUP)
    return pl.pallas_call(
        _dn_body,
        out_shape=(jax.ShapeDtypeStruct((bsz, seq, V_W), BF16),
                   jax.ShapeDtypeStruct((bsz, seq, V_W), BF16)),
        grid=(bsz // grp, nchunks),
        in_specs=[
            pl.BlockSpec((grp, CHUNK, w), lambda b, s: (b, fwd(s), 0)),
            pl.BlockSpec((grp, CHUNK, w), lambda b, s: (b, bwd(s), 0)),
            pl.BlockSpec((grp, 1, 8, HB), lambda b, s: (b, fwd(s), 0, 0)),
            pl.BlockSpec((grp, 1, 8, HB), lambda b, s: (b, bwd(s), 0, 0)),
        ] + [cst((HB, HB))] * 8 + [pl.BlockSpec((int(math.log2(CHUNK)), HB, HB), lambda b, s: (0, 0, 0))],
        out_specs=(pl.BlockSpec((grp, CHUNK, V_W), ofi), pl.BlockSpec((grp, CHUNK, V_W), obi)),
        scratch_shapes=[pltpu.VMEM((grp, 2, HEADS, DK, DV), F32)],
        compiler_params=_cparams(("arbitrary", "arbitrary")),
        name="deltanet",
    )(qkv, qkv, grow, grow, *consts)


def _fft1_body(ns2, fc_ref, fs_ref, m_ref, cos_ref, sin_ref, y_ref):
    n1 = fc_ref.shape[1]
    xin = jnp.concatenate([fc_ref[0], fs_ref[0]], axis=0)
    y = _dot(m_ref[...], xin)
    yr, yi = y[:n1], y[n1:]
    for q in range(ns2):
        cs = slice(q * FN_W, (q + 1) * FN_W)
        c = jnp.concatenate([cos_ref[q]] * (FN_W // LANES), axis=1)
        s = jnp.concatenate([sin_ref[q]] * (FN_W // LANES), axis=1)
        y_ref[0, 0, :, cs] = (yr[:, cs] * c + yi[:, cs] * s).astype(BF16)
        y_ref[0, 1, :, cs] = (yi[:, cs] * c - yr[:, cs] * s).astype(BF16)


def _fft2_body(ns1, y_ref, cs_ref, o_ref):
    for q in range(ns1):
        rhs = jnp.concatenate([y_ref[0, 0, q], y_ref[0, 1, q]], axis=0)
        o_ref[0, :, q * FN_W:(q + 1) * FN_W] = _dot(cs_ref[...], rhs).astype(BF16)


def _seq_dft(fc, fs):
    bsz, seq, w = fc.shape
    n2 = GRID_W
    n1 = seq // n2
    k1 = np.arange(n1)
    ang1 = 2 * np.pi * np.outer(k1, k1) / n1
    c1, s1 = np.cos(ang1), np.sin(ang1)
    m1 = jnp.asarray(np.block([[c1, -s1], [-s1, -c1]]), dtype=BF16)
    tw = 2 * np.pi * np.outer(np.arange(n2), k1) / seq
    cos_t = jnp.asarray(np.repeat(np.cos(tw)[:, :, None], LANES, axis=2), dtype=F32)
    sin_t = jnp.asarray(np.repeat(np.sin(tw)[:, :, None], LANES, axis=2), dtype=F32)
    k2 = np.arange(n2)
    ang2 = 2 * np.pi * np.outer(k2, k2) / n2
    cs2 = jnp.asarray(np.concatenate([np.cos(ang2), np.sin(ang2)], axis=1) / math.sqrt(seq), dtype=BF16)

    ns2 = 4
    y = pl.pallas_call(
        functools.partial(_fft1_body, ns2),
        out_shape=jax.ShapeDtypeStruct((bsz, 2, n1, n2 * w), BF16),
        grid=(bsz, n2 // ns2),
        in_specs=[
            pl.BlockSpec((1, n1, ns2 * w), lambda b, j: (b, 0, j)),
            pl.BlockSpec((1, n1, ns2 * w), lambda b, j: (b, 0, j)),
            pl.BlockSpec((2 * n1, 2 * n1), lambda b, j: (0, 0)),
            pl.BlockSpec((ns2, n1, LANES), lambda b, j: (j, 0, 0)),
            pl.BlockSpec((ns2, n1, LANES), lambda b, j: (j, 0, 0)),
        ],
        out_specs=pl.BlockSpec((1, 2, n1, ns2 * w), lambda b, j: (b, 0, 0, j)),
        compiler_params=_cparams(("arbitrary", "arbitrary")),
        name="seqdft_stage1",
    )(fc.reshape(bsz, n1, n2 * w), fs.reshape(bsz, n1, n2 * w), m1, cos_t, sin_t)

    ns1 = min(8, n1)
    out = pl.pallas_call(
        functools.partial(_fft2_body, ns1),
        out_shape=jax.ShapeDtypeStruct((bsz, n2, n1 * w), BF16),
        grid=(bsz, n1 // ns1),
        in_specs=[
            pl.BlockSpec((1, 2, ns1, n2, w), lambda b, j: (b, 0, j, 0, 0)),
            pl.BlockSpec((n2, 2 * n2), lambda b, j: (0, 0)),
        ],
        out_specs=pl.BlockSpec((1, n2, ns1 * w), lambda b, j: (b, 0, j)),
        compiler_params=_cparams(("arbitrary", "arbitrary")),
        name="seqdft_stage2",
    )(y.reshape(bsz, 2, n1, n2, w), cs2)
    return out.reshape(bsz, seq, w)


def _route_tail(x_new, pv_ref, rwt_ref, rb_ref, upper_ref, x_out_ref, h2_ref, mi_ref, mf_ref, cnt_ref,
                carry_ref):
    first = jnp.logical_and(pl.program_id(0) == 0, pl.program_id(1) == 0)

    @pl.when(first)
    def _():
        carry_ref[...] = jnp.zeros_like(carry_ref)

    x_out_ref[0] = x_new
    h2 = _rms(x_new) * pv_ref[0, 1:2, :] + pv_ref[0, 2:3, :]
    h2_ref[...] = h2
    logits = _dot_nt(rwt_ref[...], h2.astype(BF16))
    scores = _sigmoid(logits)
    sel = scores + rb_ref[:, 0:1]
    a = [sel[N_GROUPS * j:N_GROUPS * (j + 1)] for j in range(EXP_PER_GROUP)]
    sc = [scores[N_GROUPS * j:N_GROUPS * (j + 1)] for j in range(EXP_PER_GROUP)]
    gs = None
    for j0 in range(EXP_PER_GROUP):
        for j1 in range(j0 + 1, EXP_PER_GROUP):
            pair = a[j0] + a[j1]
            gs = pair if gs is None else jnp.maximum(gs, pair)
    gidx = lax.broadcasted_iota(I32, gs.shape, 0).astype(F32)
    gmax = jnp.max(gs, axis=0, keepdims=True)
    best = jnp.min(jnp.where(gs == gmax, gidx, float(N_GROUPS)), axis=0, keepdims=True)
    inb = gidx == best
    v = [jnp.sum(jnp.where(inb, a[j], 0.0), axis=0, keepdims=True) for j in range(EXP_PER_GROUP)]
    s = [jnp.sum(jnp.where(inb, sc[j], 0.0), axis=0, keepdims=True) for j in range(EXP_PER_GROUP)]
    v1, i1, s1 = v[0], jnp.zeros_like(v[0]), s[0]
    for j in range(1, EXP_PER_GROUP):
        better = v[j] > v1
        v1 = jnp.where(better, v[j], v1)
        i1 = jnp.where(better, float(j), i1)
        s1 = jnp.where(better, s[j], s1)
    v2 = jnp.full_like(v1, float("-inf"))
    i2 = jnp.zeros_like(v1)
    s2 = jnp.zeros_like(v1)
    for j in range(EXP_PER_GROUP):
        better = jnp.logical_and(i1 != float(j), v[j] > v2)
        v2 = jnp.where(better, v[j], v2)
        i2 = jnp.where(better, float(j), i2)
        s2 = jnp.where(better, s[j], s2)
    denom = s1 + s2
    rid = lax.broadcasted_iota(I32, logits.shape, 0).astype(F32)
    oh1 = rid == (i1 * N_GROUPS + best)
    oh2 = rid == (i2 * N_GROUPS + best)
    oh = jnp.where(jnp.logical_or(oh1, oh2), 1.0, 0.0)
    prefix = _dot(oh.astype(BF16), upper_ref[...])
    base = carry_ref[:, 0:1] + prefix
    rank1 = jnp.sum(jnp.where(oh1, base, 0.0), axis=0, keepdims=True)
    rank2 = jnp.sum(jnp.where(oh2, base, 0.0), axis=0, keepdims=True)
    carry_new = carry_ref[...] + jnp.sum(oh, axis=1, keepdims=True)
    carry_ref[...] = carry_new
    cnt_ref[...] = carry_new
    mi_ref[...] = jnp.zeros_like(mi_ref)
    mi_ref[0:1, :] = (best * EXP_PER_GROUP + i1).astype(I32)
    mi_ref[1:2, :] = (best * EXP_PER_GROUP + i2).astype(I32)
    mi_ref[2:3, :] = rank1.astype(I32)
    mi_ref[3:4, :] = rank2.astype(I32)
    mf_ref[...] = jnp.zeros_like(mf_ref)
    mf_ref[0:1, :] = s1 / denom
    mf_ref[1:2, :] = s2 / denom


def _route_out_shapes(bsz, seq):
    t = bsz * seq
    return (
        jax.ShapeDtypeStruct((bsz, seq, D), F32),
        jax.ShapeDtypeStruct((t, D), F32),
        jax.ShapeDtypeStruct((8, t), I32),
        jax.ShapeDtypeStruct((8, t), F32),
        jax.ShapeDtypeStruct((N_EXP, LANES), F32),
    )


def _route_out_specs(nt):
    flat = lambda b, i: (b * nt + i, 0)
    return (
        pl.BlockSpec((1, TM, D), lambda b, i: (b, i, 0)),
        pl.BlockSpec((TM, D), flat),
        pl.BlockSpec((8, TM), lambda b, i: (0, b * nt + i)),
        pl.BlockSpec((8, TM), lambda b, i: (0, b * nt + i)),
        pl.BlockSpec((N_EXP, LANES), lambda b, i: (0, 0)),
    )


def _route_in_specs():
    return [
        pl.BlockSpec((N_EXP, D), lambda b, i: (0, 0)),
        pl.BlockSpec((N_EXP, LANES), lambda b, i: (0, 0)),
        pl.BlockSpec((TM, TM), lambda b, i: (0, 0)),
    ]


def _mix0_body(of_ref, ob_ref, z_ref, fn_ref, x_ref, pos_ref, wdn_ref, wfn_ref, on_ref, pv_ref,
               rwt_ref, rb_ref, upper_ref, x_out_ref, h2_ref, mi_ref, mf_ref, cnt_ref, carry_ref):
    o = of_ref[0].astype(F32) + ob_ref[0].astype(F32)
    z = z_ref[0].astype(F32)
    parts = []
    for h in range(HEADS):
        cs = slice(h * DV, (h + 1) * DV)
        parts.append(_rms(o[:, cs]) * on_ref[0:1, cs] * _silu(z[:, cs]))
    dn = jnp.concatenate(parts, axis=1).astype(BF16)
    mix = _dot(dn, wdn_ref[...]) + _dot(fn_ref[0], wfn_ref[...])
    x_new = x_ref[0] + pos_ref[...] + pv_ref[0, 0:1, :] * mix
    _route_tail(x_new, pv_ref, rwt_ref, rb_ref, upper_ref, x_out_ref, h2_ref, mi_ref, mf_ref, cnt_ref,
                carry_ref)


def _mix0(o_f, o_b, z, fn, x, pos, wdn, wfn, onorm8, pvec, rwt, rb, upper):
    bsz, seq, d = x.shape
    nt = seq // TM
    tok = lambda w: pl.BlockSpec((1, TM, w), lambda b, i: (b, i, 0))
    cst = lambda shape: pl.BlockSpec(shape, lambda b, i: (0,) * len(shape))
    return pl.pallas_call(
        _mix0_body,
        out_shape=_route_out_shapes(bsz, seq),
        grid=(bsz, nt),
        in_specs=[tok(V_W), tok(V_W), tok(Z_W), tok(FN_W), tok(d),
                  pl.BlockSpec((TM, d), lambda b, i: (i, 0)),
                  cst((V_W, d)), cst((FN_W, d)), cst((8, V_W)),
                  pl.BlockSpec((1, 8, d), lambda b, i: (b, 0, 0))] + _route_in_specs(),
        out_specs=_route_out_specs(nt),
        scratch_shapes=[pltpu.VMEM((N_EXP, LANES), F32)],
        compiler_params=_cparams(("arbitrary", "arbitrary")),
        name="mix0_router",
    )(o_f, o_b, z, fn, x, pos, wdn, wfn, onorm8, pvec, rwt, rb, upper)


TD = 512
UNROLL = 8


def _dispatch_body(pad_ref, dest_ref, h2_ref, xs_ref, zero_ref, sem):
    ntiles = xs_ref.shape[0] // TME

    @pl.when(pl.program_id(0) == 0)
    def _():
        zero_ref[...] = jnp.zeros_like(zero_ref)

        def zero_tile(k):
            return pltpu.make_async_copy(zero_ref, xs_ref.at[pl.ds(pl.multiple_of(k * TME, TME), TME)], sem)

        def fill_tail(k, carry):
            zero_tile(k).start()
            return carry

        def drain_tail(k, carry):
            zero_tile(k).wait()
            return carry

        for e in range(N_EXP):
            @pl.when(pad_ref[e] >= 0)
            def _():
                zero_tile(pad_ref[e]).start()
        lax.fori_loop(pad_ref[N_EXP], ntiles, fill_tail, 0)
        for e in range(N_EXP):
            @pl.when(pad_ref[e] >= 0)
            def _():
                zero_tile(pad_ref[e]).wait()
        lax.fori_loop(pad_ref[N_EXP], ntiles, drain_tail, 0)

    def issue(g, carry):
        for u in range(UNROLL):
            t = g * UNROLL + u
            for sl in range(2):
                dst = dest_ref[0, 0, sl * TD + t]
                pltpu.make_async_copy(h2_ref.at[pl.ds(t, 1)], xs_ref.at[pl.ds(dst, 1)], sem).start()
        return carry

    lax.fori_loop(0, TD // UNROLL, issue, 0)
    for sl in range(2):
        pltpu.make_async_copy(h2_ref, xs_ref.at[pl.ds(0, TD)], sem).wait()


def _dispatch(pad_start, dest_tiles, h2, rows):
    t, d = h2.shape
    return pl.pallas_call(
        _dispatch_body,
        out_shape=jax.ShapeDtypeStruct((rows, d), F32),
        grid_spec=pltpu.PrefetchScalarGridSpec(
            num_scalar_prefetch=1,
            grid=(t // TD,),
            in_specs=[
                pl.BlockSpec((1, 1, 2 * TD), lambda i, pad: (i, 0, 0), memory_space=pltpu.SMEM),
                pl.BlockSpec((TD, d), lambda i, pad: (i, 0)),
            ],
            out_specs=pl.BlockSpec(memory_space=pl.ANY),
            scratch_shapes=[pltpu.VMEM((TME, d), F32), pltpu.SemaphoreType.DMA],
        ),
        compiler_params=_cparams(("arbitrary",)),
        name="moe_dispatch",
    )(pad_start, dest_tiles, h2)


def _expert_body(te_ref, nu_ref, xs_ref, wg_ref, wu_ref, wd_ref, ys_ref, wgb_ref, wub_ref, wdb_ref):
    i = pl.program_id(0)
    used = i < nu_ref[0]
    new_expert = jnp.logical_or(i == 0, te_ref[i] != te_ref[jnp.maximum(i - 1, 0)])

    @pl.when(jnp.logical_and(used, new_expert))
    def _():
        wgb_ref[...] = wg_ref[0, 0].astype(BF16)
        wub_ref[...] = wu_ref[0, 0].astype(BF16)
        wdb_ref[...] = wd_ref[0, 0].astype(BF16)

    @pl.when(used)
    def _():
        xb = xs_ref[...].astype(BF16)
        hg = _dot(xb, wgb_ref[...])
        hu = _dot(xb, wub_ref[...])
        hid = (_silu(hg) * hu).astype(BF16)
        ys_ref[...] = _dot(hid, wdb_ref[...])

    @pl.when(jnp.logical_not(used))
    def _():
        ys_ref[...] = jnp.zeros_like(ys_ref)


def _experts(layer, tile_expert, n_used, xs, rows, w_gate, w_up, w_down):
    d = xs.shape[1]
    ntiles = rows // TME
    return pl.pallas_call(
        _expert_body,
        out_shape=jax.ShapeDtypeStruct((rows, d), F32),
        grid_spec=pltpu.PrefetchScalarGridSpec(
            num_scalar_prefetch=2,
            grid=(ntiles,),
            in_specs=[
                pl.BlockSpec((TME, d), lambda i, te, nu: (jnp.minimum(i, nu[0] - 1), 0)),
                pl.BlockSpec((1, 1, d, FF), lambda i, te, nu: (layer, te[i], 0, 0)),
                pl.BlockSpec((1, 1, d, FF), lambda i, te, nu: (layer, te[i], 0, 0)),
                pl.BlockSpec((1, 1, FF, d), lambda i, te, nu: (layer, te[i], 0, 0)),
            ],
            out_specs=pl.BlockSpec((TME, d), lambda i, te, nu: (i, 0)),
            scratch_shapes=[pltpu.VMEM((d, FF), BF16), pltpu.VMEM((d, FF), BF16), pltpu.VMEM((FF, d), BF16)],
        ),
        compiler_params=_cparams(("arbitrary",)),
        name="moe_experts",
    )(tile_expert, n_used, xs, w_gate, w_up, w_down)


def _gather_pairs(dest_ref, next_ref, ys_ref, buf_ref, sem):
    step = pl.program_id(0) * pl.num_programs(1) + pl.program_id(1)
    nsteps = pl.num_programs(0) * pl.num_programs(1)
    cur = lax.rem(step, 2)

    def start_tile(idx_ref, b):
        def issue(g, carry):
            for u in range(UNROLL):
                t = g * UNROLL + u
                for sl in range(2):
                    src = idx_ref[0, 0, sl * TM + t]
                    pltpu.make_async_copy(ys_ref.at[pl.ds(src, 1)], buf_ref.at[b, sl, pl.ds(t, 1)],
                                          sem.at[b]).start()
            return carry

        lax.fori_loop(0, TM // UNROLL, issue, 0)

    @pl.when(step == 0)
    def _():
        start_tile(dest_ref, 0)

    @pl.when(step + 1 < nsteps)
    def _():
        start_tile(next_ref, 1 - cur)

    for sl in range(2):
        pltpu.make_async_copy(ys_ref.at[pl.ds(0, TM)], buf_ref.at[cur, sl], sem.at[cur]).wait()
    return buf_ref.at[cur]


def _moe_combine(wc_ref, rows_ref):
    return wc_ref[:, 0:1] * rows_ref[0] + wc_ref[:, 1:2] * rows_ref[1]


def _gather_specs(nt, ntiles):
    return [
        pl.BlockSpec((1, 1, 2 * TM), lambda b, i: (b * nt + i, 0, 0), memory_space=pltpu.SMEM),
        pl.BlockSpec((1, 1, 2 * TM), lambda b, i: (jnp.minimum(b * nt + i + 1, ntiles - 1), 0, 0),
                     memory_space=pltpu.SMEM),
    ]


GATHER_SCRATCH = [pltpu.VMEM((2, 2, TM, D), F32), pltpu.SemaphoreType.DMA((2,))]


def _routing_plan(meta_i, counts, tile_tokens):
    t = meta_i.shape[1]
    cnt_perm = counts[:, 0].astype(I32)
    e = jnp.arange(N_EXP)
    cnt = cnt_perm[(e % EXP_PER_GROUP) * N_GROUPS + e // EXP_PER_GROUP]
    tiles_e = (cnt + TME - 1) // TME
    cum_tiles = jnp.cumsum(tiles_e)
    offs = (cum_tiles - tiles_e) * TME
    sel = meta_i[0:2][..., None] == e
    dest = jnp.sum(jnp.where(sel, offs, 0), axis=-1) + meta_i[2:4]
    ntiles = (2 * t) // TME + N_EXP
    n_used = cum_tiles[-1]
    te = jnp.sum(cum_tiles[None, :] <= jnp.arange(ntiles, dtype=I32)[:, None], axis=-1).astype(I32)
    last_e = jnp.max(jnp.where(tiles_e > 0, e, 0)).astype(I32)
    te = jnp.minimum(te, last_e)

    def tiles(n):
        return dest.reshape(2, t // n, n).transpose(1, 0, 2).reshape(t // n, 1, 2 * n)

    pad_start = jnp.concatenate([jnp.where(tiles_e > 0, cum_tiles - 1, -1), n_used[None]]).astype(I32)
    return pad_start, tiles(TD), tiles(tile_tokens), te, n_used.reshape(1).astype(I32), ntiles * TME


def _moe(layer, h2, meta_i, counts, w_gate, w_up, w_down):
    pad_start, dest_td, dest_tm, te, n_used, rows = _routing_plan(meta_i, counts, TM)
    xs = _dispatch(pad_start, dest_td, h2, rows)
    ys = _experts(layer, te, n_used, xs, rows, w_gate, w_up, w_down)
    return ys, dest_tm


def _conf_in_body(x_ref, dest_ref, next_ref, wc_ref, ys_ref, pv_ref, w1_ref, b1_ref, x_out_ref, u_ref,
                  buf_ref, sem):
    rows_ref = _gather_pairs(dest_ref, next_ref, ys_ref, buf_ref, sem)
    x2 = x_ref[0] + pv_ref[0, 0:1, :] * _moe_combine(wc_ref, rows_ref)
    x_out_ref[0] = x2
    h = (_rms(x2) * pv_ref[0, 1:2, :] + pv_ref[0, 2:3, :]).astype(BF16)
    u = _dot(h, w1_ref[...]) + b1_ref[0:1, :]
    u_ref[0] = (u[:, :D] * _sigmoid(u[:, D:])).astype(BF16)


def _conf_in(x1, dest_tm, wcol, ys, pvec, w1, b1):
    bsz, seq, d = x1.shape
    nt = seq // TM
    return pl.pallas_call(
        _conf_in_body,
        out_shape=(jax.ShapeDtypeStruct((bsz, seq, d), F32), jax.ShapeDtypeStruct((bsz, seq, d), BF16)),
        grid=(bsz, nt),
        in_specs=[pl.BlockSpec((1, TM, d), lambda b, i: (b, i, 0))] + _gather_specs(nt, bsz * nt) + [
            pl.BlockSpec((TM, 8), lambda b, i: (b * nt + i, 0)),
            pl.BlockSpec(memory_space=pl.ANY),
            pl.BlockSpec((1, 8, d), lambda b, i: (b, 0, 0)),
            pl.BlockSpec((d, 2 * d), lambda b, i: (0, 0)),
            pl.BlockSpec((8, 2 * d), lambda b, i: (0, 0)),
        ],
        out_specs=(pl.BlockSpec((1, TM, d), lambda b, i: (b, i, 0)),
                   pl.BlockSpec((1, TM, d), lambda b, i: (b, i, 0))),
        scratch_shapes=GATHER_SCRATCH,
        compiler_params=_cparams(("arbitrary", "arbitrary")),
        name="combine_conf_in",
    )(x1, dest_tm, dest_tm, wcol, ys, pvec, w1, b1)


def _conf_mix_body(nt, cur_ref, prev_ref, next_ref, dw_ref, cv_ref, x_ref, w2_ref, pv_ref, rwt_ref, rb_ref,
                   upper_ref, x_out_ref, h2_ref, mi_ref, mf_ref, cnt_ref, ext_ref, sh_ref, conv_ref,
                   carry_ref):
    i = pl.program_id(1)
    ext_ref[0:HALO, :] = jnp.where(i == 0, 0.0, prev_ref[0].astype(F32))
    ext_ref[HALO:HALO + TM, :] = cur_ref[0].astype(F32)
    ext_ref[HALO + TM:2 * HALO + TM, :] = jnp.where(i == nt - 1, 0.0, next_ref[0].astype(F32))
    pad = CONF_K // 2
    span = TM + CONF_SPAN
    for cb in range(D // LANES):
        cs = slice(cb * LANES, (cb + 1) * LANES)
        for r in range(1, 8):
            sh_ref[r - 1, :, cs] = ext_ref[r:r + span, cs]
        acc = None
        for j in range(CONF_K):
            lo = HALO - pad + j
            r, base = lo % 8, lo - lo % 8
            src = ext_ref[base:base + TM, cs] if r == 0 else sh_ref[r - 1, base:base + TM, cs]
            term = src * dw_ref[j:j + 1, cs]
            acc = term if acc is None else acc + term
        conv_ref[:, cs] = acc + cv_ref[0:1, cs]
    u = conv_ref[...]
    mu = jnp.mean(u, axis=-1, keepdims=True)
    uc = u - mu
    var = jnp.mean(uc * uc, axis=-1, keepdims=True)
    y = uc * lax.rsqrt(var + EPS) * cv_ref[1:2, :] + cv_ref[2:3, :]
    mix = _dot(_silu(y).astype(BF16), w2_ref[...]) + cv_ref[3:4, :]
    x_new = x_ref[0] + pv_ref[0, 0:1, :] * mix
    _route_tail(x_new, pv_ref, rwt_ref, rb_ref, upper_ref, x_out_ref, h2_ref, mi_ref, mf_ref, cnt_ref,
                carry_ref)


def _conf_mix(u, dw32, cvec, x2, w2, pvec, rwt, rb, upper):
    bsz, seq, d = x2.shape
    nt = seq // TM
    r = TM // HALO
    cst = lambda shape: pl.BlockSpec(shape, lambda b, i: (0,) * len(shape))
    return pl.pallas_call(
        functools.partial(_conf_mix_body, nt),
        out_shape=_route_out_shapes(bsz, seq),
        grid=(bsz, nt),
        in_specs=[
            pl.BlockSpec((1, TM, d), lambda b, i: (b, i, 0)),
            pl.BlockSpec((1, HALO, d), lambda b, i: (b, jnp.maximum(i * r - 1, 0), 0)),
            pl.BlockSpec((1, HALO, d), lambda b, i: (b, jnp.minimum((i + 1) * r, nt * r - 1), 0)),
            cst((32, d)), cst((8, d)),
            pl.BlockSpec((1, TM, d), lambda b, i: (b, i, 0)),
            cst((d, d)),
            pl.BlockSpec((1, 8, d), lambda b, i: (b, 0, 0)),
        ] + _route_in_specs(),
        out_specs=_route_out_specs(nt),
        scratch_shapes=[pltpu.VMEM((TM + 2 * HALO, d), F32), pltpu.VMEM((7, TM + CONF_SPAN, d), F32),
                        pltpu.VMEM((TM, d), F32),
                        pltpu.VMEM((N_EXP, LANES), F32)],
        compiler_params=_cparams(("arbitrary", "arbitrary")),
        name="conf_mix_router",
    )(u, u, u, dw32, cvec, x2, w2, pvec, rwt, rb, upper)


def _final_body(x_ref, dest_ref, next_ref, wc_ref, ys_ref, pv_ref, o_ref, buf_ref, sem):
    rows_ref = _gather_pairs(dest_ref, next_ref, ys_ref, buf_ref, sem)
    x4 = x_ref[0] + pv_ref[0, 0:1, :] * _moe_combine(wc_ref, rows_ref)
    o_ref[0] = _rms(x4) * pv_ref[0, 1:2, :]


def _final(x3, dest_tm, wcol, ys, pvec):
    bsz, seq, d = x3.shape
    nt = seq // TM
    return pl.pallas_call(
        _final_body,
        out_shape=jax.ShapeDtypeStruct((bsz, seq, d), F32),
        grid=(bsz, nt),
        in_specs=[pl.BlockSpec((1, TM, d), lambda b, i: (b, i, 0))] + _gather_specs(nt, bsz * nt) + [
            pl.BlockSpec((TM, 8), lambda b, i: (b * nt + i, 0)),
            pl.BlockSpec(memory_space=pl.ANY),
            pl.BlockSpec((1, 8, d), lambda b, i: (b, 0, 0)),
        ],
        out_specs=pl.BlockSpec((1, TM, d), lambda b, i: (b, i, 0)),
        scratch_shapes=GATHER_SCRATCH,
        compiler_params=_cparams(("arbitrary", "arbitrary")),
        name="combine_final",
    )(x3, dest_tm, dest_tm, wcol, ys, pvec)


def _grid_sincos(rows, cols, dim):
    quarter = dim // 4
    omega = 1.0 / jnp.power(POS_BASE, jnp.arange(quarter, dtype=F32) / quarter)

    def axis_emb(n):
        ang = jnp.arange(n, dtype=F32)[:, None] * omega[None, :]
        return jnp.concatenate([jnp.sin(ang), jnp.cos(ang)], axis=-1)

    er = jnp.broadcast_to(axis_emb(rows)[:, None, :], (rows, cols, dim // 2))
    ec = jnp.broadcast_to(axis_emb(cols)[None, :, :], (rows, cols, dim // 2))
    return jnp.concatenate([er, ec], axis=-1).reshape(rows * cols, dim)


def _pad_rows(a, rows):
    return jnp.pad(a, ((0, rows - a.shape[0]),) + ((0, 0),) * (a.ndim - 1))


def _per_batch(*vecs):
    st = jnp.stack(vecs, axis=1)
    return jnp.pad(st, ((0, 0), (0, 8 - st.shape[1]), (0, 0)))


def _width_dft():
    k = np.arange(LANES)
    ang = 2 * np.pi * np.outer(k, k) / LANES
    groups = FN_W // LANES
    c = np.kron(np.eye(groups), np.cos(ang)) / math.sqrt(LANES)
    s = np.kron(np.eye(groups), np.sin(ang)) / math.sqrt(LANES)
    return jnp.asarray(np.concatenate([c, s], axis=1), dtype=BF16)


def kernel(x, c, ctx, c_ctx, ada_w, ada_b, norm1_g, norm2_g, hyb_w_in, dn_conv_w, dn_a_log, dn_dt_bias,
           dn_onorm_g, hyb_w_out, conf_w1, conf_b1, conf_dw_w, conf_dw_b, conf_ln_g, conf_ln_b, conf_w2,
           conf_b2, router_w, router_bias, moe_w_gate, moe_w_up, moe_w_down, final_g):
    bsz, seq, d = x.shape
    ctx_len = ctx.shape[1]
    assert d == D and seq % TM == 0 and ctx_len % TM == 0 and (bsz * seq) % TD == 0
    nct = ctx_len // TM
    t = bsz * seq

    pos = _grid_sincos(seq // GRID_W, GRID_W, d)

    cpad = _pad_rows(jnp.concatenate([c, c_ctx[None, :]], axis=0), 16)
    mods = _adaln(cpad, ada_w, ada_b)
    m = lambda l, k: mods[l, :, k * d:(k + 1) * d]
    a1 = norm1_g[0][None, :] * (1.0 + m(0, 1))
    mod_a = jnp.stack([jnp.broadcast_to(a1[bsz], (bsz, d)), a1[:bsz]], axis=1)
    mod_b = jnp.stack([jnp.broadcast_to(m(0, 0)[bsz], (bsz, d)), m(0, 0)[:bsz]], axis=1)

    w_in = hyb_w_in[0]
    z_lo = QKV_W + AB_W
    f_lo = z_lo + Z_W
    wqkv = w_in[:, :QKV_W].astype(BF16)
    wab = jnp.pad(w_in[:, QKV_W:z_lo], ((0, 0), (0, LANES - AB_W))).astype(BF16)
    wz = w_in[:, z_lo:f_lo].astype(BF16)
    wf = w_in[:, f_lo:].astype(BF16)
    gpar = jnp.zeros((8, LANES), F32)
    gpar = gpar.at[0, :2 * HEADS].set(-jnp.exp(dn_a_log[0].reshape(-1)))
    gpar = gpar.at[1, :2 * HEADS].set(dn_dt_bias[0].reshape(-1))
    qkv_pre, z, fc, fs, gates = _inproj(ctx, x, pos, mod_a, mod_b, wqkv, wz, wf, wab, _width_dft(), gpar)

    qkv = _dnconv(qkv_pre, _pad_rows(dn_conv_w[0], 8), nct)
    lc = qkv.shape[1]
    nchunks = lc // CHUNK
    grow = gates[:, :, :AB_W].reshape(bsz, nchunks, CHUNK, 4, HEADS)
    grow = grow.transpose(0, 1, 3, 4, 2).reshape(bsz, nchunks, 4, HB)
    grow = jnp.pad(grow, ((0, 0), (0, 0), (0, 4), (0, 0)))
    o_f, o_b = _deltanet(qkv, grow, seq, ctx_len // CHUNK)

    fn = _seq_dft(fc, fs)

    perm = (np.arange(N_EXP) % N_GROUPS) * EXP_PER_GROUP + np.arange(N_EXP) // N_GROUPS
    rwt = router_w.T[perm].astype(BF16)
    rb = jnp.broadcast_to(router_bias[perm][:, None], (N_EXP, LANES)).astype(F32)
    upper = jnp.asarray(np.triu(np.ones((TM, TM), np.float32), 1), dtype=BF16)
    w_out = hyb_w_out[0].astype(BF16)
    onorm8 = _pad_rows(jnp.tile(dn_onorm_g[0], HEADS)[None, :], 8)
    pv0 = _per_batch(m(0, 2)[:bsz], norm2_g[0][None, :] * (1.0 + m(0, 4)[:bsz]), m(0, 3)[:bsz])
    x1, h2, meta_i, meta_f, counts = _mix0(o_f, o_b, z, fn, x, pos, w_out[:V_W], w_out[V_W:], onorm8, pv0,
                                           rwt, rb, upper)
    ys, dest_tm = _moe(0, h2, meta_i, counts, moe_w_gate, moe_w_up, moe_w_down)
    wcol = meta_f.T

    pv1 = _per_batch(m(0, 5)[:bsz], norm1_g[1][None, :] * (1.0 + m(1, 1)[:bsz]), m(1, 0)[:bsz])
    x2, u = _conf_in(x1, dest_tm, wcol, ys, pv1, conf_w1[0].astype(BF16), _pad_rows(conf_b1[0][None, :], 8))
    cvec = _pad_rows(jnp.stack([conf_dw_b[0], conf_ln_g[0], conf_ln_b[0], conf_b2[0]]), 8)
    pv2 = _per_batch(m(1, 2)[:bsz], norm2_g[1][None, :] * (1.0 + m(1, 4)[:bsz]), m(1, 3)[:bsz])
    x3, h2, meta_i, meta_f, counts = _conf_mix(u, _pad_rows(conf_dw_w[0], 32), cvec, x2,
                                               conf_w2[0].astype(BF16), pv2, rwt, rb, upper)
    ys, dest_tm = _moe(1, h2, meta_i, counts, moe_w_gate, moe_w_up, moe_w_down)
    pv3 = _per_batch(m(1, 5)[:bsz], jnp.broadcast_to(final_g[None, :], (bsz, d)))
    return _final(x3, dest_tm, meta_f.T, ys, pv3)
```

```python
import functools
import math

import numpy as np
import jax
import jax.numpy as jnp
from jax import lax
from jax.experimental import pallas as pl
from jax.experimental.pallas import tpu as pltpu

F32 = jnp.float32
BF16 = jnp.bfloat16
I32 = jnp.int32

D = 1024
GRID_W = 64
HEADS = 4
DK = 128
DV = 128
CHUNK = 64
QK_W = HEADS * DK
V_W = HEADS * DV
QKV_W = 2 * QK_W + V_W
AB_W = 16
Z_W = V_W
FN_W = 512
DN_CONV_K = 5
CONF_K = 31
N_EXP = 32
N_GROUPS = 8
EXP_PER_GROUP = 4
FF = 512
EPS = 1e-6
POS_BASE = 10000.0

LANES = 128
HALO = 16
TM = 256
TME = 512
HB = HEADS * CHUNK
DN_GROUP = 4
CONF_SPAN = 24
CONF_ROWS = 64
VMEM_LIMIT = 56 * 1024 * 1024


def _cparams(sem):
    return pltpu.CompilerParams(dimension_semantics=sem, vmem_limit_bytes=VMEM_LIMIT)


def _sigmoid(v):
    return jax.nn.sigmoid(v)


def _silu(v):
    return v * jax.nn.sigmoid(v)


def _dot(a, b):
    return jnp.dot(a, b, preferred_element_type=F32)


def _dot_nt(a, b):
    return lax.dot_general(a, b, (((1,), (1,)), ((), ())), preferred_element_type=F32)


def _split3(a):
    a1 = a.astype(BF16)
    r1 = a - a1.astype(F32)
    a2 = r1.astype(BF16)
    a3 = (r1 - a2.astype(F32)).astype(BF16)
    return a1, a2, a3


def _rms(v, eps=EPS):
    return v * lax.rsqrt(jnp.mean(v * v, axis=-1, keepdims=True) + eps)


def _adaln_body(c_ref, w_ref, b_ref, o_ref):
    s = _silu(c_ref[...])
    o_ref[0] = jnp.dot(s, w_ref[0], preferred_element_type=F32,
                       precision=lax.Precision.HIGHEST) + b_ref[0]


def _adaln(cpad, ada_w, ada_b):
    depth, d, n = ada_w.shape
    tn = 1536
    rows = cpad.shape[0]
    return pl.pallas_call(
        _adaln_body,
        out_shape=jax.ShapeDtypeStruct((depth, rows, n), F32),
        grid=(depth, n // tn),
        in_specs=[
            pl.BlockSpec((rows, d), lambda l, j: (0, 0)),
            pl.BlockSpec((1, d, tn), lambda l, j: (l, 0, j)),
            pl.BlockSpec((1, 1, tn), lambda l, j: (l, 0, j)),
        ],
        out_specs=pl.BlockSpec((1, rows, tn), lambda l, j: (l, 0, j)),
        compiler_params=_cparams(("arbitrary", "arbitrary")),
        name="adaln",
    )(cpad, ada_w, ada_b.reshape(depth, 1, n))


def _inproj_body(nct, ctx_ref, x_ref, pos_ref, ma_ref, mb_ref, wqkv_ref, wz_ref, wf_ref, wab_ref,
                 dftw_ref, gpar_ref, qkv_ref, z_ref, fc_ref, fs_ref, gate_ref):
    i = pl.program_id(1)
    is_ctx = i < nct
    xin = jnp.where(is_ctx, ctx_ref[0], x_ref[0] + pos_ref[...])
    a = jnp.where(is_ctx, ma_ref[0, 0:1, :], ma_ref[0, 1:2, :])
    b = jnp.where(is_ctx, mb_ref[0, 0:1, :], mb_ref[0, 1:2, :])
    h = (_rms(xin) * a + b).astype(BF16)
    qkv_ref[0] = _dot(h, wqkv_ref[...]).astype(BF16)
    z_ref[0] = _dot(h, wz_ref[...]).astype(BF16)
    f = _dot(h, wf_ref[...]).astype(BF16)
    fcs = _dot(f, dftw_ref[...])
    fc_ref[0] = fcs[:, :FN_W].astype(BF16)
    fs_ref[0] = fcs[:, FN_W:].astype(BF16)
    ab = _dot(h, wab_ref[...])
    lane = lax.broadcasted_iota(I32, ab.shape, 1)
    pre = ab + gpar_ref[1:2, :]
    softplus = jnp.maximum(pre, 0.0) + jnp.log1p(jnp.exp(-jnp.abs(pre)))
    gate_ref[0] = jnp.where(lane < 2 * HEADS, gpar_ref[0:1, :] * softplus, _sigmoid(ab))


def _inproj(ctx, x, pos, mod_a, mod_b, wqkv, wz, wf, wab, dftw, gpar):
    bsz, seq, d = x.shape
    nct = ctx.shape[1] // TM
    nt = nct + seq // TM
    lc = nt * TM
    xi = lambda b, i: (b, jnp.maximum(i - nct, 0), 0)
    full = lambda shape: pl.BlockSpec(shape, lambda b, i: (0,) * len(shape))
    return pl.pallas_call(
        functools.partial(_inproj_body, nct),
        out_shape=(
            jax.ShapeDtypeStruct((bsz, lc, QKV_W), BF16),
            jax.ShapeDtypeStruct((bsz, seq, Z_W), BF16),
            jax.ShapeDtypeStruct((bsz, seq, FN_W), BF16),
            jax.ShapeDtypeStruct((bsz, seq, FN_W), BF16),
            jax.ShapeDtypeStruct((bsz, lc, LANES), F32),
        ),
        grid=(bsz, nt),
        in_specs=[
            pl.BlockSpec((1, TM, d), lambda b, i: (b, jnp.minimum(i, nct - 1), 0)),
            pl.BlockSpec((1, TM, d), xi),
            pl.BlockSpec((TM, d), lambda b, i: (jnp.maximum(i - nct, 0), 0)),
            pl.BlockSpec((1, 2, d), lambda b, i: (b, 0, 0)),
            pl.BlockSpec((1, 2, d), lambda b, i: (b, 0, 0)),
            full((d, QKV_W)), full((d, Z_W)), full((d, FN_W)), full((d, LANES)),
            full((FN_W, 2 * FN_W)), full((8, LANES)),
        ],
        out_specs=(
            pl.BlockSpec((1, TM, QKV_W), lambda b, i: (b, i, 0)),
            pl.BlockSpec((1, TM, Z_W), xi),
            pl.BlockSpec((1, TM, FN_W), xi),
            pl.BlockSpec((1, TM, FN_W), xi),
            pl.BlockSpec((1, TM, LANES), lambda b, i: (b, i, 0)),
        ),
        compiler_params=_cparams(("arbitrary", "arbitrary")),
        name="inproj",
    )(ctx, x, pos, mod_a, mod_b, wqkv, wz, wf, wab, dftw, gpar)


def _dnconv_body(nct, nt, cur_ref, prev_ref, next_ref, w_ref, o_ref, ext_ref):
    i = pl.program_id(1)
    first = jnp.logical_or(i == 0, i == nct)
    last = jnp.logical_or(i == nct - 1, i == nt - 1)
    ext_ref[0:HALO, :] = jnp.where(first, 0.0, prev_ref[0].astype(F32))
    ext_ref[HALO:HALO + TM, :] = cur_ref[0].astype(F32)
    ext_ref[HALO + TM:2 * HALO + TM, :] = jnp.where(last, 0.0, next_ref[0].astype(F32))
    pad = DN_CONV_K // 2
    for cb in range(QKV_W // LANES):
        cs = slice(cb * LANES, (cb + 1) * LANES)
        acc = None
        for j in range(DN_CONV_K):
            lo = HALO - pad + j
            term = ext_ref[lo:lo + TM, cs] * w_ref[j:j + 1, cs]
            acc = term if acc is None else acc + term
        y = _silu(acc)
        if cb < 2 * HEADS:
            y = y * lax.rsqrt(jnp.sum(y * y, axis=-1, keepdims=True) + EPS)
        o_ref[0, :, cs] = y.astype(BF16)


def _dnconv(qkv_pre, conv_w8, nct):
    bsz, lc, w = qkv_pre.shape
    nt = lc // TM
    r = TM // HALO
    return pl.pallas_call(
        functools.partial(_dnconv_body, nct, nt),
        out_shape=jax.ShapeDtypeStruct((bsz, lc, w), BF16),
        grid=(bsz, nt),
        in_specs=[
            pl.BlockSpec((1, TM, w), lambda b, i: (b, i, 0)),
            pl.BlockSpec((1, HALO, w), lambda b, i: (b, jnp.maximum(i * r - 1, 0), 0)),
            pl.BlockSpec((1, HALO, w), lambda b, i: (b, jnp.minimum((i + 1) * r, nt * r - 1), 0)),
            pl.BlockSpec((8, w), lambda b, i: (0, 0)),
        ],
        out_specs=pl.BlockSpec((1, TM, w), lambda b, i: (b, i, 0)),
        scratch_shapes=[pltpu.VMEM((TM + 2 * HALO, w), F32)],
        compiler_params=_cparams(("arbitrary", "arbitrary")),
        name="dnconv",
    )(qkv_pre, qkv_pre, qkv_pre, conv_w8)


def _dn_setup(d, qkv, grow, mask_incl, mask_strict, trit, bdones, eye, lvl0):
    neg_inf = float("-inf")
    stack = lambda off: jnp.concatenate(
        [qkv[:, off + h * DK: off + (h + 1) * DK] for h in range(HEADS)], axis=0)
    q_s = stack(0)
    k_s = stack(QK_W)
    v_s = stack(2 * QK_W)
    g3 = _split3(grow)
    sums = sum(_dot(t, jnp.concatenate([trit, bdones], axis=1)) for t in g3)
    cg_row = sums[:, :HB]
    tot_row = sums[:, HB:]
    rows = jnp.concatenate([cg_row, tot_row, grow, jnp.zeros((LANES - 24, HB), F32)], axis=0)
    cols = rows.T
    cgc = cols[:, d:d + 1]
    totc = cols[:, 8 + d:9 + d]
    betac = cols[:, 18 + d:19 + d]
    cgr = cg_row[d:d + 1, :]
    dmat = jnp.exp(jnp.where(mask_incl > 0.5, cgc - cgr, neg_inf))
    kk = _dot_nt(k_s, k_s)
    qk = _dot_nt(q_s, k_s)
    a = jnp.where(mask_strict > 0.5, kk * dmat, 0.0) * betac
    ecg = jnp.exp(cgc)
    k32 = k_s.astype(F32)
    scale = DK ** -0.5
    return dict(
        d=d, a=a, x=eye.astype(F32) - a * lvl0, totc=totc,
        rhs=jnp.concatenate([(betac * v_s.astype(F32)).astype(BF16),
                             ((betac * ecg) * k32).astype(BF16)], axis=1),
        ktail=(k32 * jnp.exp(totc - cgc)).astype(BF16),
        qdec=(q_s.astype(F32) * (ecg * scale)).astype(BF16),
        qkd=(qk * dmat * scale).astype(BF16))


def _dn_body(qf_ref, qb_ref, gf_ref, gb_ref, mli_ref, mls_ref, mui_ref, mus_ref, bdl_ref, bdu_ref,
             bdo_ref, eye_ref, lvl_ref, of_ref, ob_ref, s_ref):
    @pl.when(pl.program_id(1) == 0)
    def _():
        s_ref[...] = jnp.zeros_like(s_ref)

    bdl = bdl_ref[...]
    bdu = bdu_ref[...]
    bdo = bdo_ref[...]
    eye = eye_ref[...]
    chains = []
    for g in range(qf_ref.shape[0]):
        c = _dn_setup(0, qf_ref[g], gf_ref[g, 0], mli_ref[...], mls_ref[...], bdu, bdo, eye, lvl_ref[0])
        c.update(g=g, out=of_ref)
        chains.append(c)
        c = _dn_setup(1, qb_ref[g], gb_ref[g, 0], mui_ref[...], mus_ref[...], bdl, bdo, eye, lvl_ref[0])
        c.update(g=g, out=ob_ref)
        chains.append(c)
    for lvl in range(1, int(math.log2(CHUNK))):
        for c in chains:
            c["xb"] = c["x"].astype(BF16)
            c["y"] = _dot((c["a"] * lvl_ref[lvl]).astype(BF16), c["xb"]).astype(BF16)
        for c in chains:
            c["x"] = c["x"] - _dot(c["xb"], c["y"])
    for c in chains:
        uw = _dot(c["x"].astype(BF16), c["rhs"])
        c["u"] = uw[:, :DV]
        c["w"] = uw[:, DV:].astype(BF16)
    for c in chains:
        c["vn"] = []
        c["oi"] = []
    for h in range(HEADS):
        rs = slice(h * CHUNK, (h + 1) * CHUNK)
        for c in chains:
            s = s_ref[c["g"], c["d"], h]
            sb = s.astype(BF16)
            c["s"] = s
            c["vn"].append(c["u"][rs] - _dot(c["w"][rs], sb))
            c["oi"].append(_dot(c["qdec"][rs], sb))
        for c in chains:
            cd = jnp.exp(c["totc"][h * CHUNK:h * CHUNK + 1, :])
            upd = lax.dot_general(c["ktail"][rs], c["vn"][h].astype(BF16), (((0,), (0,)), ((), ())),
                                  preferred_element_type=F32)
            s_ref[c["g"], c["d"], h] = c["s"] * cd + upd
    for c in chains:
        vn_s = jnp.concatenate(c["vn"], axis=0).astype(BF16)
        o_s = jnp.concatenate(c["oi"], axis=0) + _dot(c["qkd"], vn_s)
        c["out"][c["g"]] = jnp.concatenate(
            [o_s[h * CHUNK:(h + 1) * CHUNK] for h in range(HEADS)], axis=1).astype(BF16)


def _dn_consts():
    r = np.arange(HB)
    same = (r[:, None] // CHUNK) == (r[None, :] // CHUNK)
    li = same & (r[:, None] >= r[None, :])
    ls = same & (r[:, None] > r[None, :])
    ui = same & (r[:, None] <= r[None, :])
    us = same & (r[:, None] < r[None, :])
    f = lambda m: jnp.asarray(m.astype(np.float32))
    h = lambda m: jnp.asarray(m.astype(np.float32), dtype=BF16)
    nlev = int(math.log2(CHUNK))
    lv = [((r[:, None] >> (l + 1)) == (r[None, :] >> (l + 1))) & ((r[:, None] >> l) != (r[None, :] >> l))
          for l in range(nlev)]
    return (f(li), f(ls), f(ui), f(us), h(li), h(ui), h(same), h(np.eye(HB, dtype=bool)),
            jnp.asarray(np.stack(lv).astype(np.float32)))


def _deltanet(qkv, grow, seq, ncc):
    bsz, lc, w = qkv.shape
    nchunks = lc // CHUNK
    nx = seq // CHUNK
    fwd = lambda s: s
    bwd = lambda s: jnp.where(s < ncc, ncc - 1 - s, ncc + nchunks - 1 - s)
    ofi = lambda b, s: (b, jnp.maximum(s - ncc, 0), 0)
    obi = lambda b, s: (b, jnp.minimum(nchunks - 1 - s, nx - 1), 0)
    cst = lambda shape: pl.BlockSpec(shape, lambda b, s: (0, 0))
    consts = _dn_consts()
    grp = math.gcd(bsz, DN_GROUP)
    return pl.pallas_call(
        _dn_body,
        out_shape=(jax.ShapeDtypeStruct((bsz, seq, V_W), BF16),
                   jax.ShapeDtypeStruct((bsz, seq, V_W), BF16)),
        grid=(bsz // grp, nchunks),
        in_specs=[
            pl.BlockSpec((grp, CHUNK, w), lambda b, s: (b, fwd(s), 0)),
            pl.BlockSpec((grp, CHUNK, w), lambda b, s: (b, bwd(s), 0)),
            pl.BlockSpec((grp, 1, 8, HB), lambda b, s: (b, fwd(s), 0, 0)),
            pl.BlockSpec((grp, 1, 8, HB), lambda b, s: (b, bwd(s), 0, 0)),
        ] + [cst((HB, HB))] * 8 + [pl.BlockSpec((int(math.log2(CHUNK)), HB, HB), lambda b, s: (0, 0, 0))],
        out_specs=(pl.BlockSpec((grp, CHUNK, V_W), ofi), pl.BlockSpec((grp, CHUNK, V_W), obi)),
        scratch_shapes=[pltpu.VMEM((grp, 2, HEADS, DK, DV), F32)],
        compiler_params=_cparams(("arbitrary", "arbitrary")),
        name="deltanet",
    )(qkv, qkv, grow, grow, *consts)


def _fft1_body(ns2, fc_ref, fs_ref, m_ref, cos_ref, sin_ref, y_ref):
    n1 = fc_ref.shape[1]
    xin = jnp.concatenate([fc_ref[0], fs_ref[0]], axis=0)
    y = _dot(m_ref[...], xin)
    yr, yi = y[:n1], y[n1:]
    for q in range(ns2):
        cs = slice(q * FN_W, (q + 1) * FN_W)
        c = jnp.concatenate([cos_ref[q]] * (FN_W // LANES), axis=1)
        s = jnp.concatenate([sin_ref[q]] * (FN_W // LANES), axis=1)
        y_ref[0, 0, :, cs] = (yr[:, cs] * c + yi[:, cs] * s).astype(BF16)
        y_ref[0, 1, :, cs] = (yi[:, cs] * c - yr[:, cs] * s).astype(BF16)


def _fft2_body(ns1, y_ref, cs_ref, o_ref):
    for q in range(ns1):
        rhs = jnp.concatenate([y_ref[0, 0, q], y_ref[0, 1, q]], axis=0)
        o_ref[0, :, q * FN_W:(q + 1) * FN_W] = _dot(cs_ref[...], rhs).astype(BF16)


def _seq_dft(fc, fs):
    bsz, seq, w = fc.shape
    n2 = GRID_W
    n1 = seq // n2
    k1 = np.arange(n1)
    ang1 = 2 * np.pi * np.outer(k1, k1) / n1
    c1, s1 = np.cos(ang1), np.sin(ang1)
    m1 = jnp.asarray(np.block([[c1, -s1], [-s1, -c1]]), dtype=BF16)
    tw = 2 * np.pi * np.outer(np.arange(n2), k1) / seq
    cos_t = jnp.asarray(np.repeat(np.cos(tw)[:, :, None], LANES, axis=2), dtype=F32)
    sin_t = jnp.asarray(np.repeat(np.sin(tw)[:, :, None], LANES, axis=2), dtype=F32)
    k2 = np.arange(n2)
    ang2 = 2 * np.pi * np.outer(k2, k2) / n2
    cs2 = jnp.asarray(np.concatenate([np.cos(ang2), np.sin(ang2)], axis=1) / math.sqrt(seq), dtype=BF16)

    ns2 = 4
    y = pl.pallas_call(
        functools.partial(_fft1_body, ns2),
        out_shape=jax.ShapeDtypeStruct((bsz, 2, n1, n2 * w), BF16),
        grid=(bsz, n2 // ns2),
        in_specs=[
            pl.BlockSpec((1, n1, ns2 * w), lambda b, j: (b, 0, j)),
            pl.BlockSpec((1, n1, ns2 * w), lambda b, j: (b, 0, j)),
            pl.BlockSpec((2 * n1, 2 * n1), lambda b, j: (0, 0)),
            pl.BlockSpec((ns2, n1, LANES), lambda b, j: (j, 0, 0)),
            pl.BlockSpec((ns2, n1, LANES), lambda b, j: (j, 0, 0)),
        ],
        out_specs=pl.BlockSpec((1, 2, n1, ns2 * w), lambda b, j: (b, 0, 0, j)),
        compiler_params=_cparams(("arbitrary", "arbitrary")),
        name="seqdft_stage1",
    )(fc.reshape(bsz, n1, n2 * w), fs.reshape(bsz, n1, n2 * w), m1, cos_t, sin_t)

    ns1 = min(8, n1)
    out = pl.pallas_call(
        functools.partial(_fft2_body, ns1),
        out_shape=jax.ShapeDtypeStruct((bsz, n2, n1 * w), BF16),
        grid=(bsz, n1 // ns1),
        in_specs=[
            pl.BlockSpec((1, 2, ns1, n2, w), lambda b, j: (b, 0, j, 0, 0)),
            pl.BlockSpec((n2, 2 * n2), lambda b, j: (0, 0)),
        ],
        out_specs=pl.BlockSpec((1, n2, ns1 * w), lambda b, j: (b, 0, j)),
        compiler_params=_cparams(("arbitrary", "arbitrary")),
        name="seqdft_stage2",
    )(y.reshape(bsz, 2, n1, n2, w), cs2)
    return out.reshape(bsz, seq, w)


def _route_tail(x_new, pv_ref, rwt_ref, rb_ref, upper_ref, x_out_ref, h2_ref, mi_ref, mf_ref, cnt_ref,
                carry_ref):
    first = jnp.logical_and(pl.program_id(0) == 0, pl.program_id(1) == 0)

    @pl.when(first)
    def _():
        carry_ref[...] = jnp.zeros_like(carry_ref)

    x_out_ref[0] = x_new
    h2 = _rms(x_new) * pv_ref[0, 1:2, :] + pv_ref[0, 2:3, :]
    h2_ref[...] = h2
    logits = _dot_nt(rwt_ref[...], h2.astype(BF16))
    scores = _sigmoid(logits)
    sel = scores + rb_ref[:, 0:1]
    a = [sel[N_GROUPS * j:N_GROUPS * (j + 1)] for j in range(EXP_PER_GROUP)]
    sc = [scores[N_GROUPS * j:N_GROUPS * (j + 1)] for j in range(EXP_PER_GROUP)]
    gs = None
    for j0 in range(EXP_PER_GROUP):
        for j1 in range(j0 + 1, EXP_PER_GROUP):
            pair = a[j0] + a[j1]
            gs = pair if gs is None else jnp.maximum(gs, pair)
    gidx = lax.broadcasted_iota(I32, gs.shape, 0).astype(F32)
    gmax = jnp.max(gs, axis=0, keepdims=True)
    best = jnp.min(jnp.where(gs == gmax, gidx, float(N_GROUPS)), axis=0, keepdims=True)
    inb = gidx == best
    v = [jnp.sum(jnp.where(inb, a[j], 0.0), axis=0, keepdims=True) for j in range(EXP_PER_GROUP)]
    s = [jnp.sum(jnp.where(inb, sc[j], 0.0), axis=0, keepdims=True) for j in range(EXP_PER_GROUP)]
    v1, i1, s1 = v[0], jnp.zeros_like(v[0]), s[0]
    for j in range(1, EXP_PER_GROUP):
        better = v[j] > v1
        v1 = jnp.where(better, v[j], v1)
        i1 = jnp.where(better, float(j), i1)
        s1 = jnp.where(better, s[j], s1)
    v2 = jnp.full_like(v1, float("-inf"))
    i2 = jnp.zeros_like(v1)
    s2 = jnp.zeros_like(v1)
    for j in range(EXP_PER_GROUP):
        better = jnp.logical_and(i1 != float(j), v[j] > v2)
        v2 = jnp.where(better, v[j], v2)
        i2 = jnp.where(better, float(j), i2)
        s2 = jnp.where(better, s[j], s2)
    denom = s1 + s2
    rid = lax.broadcasted_iota(I32, logits.shape, 0).astype(F32)
    oh1 = rid == (i1 * N_GROUPS + best)
    oh2 = rid == (i2 * N_GROUPS + best)
    oh = jnp.where(jnp.logical_or(oh1, oh2), 1.0, 0.0)
    prefix = _dot(oh.astype(BF16), upper_ref[...])
    base = carry_ref[:, 0:1] + prefix
    rank1 = jnp.sum(jnp.where(oh1, base, 0.0), axis=0, keepdims=True)
    rank2 = jnp.sum(jnp.where(oh2, base, 0.0), axis=0, keepdims=True)
    carry_new = carry_ref[...] + jnp.sum(oh, axis=1, keepdims=True)
    carry_ref[...] = carry_new
    cnt_ref[...] = carry_new
    mi_ref[...] = jnp.zeros_like(mi_ref)
    mi_ref[0:1, :] = (best * EXP_PER_GROUP + i1).astype(I32)
    mi_ref[1:2, :] = (best * EXP_PER_GROUP + i2).astype(I32)
    mi_ref[2:3, :] = rank1.astype(I32)
    mi_ref[3:4, :] = rank2.astype(I32)
    mf_ref[...] = jnp.zeros_like(mf_ref)
    mf_ref[0:1, :] = s1 / denom
    mf_ref[1:2, :] = s2 / denom


def _route_out_shapes(bsz, seq):
    t = bsz * seq
    return (
        jax.ShapeDtypeStruct((bsz, seq, D), F32),
        jax.ShapeDtypeStruct((t, D), F32),
        jax.ShapeDtypeStruct((8, t), I32),
        jax.ShapeDtypeStruct((8, t), F32),
        jax.ShapeDtypeStruct((N_EXP, LANES), F32),
    )


def _route_out_specs(nt):
    flat = lambda b, i: (b * nt + i, 0)
    return (
        pl.BlockSpec((1, TM, D), lambda b, i: (b, i, 0)),
        pl.BlockSpec((TM, D), flat),
        pl.BlockSpec((8, TM), lambda b, i: (0, b * nt + i)),
        pl.BlockSpec((8, TM), lambda b, i: (0, b * nt + i)),
        pl.BlockSpec((N_EXP, LANES), lambda b, i: (0, 0)),
    )


def _route_in_specs():
    return [
        pl.BlockSpec((N_EXP, D), lambda b, i: (0, 0)),
        pl.BlockSpec((N_EXP, LANES), lambda b, i: (0, 0)),
        pl.BlockSpec((TM, TM), lambda b, i: (0, 0)),
    ]


def _mix0_body(of_ref, ob_ref, z_ref, fn_ref, x_ref, pos_ref, wdn_ref, wfn_ref, on_ref, pv_ref,
               rwt_ref, rb_ref, upper_ref, x_out_ref, h2_ref, mi_ref, mf_ref, cnt_ref, carry_ref):
    o = of_ref[0].astype(F32) + ob_ref[0].astype(F32)
    z = z_ref[0].astype(F32)
    parts = []
    for h in range(HEADS):
        cs = slice(h * DV, (h + 1) * DV)
        parts.append(_rms(o[:, cs]) * on_ref[0:1, cs] * _silu(z[:, cs]))
    dn = jnp.concatenate(parts, axis=1).astype(BF16)
    mix = _dot(dn, wdn_ref[...]) + _dot(fn_ref[0], wfn_ref[...])
    x_new = x_ref[0] + pos_ref[...] + pv_ref[0, 0:1, :] * mix
    _route_tail(x_new, pv_ref, rwt_ref, rb_ref, upper_ref, x_out_ref, h2_ref, mi_ref, mf_ref, cnt_ref,
                carry_ref)


def _mix0(o_f, o_b, z, fn, x, pos, wdn, wfn, onorm8, pvec, rwt, rb, upper):
    bsz, seq, d = x.shape
    nt = seq // TM
    tok = lambda w: pl.BlockSpec((1, TM, w), lambda b, i: (b, i, 0))
    cst = lambda shape: pl.BlockSpec(shape, lambda b, i: (0,) * len(shape))
    return pl.pallas_call(
        _mix0_body,
        out_shape=_route_out_shapes(bsz, seq),
        grid=(bsz, nt),
        in_specs=[tok(V_W), tok(V_W), tok(Z_W), tok(FN_W), tok(d),
                  pl.BlockSpec((TM, d), lambda b, i: (i, 0)),
                  cst((V_W, d)), cst((FN_W, d)), cst((8, V_W)),
                  pl.BlockSpec((1, 8, d), lambda b, i: (b, 0, 0))] + _route_in_specs(),
        out_specs=_route_out_specs(nt),
        scratch_shapes=[pltpu.VMEM((N_EXP, LANES), F32)],
        compiler_params=_cparams(("arbitrary", "arbitrary")),
        name="mix0_router",
    )(o_f, o_b, z, fn, x, pos, wdn, wfn, onorm8, pvec, rwt, rb, upper)


TD = 512


def _dispatch_body(pad_ref, dest_ref, h2_ref, xs_ref, zero_ref, sem):
    ntiles = xs_ref.shape[0] // TME

    @pl.when(pl.program_id(0) == 0)
    def _():
        zero_ref[...] = jnp.zeros_like(zero_ref)

        def zero_tile(k):
            return pltpu.make_async_copy(zero_ref, xs_ref.at[pl.ds(pl.multiple_of(k * TME, TME), TME)], sem)

        def fill_tail(k, carry):
            zero_tile(k).start()
            return carry

        def drain_tail(k, carry):
            zero_tile(k).wait()
            return carry

        for e in range(N_EXP):
            @pl.when(pad_ref[e] >= 0)
            def _():
                zero_tile(pad_ref[e]).start()
        lax.fori_loop(pad_ref[N_EXP], ntiles, fill_tail, 0)
        for e in range(N_EXP):
            @pl.when(pad_ref[e] >= 0)
            def _():
                zero_tile(pad_ref[e]).wait()
        lax.fori_loop(pad_ref[N_EXP], ntiles, drain_tail, 0)

    for t in range(TD):
        for sl in range(2):
            dst = dest_ref[0, 0, sl * TD + t]
            pltpu.make_async_copy(h2_ref.at[pl.ds(t, 1)], xs_ref.at[pl.ds(dst, 1)], sem).start(priority=sl)
    for sl in range(2):
        pltpu.make_async_copy(h2_ref, xs_ref.at[pl.ds(0, TD)], sem).wait()


def _dispatch(pad_start, dest_tiles, h2, rows):
    t, d = h2.shape
    return pl.pallas_call(
        _dispatch_body,
        out_shape=jax.ShapeDtypeStruct((rows, d), F32),
        grid_spec=pltpu.PrefetchScalarGridSpec(
            num_scalar_prefetch=1,
            grid=(t // TD,),
            in_specs=[
                pl.BlockSpec((1, 1, 2 * TD), lambda i, pad: (i, 0, 0), memory_space=pltpu.SMEM),
                pl.BlockSpec((TD, d), lambda i, pad: (i, 0)),
            ],
            out_specs=pl.BlockSpec(memory_space=pl.ANY),
            scratch_shapes=[pltpu.VMEM((TME, d), F32), pltpu.SemaphoreType.DMA],
        ),
        compiler_params=_cparams(("arbitrary",)),
        name="moe_dispatch",
    )(pad_start, dest_tiles, h2)


def _expert_body(te_ref, nu_ref, xs_ref, wg_ref, wu_ref, wd_ref, ys_ref, wgb_ref, wub_ref, wdb_ref):
    i = pl.program_id(0)
    used = i < nu_ref[0]
    new_expert = jnp.logical_or(i == 0, te_ref[i] != te_ref[jnp.maximum(i - 1, 0)])

    @pl.when(jnp.logical_and(used, new_expert))
    def _():
        wgb_ref[...] = wg_ref[0, 0].astype(BF16)
        wub_ref[...] = wu_ref[0, 0].astype(BF16)
        wdb_ref[...] = wd_ref[0, 0].astype(BF16)

    @pl.when(used)
    def _():
        nblk = 2
        rb = TME // nblk
        xb = [xs_ref[k * rb:(k + 1) * rb, :].astype(BF16) for k in range(nblk)]
        hg = [_dot(v, wgb_ref[...]) for v in xb]
        hu = [_dot(v, wub_ref[...]) for v in xb]
        hid = [(_silu(g) * u).astype(BF16) for g, u in zip(hg, hu)]
        for k in range(nblk):
            ys_ref[k * rb:(k + 1) * rb, :] = _dot(hid[k], wdb_ref[...])

    @pl.when(jnp.logical_not(used))
    def _():
        ys_ref[...] = jnp.zeros_like(ys_ref)


def _experts(layer, tile_expert, n_used, xs, rows, w_gate, w_up, w_down):
    d = xs.shape[1]
    ntiles = rows // TME
    return pl.pallas_call(
        _expert_body,
        out_shape=jax.ShapeDtypeStruct((rows, d), F32),
        grid_spec=pltpu.PrefetchScalarGridSpec(
            num_scalar_prefetch=2,
            grid=(ntiles,),
            in_specs=[
                pl.BlockSpec((TME, d), lambda i, te, nu: (jnp.minimum(i, nu[0] - 1), 0)),
                pl.BlockSpec((1, 1, d, FF), lambda i, te, nu: (layer, te[i], 0, 0)),
                pl.BlockSpec((1, 1, d, FF), lambda i, te, nu: (layer, te[i], 0, 0)),
                pl.BlockSpec((1, 1, FF, d), lambda i, te, nu: (layer, te[i], 0, 0)),
            ],
            out_specs=pl.BlockSpec((TME, d), lambda i, te, nu: (i, 0)),
            scratch_shapes=[pltpu.VMEM((d, FF), BF16), pltpu.VMEM((d, FF), BF16), pltpu.VMEM((FF, d), BF16)],
        ),
        compiler_params=_cparams(("arbitrary",)),
        name="moe_experts",
    )(tile_expert, n_used, xs, w_gate, w_up, w_down)


def _gather_pairs(dest_ref, next_ref, ys_ref, buf_ref, sem):
    step = pl.program_id(0) * pl.num_programs(1) + pl.program_id(1)
    nsteps = pl.num_programs(0) * pl.num_programs(1)
    cur = lax.rem(step, 2)

    def start_tile(idx_ref, b):
        for t in range(TM):
            for sl in range(2):
                src = idx_ref[0, 0, sl * TM + t]
                pltpu.make_async_copy(ys_ref.at[pl.ds(src, 1)], buf_ref.at[b, sl, pl.ds(t, 1)],
                                      sem.at[b]).start(priority=sl)

    @pl.when(step == 0)
    def _():
        start_tile(dest_ref, 0)

    for b in range(2):
        @pl.when(jnp.logical_and(step + 1 < nsteps, cur == 1 - b))
        def _():
            start_tile(next_ref, b)

    for sl in range(2):
        pltpu.make_async_copy(ys_ref.at[pl.ds(0, TM)], buf_ref.at[cur, sl], sem.at[cur]).wait()
    return buf_ref.at[cur]


def _moe_combine(wc_ref, rows_ref):
    return wc_ref[:, 0:1] * rows_ref[0] + wc_ref[:, 1:2] * rows_ref[1]


def _gather_specs(nt, ntiles):
    return [
        pl.BlockSpec((1, 1, 2 * TM), lambda b, i: (b * nt + i, 0, 0), memory_space=pltpu.SMEM),
        pl.BlockSpec((1, 1, 2 * TM), lambda b, i: (jnp.minimum(b * nt + i + 1, ntiles - 1), 0, 0),
                     memory_space=pltpu.SMEM),
    ]


GATHER_SCRATCH = [pltpu.VMEM((2, 2, TM, D), F32), pltpu.SemaphoreType.DMA((2,))]


def _routing_plan(meta_i, counts, tile_tokens):
    t = meta_i.shape[1]
    cnt_perm = counts[:, 0].astype(I32)
    e = jnp.arange(N_EXP)
    cnt = cnt_perm[(e % EXP_PER_GROUP) * N_GROUPS + e // EXP_PER_GROUP]
    tiles_e = (cnt + TME - 1) // TME
    cum_tiles = jnp.cumsum(tiles_e)
    offs = (cum_tiles - tiles_e) * TME
    sel = meta_i[0:2][..., None] == e
    dest = jnp.sum(jnp.where(sel, offs, 0), axis=-1) + meta_i[2:4]
    ntiles = (2 * t) // TME + N_EXP
    n_used = cum_tiles[-1]
    te = jnp.sum(cum_tiles[None, :] <= jnp.arange(ntiles, dtype=I32)[:, None], axis=-1).astype(I32)
    last_e = jnp.max(jnp.where(tiles_e > 0, e, 0)).astype(I32)
    te = jnp.minimum(te, last_e)

    def tiles(n):
        return dest.reshape(2, t // n, n).transpose(1, 0, 2).reshape(t // n, 1, 2 * n)

    pad_start = jnp.concatenate([jnp.where(tiles_e > 0, cum_tiles - 1, -1), n_used[None]]).astype(I32)
    return pad_start, tiles(TD), tiles(tile_tokens), te, n_used.reshape(1).astype(I32), ntiles * TME


def _moe(layer, h2, meta_i, counts, w_gate, w_up, w_down):
    pad_start, dest_td, dest_tm, te, n_used, rows = _routing_plan(meta_i, counts, TM)
    xs = _dispatch(pad_start, dest_td, h2, rows)
    ys = _experts(layer, te, n_used, xs, rows, w_gate, w_up, w_down)
    return ys, dest_tm


def _conf_in_body(x_ref, dest_ref, next_ref, wc_ref, ys_ref, pv_ref, w1_ref, b1_ref, x_out_ref, u_ref,
                  buf_ref, sem):
    rows_ref = _gather_pairs(dest_ref, next_ref, ys_ref, buf_ref, sem)
    x2 = x_ref[0] + pv_ref[0, 0:1, :] * _moe_combine(wc_ref, rows_ref)
    x_out_ref[0] = x2
    h = (_rms(x2) * pv_ref[0, 1:2, :] + pv_ref[0, 2:3, :]).astype(BF16)
    u = _dot(h, w1_ref[...]) + b1_ref[0:1, :]
    u_ref[0] = (u[:, :D] * _sigmoid(u[:, D:])).astype(BF16)


def _conf_in(x1, dest_tm, wcol, ys, pvec, w1, b1):
    bsz, seq, d = x1.shape
    nt = seq // TM
    return pl.pallas_call(
        _conf_in_body,
        out_shape=(jax.ShapeDtypeStruct((bsz, seq, d), F32), jax.ShapeDtypeStruct((bsz, seq, d), BF16)),
        grid=(bsz, nt),
        in_specs=[pl.BlockSpec((1, TM, d), lambda b, i: (b, i, 0))] + _gather_specs(nt, bsz * nt) + [
            pl.BlockSpec((TM, 8), lambda b, i: (b * nt + i, 0)),
            pl.BlockSpec(memory_space=pl.ANY),
            pl.BlockSpec((1, 8, d), lambda b, i: (b, 0, 0)),
            pl.BlockSpec((d, 2 * d), lambda b, i: (0, 0)),
            pl.BlockSpec((8, 2 * d), lambda b, i: (0, 0)),
        ],
        out_specs=(pl.BlockSpec((1, TM, d), lambda b, i: (b, i, 0)),
                   pl.BlockSpec((1, TM, d), lambda b, i: (b, i, 0))),
        scratch_shapes=GATHER_SCRATCH,
        compiler_params=_cparams(("arbitrary", "arbitrary")),
        name="combine_conf_in",
    )(x1, dest_tm, dest_tm, wcol, ys, pvec, w1, b1)


def _conf_mix_body(nt, cur_ref, prev_ref, next_ref, dw_ref, cv_ref, x_ref, w2_ref, pv_ref, rwt_ref,
                   rb_ref, upper_ref, x_out_ref, h2_ref, mi_ref, mf_ref, cnt_ref, ext_ref, sh_ref,
                   conv_ref, carry_ref):
    i = pl.program_id(1)
    ext_ref[0:HALO, :] = jnp.where(i == 0, 0.0, prev_ref[0].astype(F32))
    ext_ref[HALO:HALO + TM, :] = cur_ref[0].astype(F32)
    ext_ref[HALO + TM:2 * HALO + TM, :] = jnp.where(i == nt - 1, 0.0, next_ref[0].astype(F32))
    pad = CONF_K // 2
    span = TM + CONF_SPAN
    for cb in range(D // LANES):
        cs = slice(cb * LANES, (cb + 1) * LANES)
        for r in range(1, 8):
            sh_ref[r - 1, :, cs] = ext_ref[r:r + span, cs]
        for rb in range(0, TM, CONF_ROWS):
            acc = None
            for j in range(CONF_K):
                lo = HALO - pad + j
                r, base = lo % 8, lo - lo % 8 + rb
                src = ext_ref[base:base + CONF_ROWS, cs] if r == 0 else sh_ref[r - 1, base:base + CONF_ROWS, cs]
                term = src * dw_ref[j:j + 1, cs]
                acc = term if acc is None else acc + term
            conv_ref[rb:rb + CONF_ROWS, cs] = acc + cv_ref[0:1, cs]
    u = conv_ref[...]
    mu = jnp.mean(u, axis=-1, keepdims=True)
    uc = u - mu
    var = jnp.mean(uc * uc, axis=-1, keepdims=True)
    y = uc * lax.rsqrt(var + EPS) * cv_ref[1:2, :] + cv_ref[2:3, :]
    mix = _dot(_silu(y).astype(BF16), w2_ref[...]) + cv_ref[3:4, :]
    x_new = x_ref[0] + pv_ref[0, 0:1, :] * mix
    _route_tail(x_new, pv_ref, rwt_ref, rb_ref, upper_ref, x_out_ref, h2_ref, mi_ref, mf_ref, cnt_ref,
                carry_ref)


def _conf_mix(u, dw32, cvec, x2, w2, pvec, rwt, rb, upper):
    bsz, seq, d = x2.shape
    nt = seq // TM
    r = TM // HALO
    cst = lambda shape: pl.BlockSpec(shape, lambda b, i: (0,) * len(shape))
    return pl.pallas_call(
        functools.partial(_conf_mix_body, nt),
        out_shape=_route_out_shapes(bsz, seq),
        grid=(bsz, nt),
        in_specs=[
            pl.BlockSpec((1, TM, d), lambda b, i: (b, i, 0)),
            pl.BlockSpec((1, HALO, d), lambda b, i: (b, jnp.maximum(i * r - 1, 0), 0)),
            pl.BlockSpec((1, HALO, d), lambda b, i: (b, jnp.minimum((i + 1) * r, nt * r - 1), 0)),
            cst((32, d)), cst((8, d)),
            pl.BlockSpec((1, TM, d), lambda b, i: (b, i, 0)),
            cst((d, d)),
            pl.BlockSpec((1, 8, d), lambda b, i: (b, 0, 0)),
        ] + _route_in_specs(),
        out_specs=_route_out_specs(nt),
        scratch_shapes=[pltpu.VMEM((TM + 2 * HALO, d), F32), pltpu.VMEM((7, TM + CONF_SPAN, d), F32),
                        pltpu.VMEM((TM, d), F32),
                        pltpu.VMEM((N_EXP, LANES), F32)],
        compiler_params=_cparams(("arbitrary", "arbitrary")),
        name="conf_mix_router",
    )(u, u, u, dw32, cvec, x2, w2, pvec, rwt, rb, upper)


def _final_body(x_ref, dest_ref, next_ref, wc_ref, ys_ref, pv_ref, o_ref, buf_ref, sem):
    rows_ref = _gather_pairs(dest_ref, next_ref, ys_ref, buf_ref, sem)
    x4 = x_ref[0] + pv_ref[0, 0:1, :] * _moe_combine(wc_ref, rows_ref)
    o_ref[0] = _rms(x4) * pv_ref[0, 1:2, :]


def _final(x3, dest_tm, wcol, ys, pvec):
    bsz, seq, d = x3.shape
    nt = seq // TM
    return pl.pallas_call(
        _final_body,
        out_shape=jax.ShapeDtypeStruct((bsz, seq, d), F32),
        grid=(bsz, nt),
        in_specs=[pl.BlockSpec((1, TM, d), lambda b, i: (b, i, 0))] + _gather_specs(nt, bsz * nt) + [
            pl.BlockSpec((TM, 8), lambda b, i: (b * nt + i, 0)),
            pl.BlockSpec(memory_space=pl.ANY),
            pl.BlockSpec((1, 8, d), lambda b, i: (b, 0, 0)),
        ],
        out_specs=pl.BlockSpec((1, TM, d), lambda b, i: (b, i, 0)),
        scratch_shapes=GATHER_SCRATCH,
        compiler_params=_cparams(("arbitrary", "arbitrary")),
        name="combine_final",
    )(x3, dest_tm, dest_tm, wcol, ys, pvec)


def _grid_sincos(rows, cols, dim):
    quarter = dim // 4
    omega = 1.0 / jnp.power(POS_BASE, jnp.arange(quarter, dtype=F32) / quarter)

    def axis_emb(n):
        ang = jnp.arange(n, dtype=F32)[:, None] * omega[None, :]
        return jnp.concatenate([jnp.sin(ang), jnp.cos(ang)], axis=-1)

    er = jnp.broadcast_to(axis_emb(rows)[:, None, :], (rows, cols, dim // 2))
    ec = jnp.broadcast_to(axis_emb(cols)[None, :, :], (rows, cols, dim // 2))
    return jnp.concatenate([er, ec], axis=-1).reshape(rows * cols, dim)


def _pad_rows(a, rows):
    return jnp.pad(a, ((0, rows - a.shape[0]),) + ((0, 0),) * (a.ndim - 1))


def _per_batch(*vecs):
    st = jnp.stack(vecs, axis=1)
    return jnp.pad(st, ((0, 0), (0, 8 - st.shape[1]), (0, 0)))


def _width_dft():
    k = np.arange(LANES)
    ang = 2 * np.pi * np.outer(k, k) / LANES
    groups = FN_W // LANES
    c = np.kron(np.eye(groups), np.cos(ang)) / math.sqrt(LANES)
    s = np.kron(np.eye(groups), np.sin(ang)) / math.sqrt(LANES)
    return jnp.asarray(np.concatenate([c, s], axis=1), dtype=BF16)


def kernel(x, c, ctx, c_ctx, ada_w, ada_b, norm1_g, norm2_g, hyb_w_in, dn_conv_w, dn_a_log, dn_dt_bias,
           dn_onorm_g, hyb_w_out, conf_w1, conf_b1, conf_dw_w, conf_dw_b, conf_ln_g, conf_ln_b, conf_w2,
           conf_b2, router_w, router_bias, moe_w_gate, moe_w_up, moe_w_down, final_g):
    bsz, seq, d = x.shape
    ctx_len = ctx.shape[1]
    assert d == D and seq % TM == 0 and ctx_len % TM == 0 and (bsz * seq) % TD == 0
    nct = ctx_len // TM
    t = bsz * seq

    pos = _grid_sincos(seq // GRID_W, GRID_W, d)

    cpad = _pad_rows(jnp.concatenate([c, c_ctx[None, :]], axis=0), 16)
    mods = _adaln(cpad, ada_w, ada_b)
    m = lambda l, k: mods[l, :, k * d:(k + 1) * d]
    a1 = norm1_g[0][None, :] * (1.0 + m(0, 1))
    mod_a = jnp.stack([jnp.broadcast_to(a1[bsz], (bsz, d)), a1[:bsz]], axis=1)
    mod_b = jnp.stack([jnp.broadcast_to(m(0, 0)[bsz], (bsz, d)), m(0, 0)[:bsz]], axis=1)

    w_in = hyb_w_in[0]
    z_lo = QKV_W + AB_W
    f_lo = z_lo + Z_W
    wqkv = w_in[:, :QKV_W].astype(BF16)
    wab = jnp.pad(w_in[:, QKV_W:z_lo], ((0, 0), (0, LANES - AB_W))).astype(BF16)
    wz = w_in[:, z_lo:f_lo].astype(BF16)
    wf = w_in[:, f_lo:].astype(BF16)
    gpar = jnp.zeros((8, LANES), F32)
    gpar = gpar.at[0, :2 * HEADS].set(-jnp.exp(dn_a_log[0].reshape(-1)))
    gpar = gpar.at[1, :2 * HEADS].set(dn_dt_bias[0].reshape(-1))
    qkv_pre, z, fc, fs, gates = _inproj(ctx, x, pos, mod_a, mod_b, wqkv, wz, wf, wab, _width_dft(), gpar)

    qkv = _dnconv(qkv_pre, _pad_rows(dn_conv_w[0], 8), nct)
    lc = qkv.shape[1]
    nchunks = lc // CHUNK
    grow = gates[:, :, :AB_W].reshape(bsz, nchunks, CHUNK, 4, HEADS)
    grow = grow.transpose(0, 1, 3, 4, 2).reshape(bsz, nchunks, 4, HB)
    grow = jnp.pad(grow, ((0, 0), (0, 0), (0, 4), (0, 0)))
    o_f, o_b = _deltanet(qkv, grow, seq, ctx_len // CHUNK)

    fn = _seq_dft(fc, fs)

    perm = (np.arange(N_EXP) % N_GROUPS) * EXP_PER_GROUP + np.arange(N_EXP) // N_GROUPS
    rwt = router_w.T[perm].astype(BF16)
    rb = jnp.broadcast_to(router_bias[perm][:, None], (N_EXP, LANES)).astype(F32)
    upper = jnp.asarray(np.triu(np.ones((TM, TM), np.float32), 1), dtype=BF16)
    w_out = hyb_w_out[0].astype(BF16)
    onorm8 = _pad_rows(jnp.tile(dn_onorm_g[0], HEADS)[None, :], 8)
    pv0 = _per_batch(m(0, 2)[:bsz], norm2_g[0][None, :] * (1.0 + m(0, 4)[:bsz]), m(0, 3)[:bsz])
    x1, h2, meta_i, meta_f, counts = _mix0(o_f, o_b, z, fn, x, pos, w_out[:V_W], w_out[V_W:], onorm8, pv0,
                                           rwt, rb, upper)
    ys, dest_tm = _moe(0, h2, meta_i, counts, moe_w_gate, moe_w_up, moe_w_down)
    wcol = meta_f.T

    pv1 = _per_batch(m(0, 5)[:bsz], norm1_g[1][None, :] * (1.0 + m(1, 1)[:bsz]), m(1, 0)[:bsz])
    x2, u = _conf_in(x1, dest_tm, wcol, ys, pv1, conf_w1[0].astype(BF16), _pad_rows(conf_b1[0][None, :], 8))
    cvec = _pad_rows(jnp.stack([conf_dw_b[0], conf_ln_g[0], conf_ln_b[0], conf_b2[0]]), 8)
    pv2 = _per_batch(m(1, 2)[:bsz], norm2_g[1][None, :] * (1.0 + m(1, 4)[:bsz]), m(1, 3)[:bsz])
    x3, h2, meta_i, meta_f, counts = _conf_mix(u, _pad_rows(conf_dw_w[0], 32), cvec, x2,
                                               conf_w2[0].astype(BF16), pv2, rwt, rb, upper)
    ys, dest_tm = _moe(1, h2, meta_i, counts, moe_w_gate, moe_w_up, moe_w_down)
    pv3 = _per_batch(m(1, 5)[:bsz], jnp.broadcast_to(final_g[None, :], (bsz, d)))
    return _final(x3, dest_tm, meta_f.T, ys, pv3)
```

```python
import functools
import math

import numpy as np
import jax
import jax.numpy as jnp
from jax import lax
from jax.experimental import pallas as pl
from jax.experimental.pallas import tpu as pltpu

F32 = jnp.float32
BF16 = jnp.bfloat16
I32 = jnp.int32
U32 = jnp.uint32

D = 1024
GRID_W = 64
HEADS = 4
DK = 128
DV = 128
CHUNK = 64
QK_W = HEADS * DK
V_W = HEADS * DV
QKV_W = 2 * QK_W + V_W
AB_W = 16
Z_W = V_W
FN_W = 512
DN_CONV_K = 5
CONF_K = 31
N_EXP = 32
N_GROUPS = 8
EXP_PER_GROUP = 4
FF = 512
EPS = 1e-6
POS_BASE = 10000.0

LANES = 128
HALO = 16
TM = 256
TME = 512
HB = HEADS * CHUNK
DN_STACK = 2
HS = DN_STACK * CHUNK
DN_GROUP = 4
CONF_SPAN = 24
CONF_ROWS = 64
VMEM_LIMIT = 56 * 1024 * 1024


def _cparams(sem):
    return pltpu.CompilerParams(dimension_semantics=sem, vmem_limit_bytes=VMEM_LIMIT)


def _sigmoid(v):
    return jax.nn.sigmoid(v)


def _silu(v):
    return v * jax.nn.sigmoid(v)


def _dot(a, b):
    return jnp.dot(a, b, preferred_element_type=F32)


def _dot_nt(a, b):
    return lax.dot_general(a, b, (((1,), (1,)), ((), ())), preferred_element_type=F32)


def _split3(a):
    a1 = a.astype(BF16)
    r1 = a - a1.astype(F32)
    a2 = r1.astype(BF16)
    a3 = (r1 - a2.astype(F32)).astype(BF16)
    return a1, a2, a3


def _pack_pairs(v):
    n = v.shape[1] // 2
    lo = lax.bitcast_convert_type(v[:, :n].astype(BF16).astype(F32), U32) >> 16
    hi = lax.bitcast_convert_type(v[:, n:].astype(BF16).astype(F32), U32) & jnp.uint32(0xFFFF0000)
    return hi | lo


def _unpack_pairs(p):
    lo = lax.bitcast_convert_type(p << 16, F32)
    hi = lax.bitcast_convert_type(p & jnp.uint32(0xFFFF0000), F32)
    return jnp.concatenate([lo, hi], axis=1)


def _rms(v, eps=EPS):
    return v * lax.rsqrt(jnp.mean(v * v, axis=-1, keepdims=True) + eps)


def _adaln_body(c_ref, w_ref, b_ref, o_ref):
    s = _silu(c_ref[...])
    o_ref[0] = jnp.dot(s, w_ref[0], preferred_element_type=F32,
                       precision=lax.Precision.HIGHEST) + b_ref[0]


def _adaln(cpad, ada_w, ada_b):
    depth, d, n = ada_w.shape
    tn = 1536
    rows = cpad.shape[0]
    return pl.pallas_call(
        _adaln_body,
        out_shape=jax.ShapeDtypeStruct((depth, rows, n), F32),
        grid=(depth, n // tn),
        in_specs=[
            pl.BlockSpec((rows, d), lambda l, j: (0, 0)),
            pl.BlockSpec((1, d, tn), lambda l, j: (l, 0, j)),
            pl.BlockSpec((1, 1, tn), lambda l, j: (l, 0, j)),
        ],
        out_specs=pl.BlockSpec((1, rows, tn), lambda l, j: (l, 0, j)),
        compiler_params=_cparams(("arbitrary", "arbitrary")),
        name="adaln",
    )(cpad, ada_w, ada_b.reshape(depth, 1, n))


def _inproj_body(nct, ctx_ref, x_ref, pos_ref, ma_ref, mb_ref, wqkv_ref, wz_ref, wf_ref, wab_ref,
                 dftw_ref, gpar_ref, qkv_ref, z_ref, fc_ref, fs_ref, gate_ref):
    i = pl.program_id(1)
    is_ctx = i < nct
    xin = jnp.where(is_ctx, ctx_ref[0], x_ref[0] + pos_ref[...])
    a = jnp.where(is_ctx, ma_ref[0, 0:1, :], ma_ref[0, 1:2, :])
    b = jnp.where(is_ctx, mb_ref[0, 0:1, :], mb_ref[0, 1:2, :])
    h = (_rms(xin) * a + b).astype(BF16)
    qkv_ref[0] = _dot(h, wqkv_ref[...]).astype(BF16)
    z_ref[0] = _dot(h, wz_ref[...]).astype(BF16)
    f = _dot(h, wf_ref[...]).astype(BF16)
    fcs = _dot(f, dftw_ref[...])
    fc_ref[0] = fcs[:, :FN_W].astype(BF16)
    fs_ref[0] = fcs[:, FN_W:].astype(BF16)
    ab = _dot(h, wab_ref[...])
    lane = lax.broadcasted_iota(I32, ab.shape, 1)
    pre = ab + gpar_ref[1:2, :]
    softplus = jnp.maximum(pre, 0.0) + jnp.log1p(jnp.exp(-jnp.abs(pre)))
    gate_ref[0] = jnp.where(lane < 2 * HEADS, gpar_ref[0:1, :] * softplus, _sigmoid(ab))


def _inproj(ctx, x, pos, mod_a, mod_b, wqkv, wz, wf, wab, dftw, gpar):
    bsz, seq, d = x.shape
    nct = ctx.shape[1] // TM
    nt = nct + seq // TM
    lc = nt * TM
    xi = lambda b, i: (b, jnp.maximum(i - nct, 0), 0)
    full = lambda shape: pl.BlockSpec(shape, lambda b, i: (0,) * len(shape))
    return pl.pallas_call(
        functools.partial(_inproj_body, nct),
        out_shape=(
            jax.ShapeDtypeStruct((bsz, lc, QKV_W), BF16),
            jax.ShapeDtypeStruct((bsz, seq, Z_W), BF16),
            jax.ShapeDtypeStruct((bsz, seq, FN_W), BF16),
            jax.ShapeDtypeStruct((bsz, seq, FN_W), BF16),
            jax.ShapeDtypeStruct((bsz, lc, LANES), F32),
        ),
        grid=(bsz, nt),
        in_specs=[
            pl.BlockSpec((1, TM, d), lambda b, i: (b, jnp.minimum(i, nct - 1), 0)),
            pl.BlockSpec((1, TM, d), xi),
            pl.BlockSpec((TM, d), lambda b, i: (jnp.maximum(i - nct, 0), 0)),
            pl.BlockSpec((1, 2, d), lambda b, i: (b, 0, 0)),
            pl.BlockSpec((1, 2, d), lambda b, i: (b, 0, 0)),
            full((d, QKV_W)), full((d, Z_W)), full((d, FN_W)), full((d, LANES)),
            full((FN_W, 2 * FN_W)), full((8, LANES)),
        ],
        out_specs=(
            pl.BlockSpec((1, TM, QKV_W), lambda b, i: (b, i, 0)),
            pl.BlockSpec((1, TM, Z_W), xi),
            pl.BlockSpec((1, TM, FN_W), xi),
            pl.BlockSpec((1, TM, FN_W), xi),
            pl.BlockSpec((1, TM, LANES), lambda b, i: (b, i, 0)),
        ),
        compiler_params=_cparams(("arbitrary", "arbitrary")),
        name="inproj",
    )(ctx, x, pos, mod_a, mod_b, wqkv, wz, wf, wab, dftw, gpar)


def _dnconv_body(nct, nt, cur_ref, prev_ref, next_ref, w_ref, o_ref, ext_ref):
    i = pl.program_id(1)
    first = jnp.logical_or(i == 0, i == nct)
    last = jnp.logical_or(i == nct - 1, i == nt - 1)
    ext_ref[0:HALO, :] = jnp.where(first, 0.0, prev_ref[0].astype(F32))
    ext_ref[HALO:HALO + TM, :] = cur_ref[0].astype(F32)
    ext_ref[HALO + TM:2 * HALO + TM, :] = jnp.where(last, 0.0, next_ref[0].astype(F32))
    pad = DN_CONV_K // 2
    for cb in range(QKV_W // LANES):
        cs = slice(cb * LANES, (cb + 1) * LANES)
        acc = None
        for j in range(DN_CONV_K):
            lo = HALO - pad + j
            term = ext_ref[lo:lo + TM, cs] * w_ref[j:j + 1, cs]
            acc = term if acc is None else acc + term
        y = _silu(acc)
        if cb < 2 * HEADS:
            y = y * lax.rsqrt(jnp.sum(y * y, axis=-1, keepdims=True) + EPS)
        o_ref[0, :, cs] = y.astype(BF16)


def _dnconv(qkv_pre, conv_w8, nct):
    bsz, lc, w = qkv_pre.shape
    nt = lc // TM
    r = TM // HALO
    return pl.pallas_call(
        functools.partial(_dnconv_body, nct, nt),
        out_shape=jax.ShapeDtypeStruct((bsz, lc, w), BF16),
        grid=(bsz, nt),
        in_specs=[
            pl.BlockSpec((1, TM, w), lambda b, i: (b, i, 0)),
            pl.BlockSpec((1, HALO, w), lambda b, i: (b, jnp.maximum(i * r - 1, 0), 0)),
            pl.BlockSpec((1, HALO, w), lambda b, i: (b, jnp.minimum((i + 1) * r, nt * r - 1), 0)),
            pl.BlockSpec((8, w), lambda b, i: (0, 0)),
        ],
        out_specs=pl.BlockSpec((1, TM, w), lambda b, i: (b, i, 0)),
        scratch_shapes=[pltpu.VMEM((TM + 2 * HALO, w), F32)],
        compiler_params=_cparams(("arbitrary", "arbitrary")),
        name="dnconv",
    )(qkv_pre, qkv_pre, qkv_pre, conv_w8)


def _dn_setup(d, st, qkv, grow, mask_incl, mask_strict, trit, bdones, eye, lvl0):
    neg_inf = float("-inf")
    heads = range(st * DN_STACK, (st + 1) * DN_STACK)
    stack = lambda off: jnp.concatenate([qkv[:, off + h * DK: off + (h + 1) * DK] for h in heads], axis=0)
    q_s = stack(0)
    k_s = stack(QK_W)
    v_s = stack(2 * QK_W)
    grow = grow[:, st * HS:(st + 1) * HS]
    g3 = _split3(grow)
    sums = sum(_dot(t, jnp.concatenate([trit, bdones], axis=1)) for t in g3)
    cg_row = sums[:, :HS]
    tot_row = sums[:, HS:]
    rows = jnp.concatenate([cg_row, tot_row, grow, jnp.zeros((LANES - 24, HS), F32)], axis=0)
    cols = rows.T
    cgc = cols[:, d:d + 1]
    totc = cols[:, 8 + d:9 + d]
    betac = cols[:, 18 + d:19 + d]
    cgr = cg_row[d:d + 1, :]
    dmat = jnp.exp(jnp.where(mask_incl > 0.5, cgc - cgr, neg_inf))
    kk = _dot_nt(k_s, k_s)
    qk = _dot_nt(q_s, k_s)
    a = jnp.where(mask_strict > 0.5, kk * dmat, 0.0) * betac
    ecg = jnp.exp(cgc)
    k32 = k_s.astype(F32)
    scale = DK ** -0.5
    return dict(
        d=d, st=st, a=a, x=eye.astype(F32) - a * lvl0, totc=totc,
        rhs=jnp.concatenate([(betac * v_s.astype(F32)).astype(BF16),
                             ((betac * ecg) * k32).astype(BF16)], axis=1),
        ktail=(k32 * jnp.exp(totc - cgc)).astype(BF16),
        qdec=(q_s.astype(F32) * (ecg * scale)).astype(BF16),
        qkd=(qk * dmat * scale).astype(BF16))


def _dn_body(qf_ref, qb_ref, gf_ref, gb_ref, mli_ref, mls_ref, mui_ref, mus_ref, bdl_ref, bdu_ref,
             bdo_ref, eye_ref, lvl_ref, of_ref, ob_ref, s_ref):
    @pl.when(pl.program_id(1) == 0)
    def _():
        s_ref[...] = jnp.zeros_like(s_ref)

    bdl = bdl_ref[...]
    bdu = bdu_ref[...]
    bdo = bdo_ref[...]
    eye = eye_ref[...]
    chains = []
    for g in range(qf_ref.shape[0]):
        for st in range(HEADS // DN_STACK):
            c = _dn_setup(0, st, qf_ref[g], gf_ref[g, 0], mli_ref[...], mls_ref[...], bdu, bdo, eye,
                          lvl_ref[0])
            c.update(g=g, out=of_ref)
            chains.append(c)
            c = _dn_setup(1, st, qb_ref[g], gb_ref[g, 0], mui_ref[...], mus_ref[...], bdl, bdo, eye,
                          lvl_ref[0])
            c.update(g=g, out=ob_ref)
            chains.append(c)
    for lvl in range(1, int(math.log2(CHUNK))):
        for c in chains:
            c["xb"] = c["x"].astype(BF16)
            c["y"] = _dot((c["a"] * lvl_ref[lvl]).astype(BF16), c["xb"]).astype(BF16)
        for c in chains:
            c["x"] = c["x"] - _dot(c["xb"], c["y"])
    for c in chains:
        uw = _dot(c["x"].astype(BF16), c["rhs"])
        c["u"] = uw[:, :DV]
        c["w"] = uw[:, DV:].astype(BF16)
    for c in chains:
        c["vn"] = []
        c["oi"] = []
    for hh in range(DN_STACK):
        rs = slice(hh * CHUNK, (hh + 1) * CHUNK)
        for c in chains:
            s = s_ref[c["g"], c["d"], c["st"] * DN_STACK + hh]
            sb = s.astype(BF16)
            c["s"] = s
            c["vn"].append(c["u"][rs] - _dot(c["w"][rs], sb))
            c["oi"].append(_dot(c["qdec"][rs], sb))
        for c in chains:
            cd = jnp.exp(c["totc"][hh * CHUNK:hh * CHUNK + 1, :])
            upd = lax.dot_general(c["ktail"][rs], c["vn"][hh].astype(BF16), (((0,), (0,)), ((), ())),
                                  preferred_element_type=F32)
            s_ref[c["g"], c["d"], c["st"] * DN_STACK + hh] = c["s"] * cd + upd
    for c in chains:
        vn_s = jnp.concatenate(c["vn"], axis=0).astype(BF16)
        o_s = jnp.concatenate(c["oi"], axis=0) + _dot(c["qkd"], vn_s)
        for hh in range(DN_STACK):
            h = c["st"] * DN_STACK + hh
            c["out"][c["g"], :, h * DV:(h + 1) * DV] = o_s[hh * CHUNK:(hh + 1) * CHUNK].astype(BF16)


def _dn_consts():
    r = np.arange(HS)
    same = (r[:, None] // CHUNK) == (r[None, :] // CHUNK)
    li = same & (r[:, None] >= r[None, :])
    ls = same & (r[:, None] > r[None, :])
    ui = same & (r[:, None] <= r[None, :])
    us = same & (r[:, None] < r[None, :])
    f = lambda m: jnp.asarray(m.astype(np.float32))
    h = lambda m: jnp.asarray(m.astype(np.float32), dtype=BF16)
    nlev = int(math.log2(CHUNK))
    lv = [((r[:, None] >> (l + 1)) == (r[None, :] >> (l + 1))) & ((r[:, None] >> l) != (r[None, :] >> l))
          for l in range(nlev)]
    return (f(li), f(ls), f(ui), f(us), h(li), h(ui), h(same), h(np.eye(HS, dtype=bool)),
            jnp.asarray(np.stack(lv).astype(np.float32)))


def _deltanet(qkv, grow, seq, ncc):
    bsz, lc, w = qkv.shape
    nchunks = lc // CHUNK
    nx = seq // CHUNK
    fwd = lambda s: s
    bwd = lambda s: jnp.where(s < ncc, ncc - 1 - s, ncc + nchunks - 1 - s)
    ofi = lambda b, s: (b, jnp.maximum(s - ncc, 0), 0)
    obi = lambda b, s: (b, jnp.minimum(nchunks - 1 - s, nx - 1), 0)
    cst = lambda shape: pl.BlockSpec(shape, lambda b, s: (0, 0))
    consts = _dn_consts()
    grp = math.gcd(bsz, DN_GROUP)
    return pl.pallas_call(
        _dn_body,
        out_shape=(jax.ShapeDtypeStruct((bsz, seq, V_W), BF16),
                   jax.ShapeDtypeStruct((bsz, seq, V_W), BF16)),
        grid=(bsz // grp, nchunks),
        in_specs=[
            pl.BlockSpec((grp, CHUNK, w), lambda b, s: (b, fwd(s), 0)),
            pl.BlockSpec((grp, CHUNK, w), lambda b, s: (b, bwd(s), 0)),
            pl.BlockSpec((grp, 1, 8, HB), lambda b, s: (b, fwd(s), 0, 0)),
            pl.BlockSpec((grp, 1, 8, HB), lambda b, s: (b, bwd(s), 0, 0)),
        ] + [cst((HS, HS))] * 8 + [pl.BlockSpec((int(math.log2(CHUNK)), HS, HS), lambda b, s: (0, 0, 0))],
        out_specs=(pl.BlockSpec((grp, CHUNK, V_W), ofi), pl.BlockSpec((grp, CHUNK, V_W), obi)),
        scratch_shapes=[pltpu.VMEM((grp, 2, HEADS, DK, DV), F32)],
        compiler_params=_cparams(("arbitrary", "arbitrary")),
        name="deltanet",
    )(qkv, qkv, grow, grow, *consts)


def _fft1_body(ns2, fc_ref, fs_ref, m_ref, cos_ref, sin_ref, y_ref):
    n1 = fc_ref.shape[1]
    xin = jnp.concatenate([fc_ref[0], fs_ref[0]], axis=0)
    y = _dot(m_ref[...], xin)
    yr, yi = y[:n1], y[n1:]
    for q in range(ns2):
        cs = slice(q * FN_W, (q + 1) * FN_W)
        c = jnp.concatenate([cos_ref[q]] * (FN_W // LANES), axis=1)
        s = jnp.concatenate([sin_ref[q]] * (FN_W // LANES), axis=1)
        y_ref[0, 0, :, cs] = (yr[:, cs] * c + yi[:, cs] * s).astype(BF16)
        y_ref[0, 1, :, cs] = (yi[:, cs] * c - yr[:, cs] * s).astype(BF16)


def _fft2_body(ns1, y_ref, cs_ref, o_ref):
    for q in range(ns1):
        rhs = jnp.concatenate([y_ref[0, 0, q], y_ref[0, 1, q]], axis=0)
        o_ref[0, :, q * FN_W:(q + 1) * FN_W] = _dot(cs_ref[...], rhs).astype(BF16)


def _seq_dft(fc, fs):
    bsz, seq, w = fc.shape
    n2 = GRID_W
    n1 = seq // n2
    k1 = np.arange(n1)
    ang1 = 2 * np.pi * np.outer(k1, k1) / n1
    c1, s1 = np.cos(ang1), np.sin(ang1)
    m1 = jnp.asarray(np.block([[c1, -s1], [-s1, -c1]]), dtype=BF16)
    tw = 2 * np.pi * np.outer(np.arange(n2), k1) / seq
    cos_t = jnp.asarray(np.repeat(np.cos(tw)[:, :, None], LANES, axis=2), dtype=F32)
    sin_t = jnp.asarray(np.repeat(np.sin(tw)[:, :, None], LANES, axis=2), dtype=F32)
    k2 = np.arange(n2)
    ang2 = 2 * np.pi * np.outer(k2, k2) / n2
    cs2 = jnp.asarray(np.concatenate([np.cos(ang2), np.sin(ang2)], axis=1) / math.sqrt(seq), dtype=BF16)

    ns2 = 4
    y = pl.pallas_call(
        functools.partial(_fft1_body, ns2),
        out_shape=jax.ShapeDtypeStruct((bsz, 2, n1, n2 * w), BF16),
        grid=(bsz, n2 // ns2),
        in_specs=[
            pl.BlockSpec((1, n1, ns2 * w), lambda b, j: (b, 0, j)),
            pl.BlockSpec((1, n1, ns2 * w), lambda b, j: (b, 0, j)),
            pl.BlockSpec((2 * n1, 2 * n1), lambda b, j: (0, 0)),
            pl.BlockSpec((ns2, n1, LANES), lambda b, j: (j, 0, 0)),
            pl.BlockSpec((ns2, n1, LANES), lambda b, j: (j, 0, 0)),
        ],
        out_specs=pl.BlockSpec((1, 2, n1, ns2 * w), lambda b, j: (b, 0, 0, j)),
        compiler_params=_cparams(("arbitrary", "arbitrary")),
        name="seqdft_stage1",
    )(fc.reshape(bsz, n1, n2 * w), fs.reshape(bsz, n1, n2 * w), m1, cos_t, sin_t)

    ns1 = min(8, n1)
    out = pl.pallas_call(
        functools.partial(_fft2_body, ns1),
        out_shape=jax.ShapeDtypeStruct((bsz, n2, n1 * w), BF16),
        grid=(bsz, n1 // ns1),
        in_specs=[
            pl.BlockSpec((1, 2, ns1, n2, w), lambda b, j: (b, 0, j, 0, 0)),
            pl.BlockSpec((n2, 2 * n2), lambda b, j: (0, 0)),
        ],
        out_specs=pl.BlockSpec((1, n2, ns1 * w), lambda b, j: (b, 0, j)),
        compiler_params=_cparams(("arbitrary", "arbitrary")),
        name="seqdft_stage2",
    )(y.reshape(bsz, 2, n1, n2, w), cs2)
    return out.reshape(bsz, seq, w)


def _route_tail(x_new, pv_ref, rwt_ref, rb_ref, upper_ref, x_out_ref, h2_ref, mi_ref, mf_ref, cnt_ref,
                carry_ref):
    first = jnp.logical_and(pl.program_id(0) == 0, pl.program_id(1) == 0)

    @pl.when(first)
    def _():
        carry_ref[...] = jnp.zeros_like(carry_ref)

    x_out_ref[0] = x_new
    h2 = _rms(x_new) * pv_ref[0, 1:2, :] + pv_ref[0, 2:3, :]
    h2_ref[...] = _pack_pairs(h2)
    logits = _dot_nt(rwt_ref[...], h2.astype(BF16))
    scores = _sigmoid(logits)
    sel = scores + rb_ref[:, 0:1]
    a = [sel[N_GROUPS * j:N_GROUPS * (j + 1)] for j in range(EXP_PER_GROUP)]
    sc = [scores[N_GROUPS * j:N_GROUPS * (j + 1)] for j in range(EXP_PER_GROUP)]
    gs = None
    for j0 in range(EXP_PER_GROUP):
        for j1 in range(j0 + 1, EXP_PER_GROUP):
            pair = a[j0] + a[j1]
            gs = pair if gs is None else jnp.maximum(gs, pair)
    gidx = lax.broadcasted_iota(I32, gs.shape, 0).astype(F32)
    gmax = jnp.max(gs, axis=0, keepdims=True)
    best = jnp.min(jnp.where(gs == gmax, gidx, float(N_GROUPS)), axis=0, keepdims=True)
    inb = gidx == best
    v = [jnp.sum(jnp.where(inb, a[j], 0.0), axis=0, keepdims=True) for j in range(EXP_PER_GROUP)]
    s = [jnp.sum(jnp.where(inb, sc[j], 0.0), axis=0, keepdims=True) for j in range(EXP_PER_GROUP)]
    v1, i1, s1 = v[0], jnp.zeros_like(v[0]), s[0]
    for j in range(1, EXP_PER_GROUP):
        better = v[j] > v1
        v1 = jnp.where(better, v[j], v1)
        i1 = jnp.where(better, float(j), i1)
        s1 = jnp.where(better, s[j], s1)
    v2 = jnp.full_like(v1, float("-inf"))
    i2 = jnp.zeros_like(v1)
    s2 = jnp.zeros_like(v1)
    for j in range(EXP_PER_GROUP):
        better = jnp.logical_and(i1 != float(j), v[j] > v2)
        v2 = jnp.where(better, v[j], v2)
        i2 = jnp.where(better, float(j), i2)
        s2 = jnp.where(better, s[j], s2)
    denom = s1 + s2
    rid = lax.broadcasted_iota(I32, logits.shape, 0).astype(F32)
    oh1 = rid == (i1 * N_GROUPS + best)
    oh2 = rid == (i2 * N_GROUPS + best)
    oh = jnp.where(jnp.logical_or(oh1, oh2), 1.0, 0.0)
    prefix = _dot(oh.astype(BF16), upper_ref[...])
    base = carry_ref[:, 0:1] + prefix
    rank1 = jnp.sum(jnp.where(oh1, base, 0.0), axis=0, keepdims=True)
    rank2 = jnp.sum(jnp.where(oh2, base, 0.0), axis=0, keepdims=True)
    carry_new = carry_ref[...] + jnp.sum(oh, axis=1, keepdims=True)
    carry_ref[...] = carry_new
    cnt_ref[...] = carry_new
    mi_ref[...] = jnp.zeros_like(mi_ref)
    mi_ref[0:1, :] = (best * EXP_PER_GROUP + i1).astype(I32)
    mi_ref[1:2, :] = (best * EXP_PER_GROUP + i2).astype(I32)
    mi_ref[2:3, :] = rank1.astype(I32)
    mi_ref[3:4, :] = rank2.astype(I32)
    mf_ref[...] = jnp.zeros_like(mf_ref)
    mf_ref[0:1, :] = s1 / denom
    mf_ref[1:2, :] = s2 / denom


def _route_out_shapes(bsz, seq):
    t = bsz * seq
    return (
        jax.ShapeDtypeStruct((bsz, seq, D), F32),
        jax.ShapeDtypeStruct((t, D // 2), U32),
        jax.ShapeDtypeStruct((8, t), I32),
        jax.ShapeDtypeStruct((8, t), F32),
        jax.ShapeDtypeStruct((N_EXP, LANES), F32),
    )


def _route_out_specs(nt):
    flat = lambda b, i: (b * nt + i, 0)
    return (
        pl.BlockSpec((1, TM, D), lambda b, i: (b, i, 0)),
        pl.BlockSpec((TM, D // 2), flat),
        pl.BlockSpec((8, TM), lambda b, i: (0, b * nt + i)),
        pl.BlockSpec((8, TM), lambda b, i: (0, b * nt + i)),
        pl.BlockSpec((N_EXP, LANES), lambda b, i: (0, 0)),
    )


def _route_in_specs():
    return [
        pl.BlockSpec((N_EXP, D), lambda b, i: (0, 0)),
        pl.BlockSpec((N_EXP, LANES), lambda b, i: (0, 0)),
        pl.BlockSpec((TM, TM), lambda b, i: (0, 0)),
    ]


def _mix0_body(of_ref, ob_ref, z_ref, fn_ref, x_ref, pos_ref, wdn_ref, wfn_ref, on_ref, pv_ref,
               rwt_ref, rb_ref, upper_ref, x_out_ref, h2_ref, mi_ref, mf_ref, cnt_ref, carry_ref):
    o = of_ref[0].astype(F32) + ob_ref[0].astype(F32)
    z = z_ref[0].astype(F32)
    parts = []
    for h in range(HEADS):
        cs = slice(h * DV, (h + 1) * DV)
        parts.append(_rms(o[:, cs]) * on_ref[0:1, cs] * _silu(z[:, cs]))
    dn = jnp.concatenate(parts, axis=1).astype(BF16)
    mix = _dot(dn, wdn_ref[...]) + _dot(fn_ref[0], wfn_ref[...])
    x_new = x_ref[0] + pos_ref[...] + pv_ref[0, 0:1, :] * mix
    _route_tail(x_new, pv_ref, rwt_ref, rb_ref, upper_ref, x_out_ref, h2_ref, mi_ref, mf_ref, cnt_ref,
                carry_ref)


def _mix0(o_f, o_b, z, fn, x, pos, wdn, wfn, onorm8, pvec, rwt, rb, upper):
    bsz, seq, d = x.shape
    nt = seq // TM
    tok = lambda w: pl.BlockSpec((1, TM, w), lambda b, i: (b, i, 0))
    cst = lambda shape: pl.BlockSpec(shape, lambda b, i: (0,) * len(shape))
    return pl.pallas_call(
        _mix0_body,
        out_shape=_route_out_shapes(bsz, seq),
        grid=(bsz, nt),
        in_specs=[tok(V_W), tok(V_W), tok(Z_W), tok(FN_W), tok(d),
                  pl.BlockSpec((TM, d), lambda b, i: (i, 0)),
                  cst((V_W, d)), cst((FN_W, d)), cst((8, V_W)),
                  pl.BlockSpec((1, 8, d), lambda b, i: (b, 0, 0))] + _route_in_specs(),
        out_specs=_route_out_specs(nt),
        scratch_shapes=[pltpu.VMEM((N_EXP, LANES), F32)],
        compiler_params=_cparams(("arbitrary", "arbitrary")),
        name="mix0_router",
    )(o_f, o_b, z, fn, x, pos, wdn, wfn, onorm8, pvec, rwt, rb, upper)


TD = 512


def _dispatch_body(pad_ref, dest_ref, h2_ref, xs_ref, zero_ref, sem):
    ntiles = xs_ref.shape[0] // TME

    @pl.when(pl.program_id(0) == 0)
    def _():
        zero_ref[...] = jnp.zeros_like(zero_ref)

        def zero_tile(k):
            return pltpu.make_async_copy(zero_ref, xs_ref.at[pl.ds(pl.multiple_of(k * TME, TME), TME)], sem)

        def fill_tail(k, carry):
            zero_tile(k).start()
            return carry

        def drain_tail(k, carry):
            zero_tile(k).wait()
            return carry

        for e in range(N_EXP):
            @pl.when(pad_ref[e] >= 0)
            def _():
                zero_tile(pad_ref[e]).start()
        lax.fori_loop(pad_ref[N_EXP], ntiles, fill_tail, 0)
        for e in range(N_EXP):
            @pl.when(pad_ref[e] >= 0)
            def _():
                zero_tile(pad_ref[e]).wait()
        lax.fori_loop(pad_ref[N_EXP], ntiles, drain_tail, 0)

    for t in range(TD):
        for sl in range(2):
            dst = dest_ref[0, 0, sl * TD + t]
            pltpu.make_async_copy(h2_ref.at[pl.ds(t, 1)], xs_ref.at[pl.ds(dst, 1)], sem).start(priority=sl)
    for sl in range(2):
        pltpu.make_async_copy(h2_ref, xs_ref.at[pl.ds(0, TD)], sem).wait()


def _dispatch(pad_start, dest_tiles, h2, rows):
    t, d = h2.shape
    return pl.pallas_call(
        _dispatch_body,
        out_shape=jax.ShapeDtypeStruct((rows, d), h2.dtype),
        grid_spec=pltpu.PrefetchScalarGridSpec(
            num_scalar_prefetch=1,
            grid=(t // TD,),
            in_specs=[
                pl.BlockSpec((1, 1, 2 * TD), lambda i, pad: (i, 0, 0), memory_space=pltpu.SMEM),
                pl.BlockSpec((TD, d), lambda i, pad: (i, 0)),
            ],
            out_specs=pl.BlockSpec(memory_space=pl.ANY),
            scratch_shapes=[pltpu.VMEM((TME, d), h2.dtype), pltpu.SemaphoreType.DMA],
        ),
        compiler_params=_cparams(("arbitrary",)),
        name="moe_dispatch",
    )(pad_start, dest_tiles, h2)


def _expert_body(te_ref, nu_ref, xs_ref, wg_ref, wu_ref, wd_ref, ys_ref, wgb_ref, wub_ref, wdb_ref):
    i = pl.program_id(0)
    used = i < nu_ref[0]
    new_expert = jnp.logical_or(i == 0, te_ref[i] != te_ref[jnp.maximum(i - 1, 0)])

    @pl.when(jnp.logical_and(used, new_expert))
    def _():
        wgb_ref[...] = wg_ref[0, 0].astype(BF16)
        wub_ref[...] = wu_ref[0, 0].astype(BF16)
        wdb_ref[...] = wd_ref[0, 0].astype(BF16)

    @pl.when(used)
    def _():
        nblk = 2
        rb = TME // nblk
        xb = [_unpack_pairs(xs_ref[k * rb:(k + 1) * rb, :]).astype(BF16) for k in range(nblk)]
        hg = [_dot(v, wgb_ref[...]) for v in xb]
        hu = [_dot(v, wub_ref[...]) for v in xb]
        hid = [(_silu(g) * u).astype(BF16) for g, u in zip(hg, hu)]
        for k in range(nblk):
            ys_ref[k * rb:(k + 1) * rb, :] = _pack_pairs(_dot(hid[k], wdb_ref[...]))

    @pl.when(jnp.logical_not(used))
    def _():
        ys_ref[...] = jnp.zeros_like(ys_ref)


def _experts(layer, tile_expert, n_used, xs, rows, w_gate, w_up, w_down):
    dp = xs.shape[1]
    d = 2 * dp
    ntiles = rows // TME
    return pl.pallas_call(
        _expert_body,
        out_shape=jax.ShapeDtypeStruct((rows, dp), xs.dtype),
        grid_spec=pltpu.PrefetchScalarGridSpec(
            num_scalar_prefetch=2,
            grid=(ntiles,),
            in_specs=[
                pl.BlockSpec((TME, dp), lambda i, te, nu: (jnp.minimum(i, nu[0] - 1), 0)),
                pl.BlockSpec((1, 1, d, FF), lambda i, te, nu: (layer, te[i], 0, 0)),
                pl.BlockSpec((1, 1, d, FF), lambda i, te, nu: (layer, te[i], 0, 0)),
                pl.BlockSpec((1, 1, FF, d), lambda i, te, nu: (layer, te[i], 0, 0)),
            ],
            out_specs=pl.BlockSpec((TME, dp), lambda i, te, nu: (i, 0)),
            scratch_shapes=[pltpu.VMEM((d, FF), BF16), pltpu.VMEM((d, FF), BF16), pltpu.VMEM((FF, d), BF16)],
        ),
        compiler_params=_cparams(("arbitrary",)),
        name="moe_experts",
    )(tile_expert, n_used, xs, w_gate, w_up, w_down)


def _gather_pairs(dest_ref, next_ref, ys_ref, buf_ref, sem):
    step = pl.program_id(0) * pl.num_programs(1) + pl.program_id(1)
    nsteps = pl.num_programs(0) * pl.num_programs(1)
    cur = lax.rem(step, 2)

    def start_tile(idx_ref, b):
        for t in range(TM):
            for sl in range(2):
                src = idx_ref[0, 0, sl * TM + t]
                pltpu.make_async_copy(ys_ref.at[pl.ds(src, 1)], buf_ref.at[b, sl, pl.ds(t, 1)],
                                      sem.at[b]).start(priority=sl)

    @pl.when(step == 0)
    def _():
        start_tile(dest_ref, 0)

    for b in range(2):
        @pl.when(jnp.logical_and(step + 1 < nsteps, cur == 1 - b))
        def _():
            start_tile(next_ref, b)

    for sl in range(2):
        pltpu.make_async_copy(ys_ref.at[pl.ds(0, TM)], buf_ref.at[cur, sl], sem.at[cur]).wait()
    return buf_ref.at[cur]


def _moe_combine(wc_ref, rows_ref):
    return wc_ref[:, 0:1] * _unpack_pairs(rows_ref[0]) + wc_ref[:, 1:2] * _unpack_pairs(rows_ref[1])


def _gather_specs(nt, ntiles):
    return [
        pl.BlockSpec((1, 1, 2 * TM), lambda b, i: (b * nt + i, 0, 0), memory_space=pltpu.SMEM),
        pl.BlockSpec((1, 1, 2 * TM), lambda b, i: (jnp.minimum(b * nt + i + 1, ntiles - 1), 0, 0),
                     memory_space=pltpu.SMEM),
    ]


GATHER_SCRATCH = [pltpu.VMEM((2, 2, TM, D // 2), U32), pltpu.SemaphoreType.DMA((2,))]


def _routing_plan(meta_i, counts, tile_tokens):
    t = meta_i.shape[1]
    cnt_perm = counts[:, 0].astype(I32)
    e = jnp.arange(N_EXP)
    cnt = cnt_perm[(e % EXP_PER_GROUP) * N_GROUPS + e // EXP_PER_GROUP]
    tiles_e = (cnt + TME - 1) // TME
    cum_tiles = jnp.cumsum(tiles_e)
    offs = (cum_tiles - tiles_e) * TME
    sel = meta_i[0:2][..., None] == e
    dest = jnp.sum(jnp.where(sel, offs, 0), axis=-1) + meta_i[2:4]
    ntiles = (2 * t) // TME + N_EXP
    n_used = cum_tiles[-1]
    te = jnp.sum(cum_tiles[None, :] <= jnp.arange(ntiles, dtype=I32)[:, None], axis=-1).astype(I32)
    last_e = jnp.max(jnp.where(tiles_e > 0, e, 0)).astype(I32)
    te = jnp.minimum(te, last_e)

    def tiles(n):
        return dest.reshape(2, t // n, n).transpose(1, 0, 2).reshape(t // n, 1, 2 * n)

    pad_start = jnp.concatenate([jnp.where(tiles_e > 0, cum_tiles - 1, -1), n_used[None]]).astype(I32)
    return pad_start, tiles(TD), tiles(tile_tokens), te, n_used.reshape(1).astype(I32), ntiles * TME


def _moe(layer, h2, meta_i, counts, w_gate, w_up, w_down):
    pad_start, dest_td, dest_tm, te, n_used, rows = _routing_plan(meta_i, counts, TM)
    xs = _dispatch(pad_start, dest_td, h2, rows)
    ys = _experts(layer, te, n_used, xs, rows, w_gate, w_up, w_down)
    return ys, dest_tm


def _conf_in_body(x_ref, dest_ref, next_ref, wc_ref, ys_ref, pv_ref, w1_ref, b1_ref, x_out_ref, u_ref,
                  buf_ref, sem):
    rows_ref = _gather_pairs(dest_ref, next_ref, ys_ref, buf_ref, sem)
    x2 = x_ref[0] + pv_ref[0, 0:1, :] * _moe_combine(wc_ref, rows_ref)
    x_out_ref[0] = x2
    h = (_rms(x2) * pv_ref[0, 1:2, :] + pv_ref[0, 2:3, :]).astype(BF16)
    u = _dot(h, w1_ref[...]) + b1_ref[0:1, :]
    u_ref[0] = (u[:, :D] * _sigmoid(u[:, D:])).astype(BF16)


def _conf_in(x1, dest_tm, wcol, ys, pvec, w1, b1):
    bsz, seq, d = x1.shape
    nt = seq // TM
    return pl.pallas_call(
        _conf_in_body,
        out_shape=(jax.ShapeDtypeStruct((bsz, seq, d), F32), jax.ShapeDtypeStruct((bsz, seq, d), BF16)),
        grid=(bsz, nt),
        in_specs=[pl.BlockSpec((1, TM, d), lambda b, i: (b, i, 0))] + _gather_specs(nt, bsz * nt) + [
            pl.BlockSpec((TM, 8), lambda b, i: (b * nt + i, 0)),
            pl.BlockSpec(memory_space=pl.ANY),
            pl.BlockSpec((1, 8, d), lambda b, i: (b, 0, 0)),
            pl.BlockSpec((d, 2 * d), lambda b, i: (0, 0)),
            pl.BlockSpec((8, 2 * d), lambda b, i: (0, 0)),
        ],
        out_specs=(pl.BlockSpec((1, TM, d), lambda b, i: (b, i, 0)),
                   pl.BlockSpec((1, TM, d), lambda b, i: (b, i, 0))),
        scratch_shapes=GATHER_SCRATCH,
        compiler_params=_cparams(("arbitrary", "arbitrary")),
        name="combine_conf_in",
    )(x1, dest_tm, dest_tm, wcol, ys, pvec, w1, b1)


def _conf_mix_body(nt, cur_ref, prev_ref, next_ref, dw_ref, cv_ref, x_ref, w2_ref, pv_ref, rwt_ref,
                   rb_ref, upper_ref, x_out_ref, h2_ref, mi_ref, mf_ref, cnt_ref, ext_ref, sh_ref,
                   conv_ref, carry_ref):
    i = pl.program_id(1)
    ext_ref[0:HALO, :] = jnp.where(i == 0, 0.0, prev_ref[0].astype(F32))
    ext_ref[HALO:HALO + TM, :] = cur_ref[0].astype(F32)
    ext_ref[HALO + TM:2 * HALO + TM, :] = jnp.where(i == nt - 1, 0.0, next_ref[0].astype(F32))
    pad = CONF_K // 2
    span = TM + CONF_SPAN
    for cb in range(D // LANES):
        cs = slice(cb * LANES, (cb + 1) * LANES)
        for r in range(1, 8):
            sh_ref[r - 1, :, cs] = ext_ref[r:r + span, cs]
        for rb in range(0, TM, CONF_ROWS):
            acc = None
            for j in range(CONF_K):
                lo = HALO - pad + j
                r, base = lo % 8, lo - lo % 8 + rb
                src = ext_ref[base:base + CONF_ROWS, cs] if r == 0 else sh_ref[r - 1, base:base + CONF_ROWS, cs]
                term = src * dw_ref[j:j + 1, cs]
                acc = term if acc is None else acc + term
            conv_ref[rb:rb + CONF_ROWS, cs] = acc + cv_ref[0:1, cs]
    u = conv_ref[...]
    mu = jnp.mean(u, axis=-1, keepdims=True)
    uc = u - mu
    var = jnp.mean(uc * uc, axis=-1, keepdims=True)
    y = uc * lax.rsqrt(var + EPS) * cv_ref[1:2, :] + cv_ref[2:3, :]
    mix = _dot(_silu(y).astype(BF16), w2_ref[...]) + cv_ref[3:4, :]
    x_new = x_ref[0] + pv_ref[0, 0:1, :] * mix
    _route_tail(x_new, pv_ref, rwt_ref, rb_ref, upper_ref, x_out_ref, h2_ref, mi_ref, mf_ref, cnt_ref,
                carry_ref)


def _conf_mix(u, dw32, cvec, x2, w2, pvec, rwt, rb, upper):
    bsz, seq, d = x2.shape
    nt = seq // TM
    r = TM // HALO
    cst = lambda shape: pl.BlockSpec(shape, lambda b, i: (0,) * len(shape))
    return pl.pallas_call(
        functools.partial(_conf_mix_body, nt),
        out_shape=_route_out_shapes(bsz, seq),
        grid=(bsz, nt),
        in_specs=[
            pl.BlockSpec((1, TM, d), lambda b, i: (b, i, 0)),
            pl.BlockSpec((1, HALO, d), lambda b, i: (b, jnp.maximum(i * r - 1, 0), 0)),
            pl.BlockSpec((1, HALO, d), lambda b, i: (b, jnp.minimum((i + 1) * r, nt * r - 1), 0)),
            cst((32, d)), cst((8, d)),
            pl.BlockSpec((1, TM, d), lambda b, i: (b, i, 0)),
            cst((d, d)),
            pl.BlockSpec((1, 8, d), lambda b, i: (b, 0, 0)),
        ] + _route_in_specs(),
        out_specs=_route_out_specs(nt),
        scratch_shapes=[pltpu.VMEM((TM + 2 * HALO, d), F32), pltpu.VMEM((7, TM + CONF_SPAN, d), F32),
                        pltpu.VMEM((TM, d), F32),
                        pltpu.VMEM((N_EXP, LANES), F32)],
        compiler_params=_cparams(("arbitrary", "arbitrary")),
        name="conf_mix_router",
    )(u, u, u, dw32, cvec, x2, w2, pvec, rwt, rb, upper)


def _final_body(x_ref, dest_ref, next_ref, wc_ref, ys_ref, pv_ref, o_ref, buf_ref, sem):
    rows_ref = _gather_pairs(dest_ref, next_ref, ys_ref, buf_ref, sem)
    x4 = x_ref[0] + pv_ref[0, 0:1, :] * _moe_combine(wc_ref, rows_ref)
    o_ref[0] = _rms(x4) * pv_ref[0, 1:2, :]


def _final(x3, dest_tm, wcol, ys, pvec):
    bsz, seq, d = x3.shape
    nt = seq // TM
    return pl.pallas_call(
        _final_body,
        out_shape=jax.ShapeDtypeStruct((bsz, seq, d), F32),
        grid=(bsz, nt),
        in_specs=[pl.BlockSpec((1, TM, d), lambda b, i: (b, i, 0))] + _gather_specs(nt, bsz * nt) + [
            pl.BlockSpec((TM, 8), lambda b, i: (b * nt + i, 0)),
            pl.BlockSpec(memory_space=pl.ANY),
            pl.BlockSpec((1, 8, d), lambda b, i: (b, 0, 0)),
        ],
        out_specs=pl.BlockSpec((1, TM, d), lambda b, i: (b, i, 0)),
        scratch_shapes=GATHER_SCRATCH,
        compiler_params=_cparams(("arbitrary", "arbitrary")),
        name="combine_final",
    )(x3, dest_tm, dest_tm, wcol, ys, pvec)


def _grid_sincos(rows, cols, dim):
    quarter = dim // 4
    omega = 1.0 / jnp.power(POS_BASE, jnp.arange(quarter, dtype=F32) / quarter)

    def axis_emb(n):
        ang = jnp.arange(n, dtype=F32)[:, None] * omega[None, :]
        return jnp.concatenate([jnp.sin(ang), jnp.cos(ang)], axis=-1)

    er = jnp.broadcast_to(axis_emb(rows)[:, None, :], (rows, cols, dim // 2))
    ec = jnp.broadcast_to(axis_emb(cols)[None, :, :], (rows, cols, dim // 2))
    return jnp.concatenate([er, ec], axis=-1).reshape(rows * cols, dim)


def _pad_rows(a, rows):
    return jnp.pad(a, ((0, rows - a.shape[0]),) + ((0, 0),) * (a.ndim - 1))


def _per_batch(*vecs):
    st = jnp.stack(vecs, axis=1)
    return jnp.pad(st, ((0, 0), (0, 8 - st.shape[1]), (0, 0)))


def _width_dft():
    k = np.arange(LANES)
    ang = 2 * np.pi * np.outer(k, k) / LANES
    groups = FN_W // LANES
    c = np.kron(np.eye(groups), np.cos(ang)) / math.sqrt(LANES)
    s = np.kron(np.eye(groups), np.sin(ang)) / math.sqrt(LANES)
    return jnp.asarray(np.concatenate([c, s], axis=1), dtype=BF16)


def kernel(x, c, ctx, c_ctx, ada_w, ada_b, norm1_g, norm2_g, hyb_w_in, dn_conv_w, dn_a_log, dn_dt_bias,
           dn_onorm_g, hyb_w_out, conf_w1, conf_b1, conf_dw_w, conf_dw_b, conf_ln_g, conf_ln_b, conf_w2,
           conf_b2, router_w, router_bias, moe_w_gate, moe_w_up, moe_w_down, final_g):
    bsz, seq, d = x.shape
    ctx_len = ctx.shape[1]
    assert d == D and seq % TM == 0 and ctx_len % TM == 0 and (bsz * seq) % TD == 0
    nct = ctx_len // TM
    t = bsz * seq

    pos = _grid_sincos(seq // GRID_W, GRID_W, d)

    cpad = _pad_rows(jnp.concatenate([c, c_ctx[None, :]], axis=0), 16)
    mods = _adaln(cpad, ada_w, ada_b)
    m = lambda l, k: mods[l, :, k * d:(k + 1) * d]
    a1 = norm1_g[0][None, :] * (1.0 + m(0, 1))
    mod_a = jnp.stack([jnp.broadcast_to(a1[bsz], (bsz, d)), a1[:bsz]], axis=1)
    mod_b = jnp.stack([jnp.broadcast_to(m(0, 0)[bsz], (bsz, d)), m(0, 0)[:bsz]], axis=1)

    w_in = hyb_w_in[0]
    z_lo = QKV_W + AB_W
    f_lo = z_lo + Z_W
    wqkv = w_in[:, :QKV_W].astype(BF16)
    wab = jnp.pad(w_in[:, QKV_W:z_lo], ((0, 0), (0, LANES - AB_W))).astype(BF16)
    wz = w_in[:, z_lo:f_lo].astype(BF16)
    wf = w_in[:, f_lo:].astype(BF16)
    gpar = jnp.zeros((8, LANES), F32)
    gpar = gpar.at[0, :2 * HEADS].set(-jnp.exp(dn_a_log[0].reshape(-1)))
    gpar = gpar.at[1, :2 * HEADS].set(dn_dt_bias[0].reshape(-1))
    qkv_pre, z, fc, fs, gates = _inproj(ctx, x, pos, mod_a, mod_b, wqkv, wz, wf, wab, _width_dft(), gpar)

    qkv = _dnconv(qkv_pre, _pad_rows(dn_conv_w[0], 8), nct)
    lc = qkv.shape[1]
    nchunks = lc // CHUNK
    grow = gates[:, :, :AB_W].reshape(bsz, nchunks, CHUNK, 4, HEADS)
    grow = grow.transpose(0, 1, 3, 4, 2).reshape(bsz, nchunks, 4, HB)
    grow = jnp.pad(grow, ((0, 0), (0, 0), (0, 4), (0, 0)))
    o_f, o_b = _deltanet(qkv, grow, seq, ctx_len // CHUNK)

    fn = _seq_dft(fc, fs)

    perm = (np.arange(N_EXP) % N_GROUPS) * EXP_PER_GROUP + np.arange(N_EXP) // N_GROUPS
    rwt = router_w.T[perm].astype(BF16)
    rb = jnp.broadcast_to(router_bias[perm][:, None], (N_EXP, LANES)).astype(F32)
    upper = jnp.asarray(np.triu(np.ones((TM, TM), np.float32), 1), dtype=BF16)
    w_out = hyb_w_out[0].astype(BF16)
    onorm8 = _pad_rows(jnp.tile(dn_onorm_g[0], HEADS)[None, :], 8)
    pv0 = _per_batch(m(0, 2)[:bsz], norm2_g[0][None, :] * (1.0 + m(0, 4)[:bsz]), m(0, 3)[:bsz])
    x1, h2, meta_i, meta_f, counts = _mix0(o_f, o_b, z, fn, x, pos, w_out[:V_W], w_out[V_W:], onorm8, pv0,
                                           rwt, rb, upper)
    ys, dest_tm = _moe(0, h2, meta_i, counts, moe_w_gate, moe_w_up, moe_w_down)
    wcol = meta_f.T

    pv1 = _per_batch(m(0, 5)[:bsz], norm1_g[1][None, :] * (1.0 + m(1, 1)[:bsz]), m(1, 0)[:bsz])
    x2, u = _conf_in(x1, dest_tm, wcol, ys, pv1, conf_w1[0].astype(BF16), _pad_rows(conf_b1[0][None, :], 8))
    cvec = _pad_rows(jnp.stack([conf_dw_b[0], conf_ln_g[0], conf_ln_b[0], conf_b2[0]]), 8)
    pv2 = _per_batch(m(1, 2)[:bsz], norm2_g[1][None, :] * (1.0 + m(1, 4)[:bsz]), m(1, 3)[:bsz])
    x3, h2, meta_i, meta_f, counts = _conf_mix(u, _pad_rows(conf_dw_w[0], 32), cvec, x2,
                                               conf_w2[0].astype(BF16), pv2, rwt, rb, upper)
    ys, dest_tm = _moe(1, h2, meta_i, counts, moe_w_gate, moe_w_up, moe_w_down)
    pv3 = _per_batch(m(1, 5)[:bsz], jnp.broadcast_to(final_g[None, :], (bsz, d)))
    return _final(x3, dest_tm, meta_f.T, ys, pv3)
```

```python
import functools
import math

import numpy as np
import jax
import jax.numpy as jnp
from jax import lax
from jax.experimental import pallas as pl
from jax.experimental.pallas import tpu as pltpu

F32 = jnp.float32
BF16 = jnp.bfloat16
I32 = jnp.int32
U32 = jnp.uint32

D = 1024
GRID_W = 64
HEADS = 4
DK = 128
DV = 128
CHUNK = 64
QK_W = HEADS * DK
V_W = HEADS * DV
QKV_W = 2 * QK_W + V_W
AB_W = 16
Z_W = V_W
FN_W = 512
DN_CONV_K = 5
CONF_K = 31
N_EXP = 32
N_GROUPS = 8
EXP_PER_GROUP = 4
FF = 512
EPS = 1e-6
POS_BASE = 10000.0

LANES = 128
HALO = 16
TM = 256
TMR = 512
TME = 512
HB = HEADS * CHUNK
DN_STACK = 2
HS = DN_STACK * CHUNK
DN_GROUP = 8
CONF_SPAN = 24
CONF_ROWS = 64
VMEM_LIMIT = 56 * 1024 * 1024


def _cparams(sem):
    return pltpu.CompilerParams(dimension_semantics=sem, vmem_limit_bytes=VMEM_LIMIT)


def _sigmoid(v):
    return jax.nn.sigmoid(v)


def _silu(v):
    return v * jax.nn.sigmoid(v)


def _dot(a, b):
    return jnp.dot(a, b, preferred_element_type=F32)


def _dot_nt(a, b):
    return lax.dot_general(a, b, (((1,), (1,)), ((), ())), preferred_element_type=F32)


def _split3(a):
    a1 = a.astype(BF16)
    r1 = a - a1.astype(F32)
    a2 = r1.astype(BF16)
    a3 = (r1 - a2.astype(F32)).astype(BF16)
    return a1, a2, a3


def _pack_pairs(v):
    n = v.shape[1] // 2
    lo = lax.bitcast_convert_type(v[:, :n].astype(BF16).astype(F32), U32) >> 16
    hi = lax.bitcast_convert_type(v[:, n:].astype(BF16).astype(F32), U32) & jnp.uint32(0xFFFF0000)
    return hi | lo


def _unpack_pairs(p):
    lo = lax.bitcast_convert_type(p << 16, F32)
    hi = lax.bitcast_convert_type(p & jnp.uint32(0xFFFF0000), F32)
    return jnp.concatenate([lo, hi], axis=1)


def _rms(v, eps=EPS):
    return v * lax.rsqrt(jnp.mean(v * v, axis=-1, keepdims=True) + eps)


def _adaln_body(c_ref, w_ref, b_ref, o_ref):
    s = _silu(c_ref[...])
    o_ref[0] = jnp.dot(s, w_ref[0], preferred_element_type=F32,
                       precision=lax.Precision.HIGHEST) + b_ref[0]


def _adaln(cpad, ada_w, ada_b):
    depth, d, n = ada_w.shape
    tn = 1536
    rows = cpad.shape[0]
    return pl.pallas_call(
        _adaln_body,
        out_shape=jax.ShapeDtypeStruct((depth, rows, n), F32),
        grid=(depth, n // tn),
        in_specs=[
            pl.BlockSpec((rows, d), lambda l, j: (0, 0)),
            pl.BlockSpec((1, d, tn), lambda l, j: (l, 0, j)),
            pl.BlockSpec((1, 1, tn), lambda l, j: (l, 0, j)),
        ],
        out_specs=pl.BlockSpec((1, rows, tn), lambda l, j: (l, 0, j)),
        compiler_params=_cparams(("arbitrary", "arbitrary")),
        name="adaln",
    )(cpad, ada_w, ada_b.reshape(depth, 1, n))


def _inproj_body(nct, ctx_ref, x_ref, pos_ref, ma_ref, mb_ref, wqkv_ref, wz_ref, wf_ref, wab_ref,
                 dftw_ref, gpar_ref, qkv_ref, z_ref, fc_ref, fs_ref, gate_ref):
    i = pl.program_id(1)
    is_ctx = i < nct
    xin = jnp.where(is_ctx, ctx_ref[0], x_ref[0] + pos_ref[...])
    a = jnp.where(is_ctx, ma_ref[0, 0:1, :], ma_ref[0, 1:2, :])
    b = jnp.where(is_ctx, mb_ref[0, 0:1, :], mb_ref[0, 1:2, :])
    h = (_rms(xin) * a + b).astype(BF16)
    qkv_ref[0] = _dot(h, wqkv_ref[...]).astype(BF16)
    z_ref[0] = _dot(h, wz_ref[...]).astype(BF16)
    f = _dot(h, wf_ref[...]).astype(BF16)
    fcs = _dot(f, dftw_ref[...])
    fc_ref[0] = fcs[:, :FN_W].astype(BF16)
    fs_ref[0] = fcs[:, FN_W:].astype(BF16)
    ab = _dot(h, wab_ref[...])
    lane = lax.broadcasted_iota(I32, ab.shape, 1)
    pre = ab + gpar_ref[1:2, :]
    softplus = jnp.maximum(pre, 0.0) + jnp.log1p(jnp.exp(-jnp.abs(pre)))
    gate_ref[0] = jnp.where(lane < 2 * HEADS, gpar_ref[0:1, :] * softplus, _sigmoid(ab))


def _inproj(ctx, x, pos, mod_a, mod_b, wqkv, wz, wf, wab, dftw, gpar):
    bsz, seq, d = x.shape
    nct = ctx.shape[1] // TM
    nt = nct + seq // TM
    lc = nt * TM
    xi = lambda b, i: (b, jnp.maximum(i - nct, 0), 0)
    full = lambda shape: pl.BlockSpec(shape, lambda b, i: (0,) * len(shape))
    return pl.pallas_call(
        functools.partial(_inproj_body, nct),
        out_shape=(
            jax.ShapeDtypeStruct((bsz, lc, QKV_W), BF16),
            jax.ShapeDtypeStruct((bsz, seq, Z_W), BF16),
            jax.ShapeDtypeStruct((bsz, seq, FN_W), BF16),
            jax.ShapeDtypeStruct((bsz, seq, FN_W), BF16),
            jax.ShapeDtypeStruct((bsz, lc, LANES), F32),
        ),
        grid=(bsz, nt),
        in_specs=[
            pl.BlockSpec((1, TM, d), lambda b, i: (b, jnp.minimum(i, nct - 1), 0)),
            pl.BlockSpec((1, TM, d), xi),
            pl.BlockSpec((TM, d), lambda b, i: (jnp.maximum(i - nct, 0), 0)),
            pl.BlockSpec((1, 2, d), lambda b, i: (b, 0, 0)),
            pl.BlockSpec((1, 2, d), lambda b, i: (b, 0, 0)),
            full((d, QKV_W)), full((d, Z_W)), full((d, FN_W)), full((d, LANES)),
            full((FN_W, 2 * FN_W)), full((8, LANES)),
        ],
        out_specs=(
            pl.BlockSpec((1, TM, QKV_W), lambda b, i: (b, i, 0)),
            pl.BlockSpec((1, TM, Z_W), xi),
            pl.BlockSpec((1, TM, FN_W), xi),
            pl.BlockSpec((1, TM, FN_W), xi),
            pl.BlockSpec((1, TM, LANES), lambda b, i: (b, i, 0)),
        ),
        compiler_params=_cparams(("arbitrary", "arbitrary")),
        name="inproj",
    )(ctx, x, pos, mod_a, mod_b, wqkv, wz, wf, wab, dftw, gpar)


def _dnconv_body(nct, nt, cur_ref, prev_ref, next_ref, w_ref, o_ref, ext_ref):
    i = pl.program_id(1)
    first = jnp.logical_or(i == 0, i == nct)
    last = jnp.logical_or(i == nct - 1, i == nt - 1)
    ext_ref[0:HALO, :] = jnp.where(first, 0.0, prev_ref[0].astype(F32))
    ext_ref[HALO:HALO + TM, :] = cur_ref[0].astype(F32)
    ext_ref[HALO + TM:2 * HALO + TM, :] = jnp.where(last, 0.0, next_ref[0].astype(F32))
    pad = DN_CONV_K // 2
    for cb in range(QKV_W // LANES):
        cs = slice(cb * LANES, (cb + 1) * LANES)
        acc = None
        for j in range(DN_CONV_K):
            lo = HALO - pad + j
            term = ext_ref[lo:lo + TM, cs] * w_ref[j:j + 1, cs]
            acc = term if acc is None else acc + term
        y = _silu(acc)
        if cb < 2 * HEADS:
            y = y * lax.rsqrt(jnp.sum(y * y, axis=-1, keepdims=True) + EPS)
        o_ref[0, :, cs] = y.astype(BF16)


def _dnconv(qkv_pre, conv_w8, nct):
    bsz, lc, w = qkv_pre.shape
    nt = lc // TM
    r = TM // HALO
    return pl.pallas_call(
        functools.partial(_dnconv_body, nct, nt),
        out_shape=jax.ShapeDtypeStruct((bsz, lc, w), BF16),
        grid=(bsz, nt),
        in_specs=[
            pl.BlockSpec((1, TM, w), lambda b, i: (b, i, 0)),
            pl.BlockSpec((1, HALO, w), lambda b, i: (b, jnp.maximum(i * r - 1, 0), 0)),
            pl.BlockSpec((1, HALO, w), lambda b, i: (b, jnp.minimum((i + 1) * r, nt * r - 1), 0)),
            pl.BlockSpec((8, w), lambda b, i: (0, 0)),
        ],
        out_specs=pl.BlockSpec((1, TM, w), lambda b, i: (b, i, 0)),
        scratch_shapes=[pltpu.VMEM((TM + 2 * HALO, w), F32)],
        compiler_params=_cparams(("arbitrary", "arbitrary")),
        name="dnconv",
    )(qkv_pre, qkv_pre, qkv_pre, conv_w8)


def _dn_setup(d, st, qkv, grow, mask_incl, mask_strict, trit, bdones, eye, lvl0):
    neg_inf = float("-inf")
    heads = range(st * DN_STACK, (st + 1) * DN_STACK)
    stack = lambda off: jnp.concatenate([qkv[:, off + h * DK: off + (h + 1) * DK] for h in heads], axis=0)
    q_s = stack(0)
    k_s = stack(QK_W)
    v_s = stack(2 * QK_W)
    grow = grow[:, st * HS:(st + 1) * HS]
    g3 = _split3(grow)
    sums = sum(_dot(t, jnp.concatenate([trit, bdones], axis=1)) for t in g3)
    cg_row = sums[:, :HS]
    tot_row = sums[:, HS:]
    rows = jnp.concatenate([cg_row, tot_row, grow, jnp.zeros((LANES - 24, HS), F32)], axis=0)
    cols = rows.T
    cgc = cols[:, d:d + 1]
    totc = cols[:, 8 + d:9 + d]
    betac = cols[:, 18 + d:19 + d]
    cgr = cg_row[d:d + 1, :]
    dmat = jnp.exp(jnp.where(mask_incl > 0.5, cgc - cgr, neg_inf))
    kk = _dot_nt(k_s, k_s)
    qk = _dot_nt(q_s, k_s)
    a = jnp.where(mask_strict > 0.5, kk * dmat, 0.0) * betac
    ecg = jnp.exp(cgc)
    k32 = k_s.astype(F32)
    scale = DK ** -0.5
    return dict(
        d=d, st=st, a=a, x=eye.astype(F32) - a * lvl0, totc=totc,
        rhs=jnp.concatenate([(betac * v_s.astype(F32)).astype(BF16),
                             ((betac * ecg) * k32).astype(BF16)], axis=1),
        ktail=(k32 * jnp.exp(totc - cgc)).astype(BF16),
        qdec=(q_s.astype(F32) * (ecg * scale)).astype(BF16),
        qkd=(qk * dmat * scale).astype(BF16))


def _dn_body(qf_ref, qb_ref, gf_ref, gb_ref, mli_ref, mls_ref, mui_ref, mus_ref, bdl_ref, bdu_ref,
             bdo_ref, eye_ref, lvl_ref, of_ref, ob_ref, s_ref):
    @pl.when(pl.program_id(1) == 0)
    def _():
        s_ref[...] = jnp.zeros_like(s_ref)

    bdl = bdl_ref[...]
    bdu = bdu_ref[...]
    bdo = bdo_ref[...]
    eye = eye_ref[...]
    chains = []
    for g in range(qf_ref.shape[0]):
        for st in range(HEADS // DN_STACK):
            c = _dn_setup(0, st, qf_ref[g], gf_ref[g, 0], mli_ref[...], mls_ref[...], bdu, bdo, eye,
                          lvl_ref[0])
            c.update(g=g, out=of_ref)
            chains.append(c)
            c = _dn_setup(1, st, qb_ref[g], gb_ref[g, 0], mui_ref[...], mus_ref[...], bdl, bdo, eye,
                          lvl_ref[0])
            c.update(g=g, out=ob_ref)
            chains.append(c)
    for lvl in range(1, int(math.log2(CHUNK))):
        for c in chains:
            c["xb"] = c["x"].astype(BF16)
            c["y"] = _dot((c["a"] * lvl_ref[lvl]).astype(BF16), c["xb"]).astype(BF16)
        for c in chains:
            c["x"] = c["x"] - _dot(c["xb"], c["y"])
    for c in chains:
        uw = _dot(c["x"].astype(BF16), c["rhs"])
        c["u"] = uw[:, :DV]
        c["w"] = uw[:, DV:].astype(BF16)
    for c in chains:
        c["vn"] = []
        c["oi"] = []
    for hh in range(DN_STACK):
        rs = slice(hh * CHUNK, (hh + 1) * CHUNK)
        for c in chains:
            s = s_ref[c["g"], c["d"], c["st"] * DN_STACK + hh]
            sb = s.astype(BF16)
            c["s"] = s
            c["vn"].append(c["u"][rs] - _dot(c["w"][rs], sb))
            c["oi"].append(_dot(c["qdec"][rs], sb))
        for c in chains:
            cd = jnp.exp(c["totc"][hh * CHUNK:hh * CHUNK + 1, :])
            upd = lax.dot_general(c["ktail"][rs], c["vn"][hh].astype(BF16), (((0,), (0,)), ((), ())),
                                  preferred_element_type=F32)
            s_ref[c["g"], c["d"], c["st"] * DN_STACK + hh] = c["s"] * cd + upd
    for c in chains:
        vn_s = jnp.concatenate(c["vn"], axis=0).astype(BF16)
        o_s = jnp.concatenate(c["oi"], axis=0) + _dot(c["qkd"], vn_s)
        for hh in range(DN_STACK):
            h = c["st"] * DN_STACK + hh
            c["out"][c["g"], :, h * DV:(h + 1) * DV] = o_s[hh * CHUNK:(hh + 1) * CHUNK].astype(BF16)


def _dn_consts():
    r = np.arange(HS)
    same = (r[:, None] // CHUNK) == (r[None, :] // CHUNK)
    li = same & (r[:, None] >= r[None, :])
    ls = same & (r[:, None] > r[None, :])
    ui = same & (r[:, None] <= r[None, :])
    us = same & (r[:, None] < r[None, :])
    f = lambda m: jnp.asarray(m.astype(np.float32))
    h = lambda m: jnp.asarray(m.astype(np.float32), dtype=BF16)
    nlev = int(math.log2(CHUNK))
    lv = [((r[:, None] >> (l + 1)) == (r[None, :] >> (l + 1))) & ((r[:, None] >> l) != (r[None, :] >> l))
          for l in range(nlev)]
    return (f(li), f(ls), f(ui), f(us), h(li), h(ui), h(same), h(np.eye(HS, dtype=bool)),
            jnp.asarray(np.stack(lv).astype(np.float32)))


def _deltanet(qkv, grow, seq, ncc):
    bsz, lc, w = qkv.shape
    nchunks = lc // CHUNK
    nx = seq // CHUNK
    fwd = lambda s: s
    bwd = lambda s: jnp.where(s < ncc, ncc - 1 - s, ncc + nchunks - 1 - s)
    ofi = lambda b, s: (b, jnp.maximum(s - ncc, 0), 0)
    obi = lambda b, s: (b, jnp.minimum(nchunks - 1 - s, nx - 1), 0)
    cst = lambda shape: pl.BlockSpec(shape, lambda b, s: (0, 0))
    consts = _dn_consts()
    grp = math.gcd(bsz, DN_GROUP)
    return pl.pallas_call(
        _dn_body,
        out_shape=(jax.ShapeDtypeStruct((bsz, seq, V_W), BF16),
                   jax.ShapeDtypeStruct((bsz, seq, V_W), BF16)),
        grid=(bsz // grp, nchunks),
        in_specs=[
            pl.BlockSpec((grp, CHUNK, w), lambda b, s: (b, fwd(s), 0)),
            pl.BlockSpec((grp, CHUNK, w), lambda b, s: (b, bwd(s), 0)),
            pl.BlockSpec((grp, 1, 8, HB), lambda b, s: (b, fwd(s), 0, 0)),
            pl.BlockSpec((grp, 1, 8, HB), lambda b, s: (b, bwd(s), 0, 0)),
        ] + [cst((HS, HS))] * 8 + [pl.BlockSpec((int(math.log2(CHUNK)), HS, HS), lambda b, s: (0, 0, 0))],
        out_specs=(pl.BlockSpec((grp, CHUNK, V_W), ofi), pl.BlockSpec((grp, CHUNK, V_W), obi)),
        scratch_shapes=[pltpu.VMEM((grp, 2, HEADS, DK, DV), F32)],
        compiler_params=_cparams(("arbitrary", "arbitrary")),
        name="deltanet",
    )(qkv, qkv, grow, grow, *consts)


def _fft1_body(ns2, fc_ref, fs_ref, m_ref, cos_ref, sin_ref, y_ref):
    n1 = fc_ref.shape[1]
    xin = jnp.concatenate([fc_ref[0], fs_ref[0]], axis=0)
    y = _dot(m_ref[...], xin)
    yr, yi = y[:n1], y[n1:]
    for q in range(ns2):
        cs = slice(q * FN_W, (q + 1) * FN_W)
        c = jnp.concatenate([cos_ref[q]] * (FN_W // LANES), axis=1)
        s = jnp.concatenate([sin_ref[q]] * (FN_W // LANES), axis=1)
        y_ref[0, 0, :, cs] = (yr[:, cs] * c + yi[:, cs] * s).astype(BF16)
        y_ref[0, 1, :, cs] = (yi[:, cs] * c - yr[:, cs] * s).astype(BF16)


def _fft2_body(ns1, y_ref, cs_ref, o_ref):
    for q in range(ns1):
        rhs = jnp.concatenate([y_ref[0, 0, q], y_ref[0, 1, q]], axis=0)
        o_ref[0, :, q * FN_W:(q + 1) * FN_W] = _dot(cs_ref[...], rhs).astype(BF16)


def _seq_dft(fc, fs):
    bsz, seq, w = fc.shape
    n2 = GRID_W
    n1 = seq // n2
    k1 = np.arange(n1)
    ang1 = 2 * np.pi * np.outer(k1, k1) / n1
    c1, s1 = np.cos(ang1), np.sin(ang1)
    m1 = jnp.asarray(np.block([[c1, -s1], [-s1, -c1]]), dtype=BF16)
    tw = 2 * np.pi * np.outer(np.arange(n2), k1) / seq
    cos_t = jnp.asarray(np.repeat(np.cos(tw)[:, :, None], LANES, axis=2), dtype=F32)
    sin_t = jnp.asarray(np.repeat(np.sin(tw)[:, :, None], LANES, axis=2), dtype=F32)
    k2 = np.arange(n2)
    ang2 = 2 * np.pi * np.outer(k2, k2) / n2
    cs2 = jnp.asarray(np.concatenate([np.cos(ang2), np.sin(ang2)], axis=1) / math.sqrt(seq), dtype=BF16)

    ns2 = 4
    y = pl.pallas_call(
        functools.partial(_fft1_body, ns2),
        out_shape=jax.ShapeDtypeStruct((bsz, 2, n1, n2 * w), BF16),
        grid=(bsz, n2 // ns2),
        in_specs=[
            pl.BlockSpec((1, n1, ns2 * w), lambda b, j: (b, 0, j)),
            pl.BlockSpec((1, n1, ns2 * w), lambda b, j: (b, 0, j)),
            pl.BlockSpec((2 * n1, 2 * n1), lambda b, j: (0, 0)),
            pl.BlockSpec((ns2, n1, LANES), lambda b, j: (j, 0, 0)),
            pl.BlockSpec((ns2, n1, LANES), lambda b, j: (j, 0, 0)),
        ],
        out_specs=pl.BlockSpec((1, 2, n1, ns2 * w), lambda b, j: (b, 0, 0, j)),
        compiler_params=_cparams(("arbitrary", "arbitrary")),
        name="seqdft_stage1",
    )(fc.reshape(bsz, n1, n2 * w), fs.reshape(bsz, n1, n2 * w), m1, cos_t, sin_t)

    ns1 = min(8, n1)
    out = pl.pallas_call(
        functools.partial(_fft2_body, ns1),
        out_shape=jax.ShapeDtypeStruct((bsz, n2, n1 * w), BF16),
        grid=(bsz, n1 // ns1),
        in_specs=[
            pl.BlockSpec((1, 2, ns1, n2, w), lambda b, j: (b, 0, j, 0, 0)),
            pl.BlockSpec((n2, 2 * n2), lambda b, j: (0, 0)),
        ],
        out_specs=pl.BlockSpec((1, n2, ns1 * w), lambda b, j: (b, 0, j)),
        compiler_params=_cparams(("arbitrary", "arbitrary")),
        name="seqdft_stage2",
    )(y.reshape(bsz, 2, n1, n2, w), cs2)
    return out.reshape(bsz, seq, w)


def _route_tail(x_new, pv_ref, rwt_ref, rb_ref, upper_ref, x_out_ref, h2_ref, mi_ref, mf_ref, cnt_ref,
                carry_ref):
    first = jnp.logical_and(pl.program_id(0) == 0, pl.program_id(1) == 0)

    @pl.when(first)
    def _():
        carry_ref[...] = jnp.zeros_like(carry_ref)

    x_out_ref[0] = x_new
    h2 = _rms(x_new) * pv_ref[0, 1:2, :] + pv_ref[0, 2:3, :]
    h2_ref[...] = _pack_pairs(h2)
    logits = _dot_nt(rwt_ref[...], h2.astype(BF16))
    scores = _sigmoid(logits)
    sel = scores + rb_ref[:, 0:1]
    a = [sel[N_GROUPS * j:N_GROUPS * (j + 1)] for j in range(EXP_PER_GROUP)]
    sc = [scores[N_GROUPS * j:N_GROUPS * (j + 1)] for j in range(EXP_PER_GROUP)]
    gs = None
    for j0 in range(EXP_PER_GROUP):
        for j1 in range(j0 + 1, EXP_PER_GROUP):
            pair = a[j0] + a[j1]
            gs = pair if gs is None else jnp.maximum(gs, pair)
    gidx = lax.broadcasted_iota(I32, gs.shape, 0).astype(F32)
    gmax = jnp.max(gs, axis=0, keepdims=True)
    best = jnp.min(jnp.where(gs == gmax, gidx, float(N_GROUPS)), axis=0, keepdims=True)
    inb = gidx == best
    v = [jnp.sum(jnp.where(inb, a[j], 0.0), axis=0, keepdims=True) for j in range(EXP_PER_GROUP)]
    s = [jnp.sum(jnp.where(inb, sc[j], 0.0), axis=0, keepdims=True) for j in range(EXP_PER_GROUP)]
    v1, i1, s1 = v[0], jnp.zeros_like(v[0]), s[0]
    for j in range(1, EXP_PER_GROUP):
        better = v[j] > v1
        v1 = jnp.where(better, v[j], v1)
        i1 = jnp.where(better, float(j), i1)
        s1 = jnp.where(better, s[j], s1)
    v2 = jnp.full_like(v1, float("-inf"))
    i2 = jnp.zeros_like(v1)
    s2 = jnp.zeros_like(v1)
    for j in range(EXP_PER_GROUP):
        better = jnp.logical_and(i1 != float(j), v[j] > v2)
        v2 = jnp.where(better, v[j], v2)
        i2 = jnp.where(better, float(j), i2)
        s2 = jnp.where(better, s[j], s2)
    denom = s1 + s2
    rid = lax.broadcasted_iota(I32, logits.shape, 0).astype(F32)
    oh1 = rid == (i1 * N_GROUPS + best)
    oh2 = rid == (i2 * N_GROUPS + best)
    oh = jnp.where(jnp.logical_or(oh1, oh2), 1.0, 0.0)
    prefix = _dot(oh.astype(BF16), upper_ref[...])
    base = carry_ref[:, 0:1] + prefix
    rank1 = jnp.sum(jnp.where(oh1, base, 0.0), axis=0, keepdims=True)
    rank2 = jnp.sum(jnp.where(oh2, base, 0.0), axis=0, keepdims=True)
    carry_new = carry_ref[...] + jnp.sum(oh, axis=1, keepdims=True)
    carry_ref[...] = carry_new
    cnt_ref[...] = carry_new
    mi_ref[...] = jnp.zeros_like(mi_ref)
    mi_ref[0:1, :] = (best * EXP_PER_GROUP + i1).astype(I32)
    mi_ref[1:2, :] = (best * EXP_PER_GROUP + i2).astype(I32)
    mi_ref[2:3, :] = rank1.astype(I32)
    mi_ref[3:4, :] = rank2.astype(I32)
    mf_ref[...] = jnp.zeros_like(mf_ref)
    mf_ref[0:1, :] = s1 / denom
    mf_ref[1:2, :] = s2 / denom


def _route_out_shapes(bsz, seq):
    t = bsz * seq
    return (
        jax.ShapeDtypeStruct((bsz, seq, D), F32),
        jax.ShapeDtypeStruct((t, D // 2), U32),
        jax.ShapeDtypeStruct((8, t), I32),
        jax.ShapeDtypeStruct((8, t), F32),
        jax.ShapeDtypeStruct((N_EXP, LANES), F32),
    )


def _route_out_specs(nt):
    flat = lambda b, i: (b * nt + i, 0)
    return (
        pl.BlockSpec((1, TMR, D), lambda b, i: (b, i, 0)),
        pl.BlockSpec((TMR, D // 2), flat),
        pl.BlockSpec((8, TMR), lambda b, i: (0, b * nt + i)),
        pl.BlockSpec((8, TMR), lambda b, i: (0, b * nt + i)),
        pl.BlockSpec((N_EXP, LANES), lambda b, i: (0, 0)),
    )


def _route_in_specs():
    return [
        pl.BlockSpec((N_EXP, D), lambda b, i: (0, 0)),
        pl.BlockSpec((N_EXP, LANES), lambda b, i: (0, 0)),
        pl.BlockSpec((TMR, TMR), lambda b, i: (0, 0)),
    ]


def _mix0_body(of_ref, ob_ref, z_ref, fn_ref, x_ref, pos_ref, wdn_ref, wfn_ref, on_ref, pv_ref,
               rwt_ref, rb_ref, upper_ref, x_out_ref, h2_ref, mi_ref, mf_ref, cnt_ref, carry_ref):
    o = of_ref[0].astype(F32) + ob_ref[0].astype(F32)
    z = z_ref[0].astype(F32)
    parts = []
    for h in range(HEADS):
        cs = slice(h * DV, (h + 1) * DV)
        parts.append(_rms(o[:, cs]) * on_ref[0:1, cs] * _silu(z[:, cs]))
    dn = jnp.concatenate(parts, axis=1).astype(BF16)
    mix = _dot(dn, wdn_ref[...]) + _dot(fn_ref[0], wfn_ref[...])
    x_new = x_ref[0] + pos_ref[...] + pv_ref[0, 0:1, :] * mix
    _route_tail(x_new, pv_ref, rwt_ref, rb_ref, upper_ref, x_out_ref, h2_ref, mi_ref, mf_ref, cnt_ref,
                carry_ref)


def _mix0(o_f, o_b, z, fn, x, pos, wdn, wfn, onorm8, pvec, rwt, rb, upper):
    bsz, seq, d = x.shape
    nt = seq // TMR
    tok = lambda w: pl.BlockSpec((1, TMR, w), lambda b, i: (b, i, 0))
    cst = lambda shape: pl.BlockSpec(shape, lambda b, i: (0,) * len(shape))
    return pl.pallas_call(
        _mix0_body,
        out_shape=_route_out_shapes(bsz, seq),
        grid=(bsz, nt),
        in_specs=[tok(V_W), tok(V_W), tok(Z_W), tok(FN_W), tok(d),
                  pl.BlockSpec((TMR, d), lambda b, i: (i, 0)),
                  cst((V_W, d)), cst((FN_W, d)), cst((8, V_W)),
                  pl.BlockSpec((1, 8, d), lambda b, i: (b, 0, 0))] + _route_in_specs(),
        out_specs=_route_out_specs(nt),
        scratch_shapes=[pltpu.VMEM((N_EXP, LANES), F32)],
        compiler_params=_cparams(("arbitrary", "arbitrary")),
        name="mix0_router",
    )(o_f, o_b, z, fn, x, pos, wdn, wfn, onorm8, pvec, rwt, rb, upper)


TD = 512


def _dispatch_body(pad_ref, dest_ref, h2_ref, xs_ref, zero_ref, sem):
    ntiles = xs_ref.shape[0] // TME

    @pl.when(pl.program_id(0) == 0)
    def _():
        zero_ref[...] = jnp.zeros_like(zero_ref)

        def zero_tile(k):
            return pltpu.make_async_copy(zero_ref, xs_ref.at[pl.ds(pl.multiple_of(k * TME, TME), TME)], sem)

        def fill_tail(k, carry):
            zero_tile(k).start()
            return carry

        def drain_tail(k, carry):
            zero_tile(k).wait()
            return carry

        for e in range(N_EXP):
            @pl.when(pad_ref[e] >= 0)
            def _():
                zero_tile(pad_ref[e]).start()
        lax.fori_loop(pad_ref[N_EXP], ntiles, fill_tail, 0)
        for e in range(N_EXP):
            @pl.when(pad_ref[e] >= 0)
            def _():
                zero_tile(pad_ref[e]).wait()
        lax.fori_loop(pad_ref[N_EXP], ntiles, drain_tail, 0)

    for t in range(TD):
        for sl in range(2):
            dst = dest_ref[0, 0, sl * TD + t]
            pltpu.make_async_copy(h2_ref.at[pl.ds(t, 1)], xs_ref.at[pl.ds(dst, 1)], sem).start(priority=sl)
    for sl in range(2):
        pltpu.make_async_copy(h2_ref, xs_ref.at[pl.ds(0, TD)], sem).wait()


def _dispatch(pad_start, dest_tiles, h2, rows):
    t, d = h2.shape
    return pl.pallas_call(
        _dispatch_body,
        out_shape=jax.ShapeDtypeStruct((rows, d), h2.dtype),
        grid_spec=pltpu.PrefetchScalarGridSpec(
            num_scalar_prefetch=1,
            grid=(t // TD,),
            in_specs=[
                pl.BlockSpec((1, 1, 2 * TD), lambda i, pad: (i, 0, 0), memory_space=pltpu.SMEM),
                pl.BlockSpec((TD, d), lambda i, pad: (i, 0)),
            ],
            out_specs=pl.BlockSpec(memory_space=pl.ANY),
            scratch_shapes=[pltpu.VMEM((TME, d), h2.dtype), pltpu.SemaphoreType.DMA],
        ),
        compiler_params=_cparams(("arbitrary",)),
        name="moe_dispatch",
    )(pad_start, dest_tiles, h2)


def _expert_body(te_ref, nu_ref, xs_ref, wg_ref, wu_ref, wd_ref, ys_ref, wgb_ref, wub_ref, wdb_ref):
    i = pl.program_id(0)
    used = i < nu_ref[0]
    new_expert = jnp.logical_or(i == 0, te_ref[i] != te_ref[jnp.maximum(i - 1, 0)])

    @pl.when(jnp.logical_and(used, new_expert))
    def _():
        wgb_ref[...] = wg_ref[0, 0].astype(BF16)
        wub_ref[...] = wu_ref[0, 0].astype(BF16)
        wdb_ref[...] = wd_ref[0, 0].astype(BF16)

    @pl.when(used)
    def _():
        nblk = 2
        rb = TME // nblk
        xb = [_unpack_pairs(xs_ref[k * rb:(k + 1) * rb, :]).astype(BF16) for k in range(nblk)]
        hg = [_dot(v, wgb_ref[...]) for v in xb]
        hu = [_dot(v, wub_ref[...]) for v in xb]
        hid = [(_silu(g) * u).astype(BF16) for g, u in zip(hg, hu)]
        for k in range(nblk):
            ys_ref[k * rb:(k + 1) * rb, :] = _pack_pairs(_dot(hid[k], wdb_ref[...]))

    @pl.when(jnp.logical_not(used))
    def _():
        ys_ref[...] = jnp.zeros_like(ys_ref)


def _experts(layer, tile_expert, n_used, xs, rows, w_gate, w_up, w_down):
    dp = xs.shape[1]
    d = 2 * dp
    ntiles = rows // TME
    return pl.pallas_call(
        _expert_body,
        out_shape=jax.ShapeDtypeStruct((rows, dp), xs.dtype),
        grid_spec=pltpu.PrefetchScalarGridSpec(
            num_scalar_prefetch=2,
            grid=(ntiles,),
            in_specs=[
                pl.BlockSpec((TME, dp), lambda i, te, nu: (jnp.minimum(i, nu[0] - 1), 0)),
                pl.BlockSpec((1, 1, d, FF), lambda i, te, nu: (layer, te[i], 0, 0)),
                pl.BlockSpec((1, 1, d, FF), lambda i, te, nu: (layer, te[i], 0, 0)),
                pl.BlockSpec((1, 1, FF, d), lambda i, te, nu: (layer, te[i], 0, 0)),
            ],
            out_specs=pl.BlockSpec((TME, dp), lambda i, te, nu: (i, 0)),
            scratch_shapes=[pltpu.VMEM((d, FF), BF16), pltpu.VMEM((d, FF), BF16), pltpu.VMEM((FF, d), BF16)],
        ),
        compiler_params=_cparams(("arbitrary",)),
        name="moe_experts",
    )(tile_expert, n_used, xs, w_gate, w_up, w_down)


def _gather_pairs(dest_ref, next_ref, ys_ref, buf_ref, sem):
    step = pl.program_id(0) * pl.num_programs(1) + pl.program_id(1)
    nsteps = pl.num_programs(0) * pl.num_programs(1)
    cur = lax.rem(step, 2)

    def start_tile(idx_ref, b):
        for t in range(TM):
            for sl in range(2):
                src = idx_ref[0, 0, sl * TM + t]
                pltpu.make_async_copy(ys_ref.at[pl.ds(src, 1)], buf_ref.at[b, sl, pl.ds(t, 1)],
                                      sem.at[b]).start(priority=sl)

    @pl.when(step == 0)
    def _():
        start_tile(dest_ref, 0)

    for b in range(2):
        @pl.when(jnp.logical_and(step + 1 < nsteps, cur == 1 - b))
        def _():
            start_tile(next_ref, b)

    for sl in range(2):
        pltpu.make_async_copy(ys_ref.at[pl.ds(0, TM)], buf_ref.at[cur, sl], sem.at[cur]).wait()
    return buf_ref.at[cur]


def _moe_combine(wc_ref, rows_ref):
    return wc_ref[:, 0:1] * _unpack_pairs(rows_ref[0]) + wc_ref[:, 1:2] * _unpack_pairs(rows_ref[1])


def _gather_specs(nt, ntiles):
    return [
        pl.BlockSpec((1, 1, 2 * TM), lambda b, i: (b * nt + i, 0, 0), memory_space=pltpu.SMEM),
        pl.BlockSpec((1, 1, 2 * TM), lambda b, i: (jnp.minimum(b * nt + i + 1, ntiles - 1), 0, 0),
                     memory_space=pltpu.SMEM),
    ]


GATHER_SCRATCH = [pltpu.VMEM((2, 2, TM, D // 2), U32), pltpu.SemaphoreType.DMA((2,))]


def _routing_plan(meta_i, counts, tile_tokens):
    t = meta_i.shape[1]
    cnt_perm = counts[:, 0].astype(I32)
    e = jnp.arange(N_EXP)
    cnt = cnt_perm[(e % EXP_PER_GROUP) * N_GROUPS + e // EXP_PER_GROUP]
    tiles_e = (cnt + TME - 1) // TME
    cum_tiles = jnp.cumsum(tiles_e)
    offs = (cum_tiles - tiles_e) * TME
    sel = meta_i[0:2][..., None] == e
    dest = jnp.sum(jnp.where(sel, offs, 0), axis=-1) + meta_i[2:4]
    ntiles = (2 * t) // TME + N_EXP
    n_used = cum_tiles[-1]
    te = jnp.sum(cum_tiles[None, :] <= jnp.arange(ntiles, dtype=I32)[:, None], axis=-1).astype(I32)
    last_e = jnp.max(jnp.where(tiles_e > 0, e, 0)).astype(I32)
    te = jnp.minimum(te, last_e)

    def tiles(n):
        return dest.reshape(2, t // n, n).transpose(1, 0, 2).reshape(t // n, 1, 2 * n)

    pad_start = jnp.concatenate([jnp.where(tiles_e > 0, cum_tiles - 1, -1), n_used[None]]).astype(I32)
    return pad_start, tiles(TD), tiles(tile_tokens), te, n_used.reshape(1).astype(I32), ntiles * TME


def _moe(layer, h2, meta_i, counts, w_gate, w_up, w_down):
    pad_start, dest_td, dest_tm, te, n_used, rows = _routing_plan(meta_i, counts, TM)
    xs = _dispatch(pad_start, dest_td, h2, rows)
    ys = _experts(layer, te, n_used, xs, rows, w_gate, w_up, w_down)
    return ys, dest_tm


def _conf_in_body(x_ref, dest_ref, next_ref, wc_ref, ys_ref, pv_ref, w1_ref, b1_ref, x_out_ref, u_ref,
                  buf_ref, sem):
    rows_ref = _gather_pairs(dest_ref, next_ref, ys_ref, buf_ref, sem)
    x2 = x_ref[0] + pv_ref[0, 0:1, :] * _moe_combine(wc_ref, rows_ref)
    x_out_ref[0] = x2
    h = (_rms(x2) * pv_ref[0, 1:2, :] + pv_ref[0, 2:3, :]).astype(BF16)
    u = _dot(h, w1_ref[...]) + b1_ref[0:1, :]
    u_ref[0] = (u[:, :D] * _sigmoid(u[:, D:])).astype(BF16)


def _conf_in(x1, dest_tm, wcol, ys, pvec, w1, b1):
    bsz, seq, d = x1.shape
    nt = seq // TM
    return pl.pallas_call(
        _conf_in_body,
        out_shape=(jax.ShapeDtypeStruct((bsz, seq, d), F32), jax.ShapeDtypeStruct((bsz, seq, d), BF16)),
        grid=(bsz, nt),
        in_specs=[pl.BlockSpec((1, TM, d), lambda b, i: (b, i, 0))] + _gather_specs(nt, bsz * nt) + [
            pl.BlockSpec((TM, 8), lambda b, i: (b * nt + i, 0)),
            pl.BlockSpec(memory_space=pl.ANY),
            pl.BlockSpec((1, 8, d), lambda b, i: (b, 0, 0)),
            pl.BlockSpec((d, 2 * d), lambda b, i: (0, 0)),
            pl.BlockSpec((8, 2 * d), lambda b, i: (0, 0)),
        ],
        out_specs=(pl.BlockSpec((1, TM, d), lambda b, i: (b, i, 0)),
                   pl.BlockSpec((1, TM, d), lambda b, i: (b, i, 0))),
        scratch_shapes=GATHER_SCRATCH,
        compiler_params=_cparams(("arbitrary", "arbitrary")),
        name="combine_conf_in",
    )(x1, dest_tm, dest_tm, wcol, ys, pvec, w1, b1)


def _conf_mix_body(nt, cur_ref, prev_ref, next_ref, dw_ref, cv_ref, x_ref, w2_ref, pv_ref, rwt_ref,
                   rb_ref, upper_ref, x_out_ref, h2_ref, mi_ref, mf_ref, cnt_ref, ext_ref, sh_ref,
                   conv_ref, carry_ref):
    i = pl.program_id(1)
    ext_ref[0:HALO, :] = jnp.where(i == 0, 0.0, prev_ref[0].astype(F32))
    ext_ref[HALO:HALO + TMR, :] = cur_ref[0].astype(F32)
    ext_ref[HALO + TMR:2 * HALO + TMR, :] = jnp.where(i == nt - 1, 0.0, next_ref[0].astype(F32))
    pad = CONF_K // 2
    span = TMR + CONF_SPAN
    for cb in range(D // LANES):
        cs = slice(cb * LANES, (cb + 1) * LANES)
        for r in range(1, 8):
            sh_ref[r - 1, :, cs] = ext_ref[r:r + span, cs]
        for rb in range(0, TMR, CONF_ROWS):
            acc = None
            for j in range(CONF_K):
                lo = HALO - pad + j
                r, base = lo % 8, lo - lo % 8 + rb
                src = ext_ref[base:base + CONF_ROWS, cs] if r == 0 else sh_ref[r - 1, base:base + CONF_ROWS, cs]
                term = src * dw_ref[j:j + 1, cs]
                acc = term if acc is None else acc + term
            conv_ref[rb:rb + CONF_ROWS, cs] = acc + cv_ref[0:1, cs]
    u = conv_ref[...]
    mu = jnp.mean(u, axis=-1, keepdims=True)
    uc = u - mu
    var = jnp.mean(uc * uc, axis=-1, keepdims=True)
    y = uc * lax.rsqrt(var + EPS) * cv_ref[1:2, :] + cv_ref[2:3, :]
    mix = _dot(_silu(y).astype(BF16), w2_ref[...]) + cv_ref[3:4, :]
    x_new = x_ref[0] + pv_ref[0, 0:1, :] * mix
    _route_tail(x_new, pv_ref, rwt_ref, rb_ref, upper_ref, x_out_ref, h2_ref, mi_ref, mf_ref, cnt_ref,
                carry_ref)


def _conf_mix(u, dw32, cvec, x2, w2, pvec, rwt, rb, upper):
    bsz, seq, d = x2.shape
    nt = seq // TMR
    r = TMR // HALO
    cst = lambda shape: pl.BlockSpec(shape, lambda b, i: (0,) * len(shape))
    return pl.pallas_call(
        functools.partial(_conf_mix_body, nt),
        out_shape=_route_out_shapes(bsz, seq),
        grid=(bsz, nt),
        in_specs=[
            pl.BlockSpec((1, TMR, d), lambda b, i: (b, i, 0)),
            pl.BlockSpec((1, HALO, d), lambda b, i: (b, jnp.maximum(i * r - 1, 0), 0)),
            pl.BlockSpec((1, HALO, d), lambda b, i: (b, jnp.minimum((i + 1) * r, nt * r - 1), 0)),
            cst((32, d)), cst((8, d)),
            pl.BlockSpec((1, TMR, d), lambda b, i: (b, i, 0)),
            cst((d, d)),
            pl.BlockSpec((1, 8, d), lambda b, i: (b, 0, 0)),
        ] + _route_in_specs(),
        out_specs=_route_out_specs(nt),
        scratch_shapes=[pltpu.VMEM((TMR + 2 * HALO, d), F32), pltpu.VMEM((7, TMR + CONF_SPAN, d), F32),
                        pltpu.VMEM((TMR, d), F32),
                        pltpu.VMEM((N_EXP, LANES), F32)],
        compiler_params=_cparams(("arbitrary", "arbitrary")),
        name="conf_mix_router",
    )(u, u, u, dw32, cvec, x2, w2, pvec, rwt, rb, upper)


def _final_body(x_ref, dest_ref, next_ref, wc_ref, ys_ref, pv_ref, o_ref, buf_ref, sem):
    rows_ref = _gather_pairs(dest_ref, next_ref, ys_ref, buf_ref, sem)
    x4 = x_ref[0] + pv_ref[0, 0:1, :] * _moe_combine(wc_ref, rows_ref)
    o_ref[0] = _rms(x4) * pv_ref[0, 1:2, :]


def _final(x3, dest_tm, wcol, ys, pvec):
    bsz, seq, d = x3.shape
    nt = seq // TM
    return pl.pallas_call(
        _final_body,
        out_shape=jax.ShapeDtypeStruct((bsz, seq, d), F32),
        grid=(bsz, nt),
        in_specs=[pl.BlockSpec((1, TM, d), lambda b, i: (b, i, 0))] + _gather_specs(nt, bsz * nt) + [
            pl.BlockSpec((TM, 8), lambda b, i: (b * nt + i, 0)),
            pl.BlockSpec(memory_space=pl.ANY),
            pl.BlockSpec((1, 8, d), lambda b, i: (b, 0, 0)),
        ],
        out_specs=pl.BlockSpec((1, TM, d), lambda b, i: (b, i, 0)),
        scratch_shapes=GATHER_SCRATCH,
        compiler_params=_cparams(("arbitrary", "arbitrary")),
        name="combine_final",
    )(x3, dest_tm, dest_tm, wcol, ys, pvec)


def _grid_sincos(rows, cols, dim):
    quarter = dim // 4
    omega = 1.0 / jnp.power(POS_BASE, jnp.arange(quarter, dtype=F32) / quarter)

    def axis_emb(n):
        ang = jnp.arange(n, dtype=F32)[:, None] * omega[None, :]
        return jnp.concatenate([jnp.sin(ang), jnp.cos(ang)], axis=-1)

    er = jnp.broadcast_to(axis_emb(rows)[:, None, :], (rows, cols, dim // 2))
    ec = jnp.broadcast_to(axis_emb(cols)[None, :, :], (rows, cols, dim // 2))
    return jnp.concatenate([er, ec], axis=-1).reshape(rows * cols, dim)


def _pad_rows(a, rows):
    return jnp.pad(a, ((0, rows - a.shape[0]),) + ((0, 0),) * (a.ndim - 1))


def _per_batch(*vecs):
    st = jnp.stack(vecs, axis=1)
    return jnp.pad(st, ((0, 0), (0, 8 - st.shape[1]), (0, 0)))


def _width_dft():
    k = np.arange(LANES)
    ang = 2 * np.pi * np.outer(k, k) / LANES
    groups = FN_W // LANES
    c = np.kron(np.eye(groups), np.cos(ang)) / math.sqrt(LANES)
    s = np.kron(np.eye(groups), np.sin(ang)) / math.sqrt(LANES)
    return jnp.asarray(np.concatenate([c, s], axis=1), dtype=BF16)


def kernel(x, c, ctx, c_ctx, ada_w, ada_b, norm1_g, norm2_g, hyb_w_in, dn_conv_w, dn_a_log, dn_dt_bias,
           dn_onorm_g, hyb_w_out, conf_w1, conf_b1, conf_dw_w, conf_dw_b, conf_ln_g, conf_ln_b, conf_w2,
           conf_b2, router_w, router_bias, moe_w_gate, moe_w_up, moe_w_down, final_g):
    bsz, seq, d = x.shape
    ctx_len = ctx.shape[1]
    assert d == D and seq % TM == 0 and seq % TMR == 0 and ctx_len % TM == 0 and (bsz * seq) % TD == 0
    nct = ctx_len // TM
    t = bsz * seq

    pos = _grid_sincos(seq // GRID_W, GRID_W, d)

    cpad = _pad_rows(jnp.concatenate([c, c_ctx[None, :]], axis=0), 16)
    mods = _adaln(cpad, ada_w, ada_b)
    m = lambda l, k: mods[l, :, k * d:(k + 1) * d]
    a1 = norm1_g[0][None, :] * (1.0 + m(0, 1))
    mod_a = jnp.stack([jnp.broadcast_to(a1[bsz], (bsz, d)), a1[:bsz]], axis=1)
    mod_b = jnp.stack([jnp.broadcast_to(m(0, 0)[bsz], (bsz, d)), m(0, 0)[:bsz]], axis=1)

    w_in = hyb_w_in[0]
    z_lo = QKV_W + AB_W
    f_lo = z_lo + Z_W
    wqkv = w_in[:, :QKV_W].astype(BF16)
    wab = jnp.pad(w_in[:, QKV_W:z_lo], ((0, 0), (0, LANES - AB_W))).astype(BF16)
    wz = w_in[:, z_lo:f_lo].astype(BF16)
    wf = w_in[:, f_lo:].astype(BF16)
    gpar = jnp.zeros((8, LANES), F32)
    gpar = gpar.at[0, :2 * HEADS].set(-jnp.exp(dn_a_log[0].reshape(-1)))
    gpar = gpar.at[1, :2 * HEADS].set(dn_dt_bias[0].reshape(-1))
    qkv_pre, z, fc, fs, gates = _inproj(ctx, x, pos, mod_a, mod_b, wqkv, wz, wf, wab, _width_dft(), gpar)

    qkv = _dnconv(qkv_pre, _pad_rows(dn_conv_w[0], 8), nct)
    lc = qkv.shape[1]
    nchunks = lc // CHUNK
    grow = gates[:, :, :AB_W].reshape(bsz, nchunks, CHUNK, 4, HEADS)
    grow = grow.transpose(0, 1, 3, 4, 2).reshape(bsz, nchunks, 4, HB)
    grow = jnp.pad(grow, ((0, 0), (0, 0), (0, 4), (0, 0)))
    o_f, o_b = _deltanet(qkv, grow, seq, ctx_len // CHUNK)

    fn = _seq_dft(fc, fs)

    perm = (np.arange(N_EXP) % N_GROUPS) * EXP_PER_GROUP + np.arange(N_EXP) // N_GROUPS
    rwt = router_w.T[perm].astype(BF16)
    rb = jnp.broadcast_to(router_bias[perm][:, None], (N_EXP, LANES)).astype(F32)
    upper = jnp.asarray(np.triu(np.ones((TMR, TMR), np.float32), 1), dtype=BF16)
    w_out = hyb_w_out[0].astype(BF16)
    onorm8 = _pad_rows(jnp.tile(dn_onorm_g[0], HEADS)[None, :], 8)
    pv0 = _per_batch(m(0, 2)[:bsz], norm2_g[0][None, :] * (1.0 + m(0, 4)[:bsz]), m(0, 3)[:bsz])
    x1, h2, meta_i, meta_f, counts = _mix0(o_f, o_b, z, fn, x, pos, w_out[:V_W], w_out[V_W:], onorm8, pv0,
                                           rwt, rb, upper)
    ys, dest_tm = _moe(0, h2, meta_i, counts, moe_w_gate, moe_w_up, moe_w_down)
    wcol = meta_f.T

    pv1 = _per_batch(m(0, 5)[:bsz], norm1_g[1][None, :] * (1.0 + m(1, 1)[:bsz]), m(1, 0)[:bsz])
    x2, u = _conf_in(x1, dest_tm, wcol, ys, pv1, conf_w1[0].astype(BF16), _pad_rows(conf_b1[0][None, :], 8))
    cvec = _pad_rows(jnp.stack([conf_dw_b[0], conf_ln_g[0], conf_ln_b[0], conf_b2[0]]), 8)
    pv2 = _per_batch(m(1, 2)[:bsz], norm2_g[1][None, :] * (1.0 + m(1, 4)[:bsz]), m(1, 3)[:bsz])
    x3, h2, meta_i, meta_f, counts = _conf_mix(u, _pad_rows(conf_dw_w[0], 32), cvec, x2,
                                               conf_w2[0].astype(BF16), pv2, rwt, rb, upper)
    ys, dest_tm = _moe(1, h2, meta_i, counts, moe_w_gate, moe_w_up, moe_w_down)
    pv3 = _per_batch(m(1, 5)[:bsz], jnp.broadcast_to(final_g[None, :], (bsz, d)))
    return _final(x3, dest_tm, meta_f.T, ys, pv3)
```

```python
import functools
import math

import numpy as np
import jax
import jax.numpy as jnp
from jax import lax
from jax.experimental import pallas as pl
from jax.experimental.pallas import tpu as pltpu

F32 = jnp.float32
BF16 = jnp.bfloat16
I32 = jnp.int32
U32 = jnp.uint32

D = 1024
GRID_W = 64
HEADS = 4
DK = 128
DV = 128
CHUNK = 64
QK_W = HEADS * DK
V_W = HEADS * DV
QKV_W = 2 * QK_W + V_W
AB_W = 16
Z_W = V_W
FN_W = 512
DN_CONV_K = 5
CONF_K = 31
N_EXP = 32
N_GROUPS = 8
EXP_PER_GROUP = 4
FF = 512
EPS = 1e-6
POS_BASE = 10000.0

LANES = 128
HALO = 16
TM = 256
TMR = 512
TMG = 512
TME = 512
HB = HEADS * CHUNK
DN_STACK = 2
HS = DN_STACK * CHUNK
DN_GROUP = 8
CONF_SPAN = 24
CONF_ROWS = 64
VMEM_LIMIT = 56 * 1024 * 1024


def _cparams(sem):
    return pltpu.CompilerParams(dimension_semantics=sem, vmem_limit_bytes=VMEM_LIMIT)


def _sigmoid(v):
    return jax.nn.sigmoid(v)


def _silu(v):
    return v * jax.nn.sigmoid(v)


def _dot(a, b):
    return jnp.dot(a, b, preferred_element_type=F32)


def _dot_nt(a, b):
    return lax.dot_general(a, b, (((1,), (1,)), ((), ())), preferred_element_type=F32)


def _split3(a):
    a1 = a.astype(BF16)
    r1 = a - a1.astype(F32)
    a2 = r1.astype(BF16)
    a3 = (r1 - a2.astype(F32)).astype(BF16)
    return a1, a2, a3


def _pack_pairs(v):
    n = v.shape[1] // 2
    lo = lax.bitcast_convert_type(v[:, :n].astype(BF16).astype(F32), U32) >> 16
    hi = lax.bitcast_convert_type(v[:, n:].astype(BF16).astype(F32), U32) & jnp.uint32(0xFFFF0000)
    return hi | lo


def _unpack_pairs(p):
    lo = lax.bitcast_convert_type(p << 16, F32)
    hi = lax.bitcast_convert_type(p & jnp.uint32(0xFFFF0000), F32)
    return jnp.concatenate([lo, hi], axis=1)


def _rms(v, eps=EPS):
    return v * lax.rsqrt(jnp.mean(v * v, axis=-1, keepdims=True) + eps)


def _adaln_body(c_ref, w_ref, b_ref, o_ref):
    s = _silu(c_ref[...])
    o_ref[0] = jnp.dot(s, w_ref[0], preferred_element_type=F32,
                       precision=lax.Precision.HIGHEST) + b_ref[0]


def _adaln(cpad, ada_w, ada_b):
    depth, d, n = ada_w.shape
    tn = 1536
    rows = cpad.shape[0]
    return pl.pallas_call(
        _adaln_body,
        out_shape=jax.ShapeDtypeStruct((depth, rows, n), F32),
        grid=(depth, n // tn),
        in_specs=[
            pl.BlockSpec((rows, d), lambda l, j: (0, 0)),
            pl.BlockSpec((1, d, tn), lambda l, j: (l, 0, j)),
            pl.BlockSpec((1, 1, tn), lambda l, j: (l, 0, j)),
        ],
        out_specs=pl.BlockSpec((1, rows, tn), lambda l, j: (l, 0, j)),
        compiler_params=_cparams(("arbitrary", "arbitrary")),
        name="adaln",
    )(cpad, ada_w, ada_b.reshape(depth, 1, n))


def _inproj_body(nct, ctx_ref, x_ref, pos_ref, ma_ref, mb_ref, wqkv_ref, wz_ref, wf_ref, wab_ref,
                 dftw_ref, gpar_ref, qkv_ref, z_ref, fc_ref, fs_ref, gate_ref):
    i = pl.program_id(1)
    is_ctx = i < nct
    xin = jnp.where(is_ctx, ctx_ref[0], x_ref[0] + pos_ref[...])
    a = jnp.where(is_ctx, ma_ref[0, 0:1, :], ma_ref[0, 1:2, :])
    b = jnp.where(is_ctx, mb_ref[0, 0:1, :], mb_ref[0, 1:2, :])
    h = (_rms(xin) * a + b).astype(BF16)
    qkv_ref[0] = _dot(h, wqkv_ref[...]).astype(BF16)
    z_ref[0] = _dot(h, wz_ref[...]).astype(BF16)
    f = _dot(h, wf_ref[...]).astype(BF16)
    fcs = _dot(f, dftw_ref[...])
    fc_ref[0] = fcs[:, :FN_W].astype(BF16)
    fs_ref[0] = fcs[:, FN_W:].astype(BF16)
    ab = _dot(h, wab_ref[...])
    lane = lax.broadcasted_iota(I32, ab.shape, 1)
    pre = ab + gpar_ref[1:2, :]
    softplus = jnp.maximum(pre, 0.0) + jnp.log1p(jnp.exp(-jnp.abs(pre)))
    gate_ref[0] = jnp.where(lane < 2 * HEADS, gpar_ref[0:1, :] * softplus, _sigmoid(ab))


def _inproj(ctx, x, pos, mod_a, mod_b, wqkv, wz, wf, wab, dftw, gpar):
    bsz, seq, d = x.shape
    nct = ctx.shape[1] // TM
    nt = nct + seq // TM
    lc = nt * TM
    xi = lambda b, i: (b, jnp.maximum(i - nct, 0), 0)
    full = lambda shape: pl.BlockSpec(shape, lambda b, i: (0,) * len(shape))
    return pl.pallas_call(
        functools.partial(_inproj_body, nct),
        out_shape=(
            jax.ShapeDtypeStruct((bsz, lc, QKV_W), BF16),
            jax.ShapeDtypeStruct((bsz, seq, Z_W), BF16),
            jax.ShapeDtypeStruct((bsz, seq, FN_W), BF16),
            jax.ShapeDtypeStruct((bsz, seq, FN_W), BF16),
            jax.ShapeDtypeStruct((bsz, lc, LANES), F32),
        ),
        grid=(bsz, nt),
        in_specs=[
            pl.BlockSpec((1, TM, d), lambda b, i: (b, jnp.minimum(i, nct - 1), 0)),
            pl.BlockSpec((1, TM, d), xi),
            pl.BlockSpec((TM, d), lambda b, i: (jnp.maximum(i - nct, 0), 0)),
            pl.BlockSpec((1, 2, d), lambda b, i: (b, 0, 0)),
            pl.BlockSpec((1, 2, d), lambda b, i: (b, 0, 0)),
            full((d, QKV_W)), full((d, Z_W)), full((d, FN_W)), full((d, LANES)),
            full((FN_W, 2 * FN_W)), full((8, LANES)),
        ],
        out_specs=(
            pl.BlockSpec((1, TM, QKV_W), lambda b, i: (b, i, 0)),
            pl.BlockSpec((1, TM, Z_W), xi),
            pl.BlockSpec((1, TM, FN_W), xi),
            pl.BlockSpec((1, TM, FN_W), xi),
            pl.BlockSpec((1, TM, LANES), lambda b, i: (b, i, 0)),
        ),
        compiler_params=_cparams(("arbitrary", "arbitrary")),
        name="inproj",
    )(ctx, x, pos, mod_a, mod_b, wqkv, wz, wf, wab, dftw, gpar)


def _dnconv_body(nct, nt, cur_ref, prev_ref, next_ref, w_ref, o_ref, ext_ref):
    i = pl.program_id(1)
    first = jnp.logical_or(i == 0, i == nct)
    last = jnp.logical_or(i == nct - 1, i == nt - 1)
    ext_ref[0:HALO, :] = jnp.where(first, 0.0, prev_ref[0].astype(F32))
    ext_ref[HALO:HALO + TM, :] = cur_ref[0].astype(F32)
    ext_ref[HALO + TM:2 * HALO + TM, :] = jnp.where(last, 0.0, next_ref[0].astype(F32))
    pad = DN_CONV_K // 2
    for cb in range(QKV_W // LANES):
        cs = slice(cb * LANES, (cb + 1) * LANES)
        acc = None
        for j in range(DN_CONV_K):
            lo = HALO - pad + j
            term = ext_ref[lo:lo + TM, cs] * w_ref[j:j + 1, cs]
            acc = term if acc is None else acc + term
        y = _silu(acc)
        if cb < 2 * HEADS:
            y = y * lax.rsqrt(jnp.sum(y * y, axis=-1, keepdims=True) + EPS)
        o_ref[0, :, cs] = y.astype(BF16)


def _dnconv(qkv_pre, conv_w8, nct):
    bsz, lc, w = qkv_pre.shape
    nt = lc // TM
    r = TM // HALO
    return pl.pallas_call(
        functools.partial(_dnconv_body, nct, nt),
        out_shape=jax.ShapeDtypeStruct((bsz, lc, w), BF16),
        grid=(bsz, nt),
        in_specs=[
            pl.BlockSpec((1, TM, w), lambda b, i: (b, i, 0)),
            pl.BlockSpec((1, HALO, w), lambda b, i: (b, jnp.maximum(i * r - 1, 0), 0)),
            pl.BlockSpec((1, HALO, w), lambda b, i: (b, jnp.minimum((i + 1) * r, nt * r - 1), 0)),
            pl.BlockSpec((8, w), lambda b, i: (0, 0)),
        ],
        out_specs=pl.BlockSpec((1, TM, w), lambda b, i: (b, i, 0)),
        scratch_shapes=[pltpu.VMEM((TM + 2 * HALO, w), F32)],
        compiler_params=_cparams(("arbitrary", "arbitrary")),
        name="dnconv",
    )(qkv_pre, qkv_pre, qkv_pre, conv_w8)


def _dn_setup(d, st, qkv, grow, mask_incl, mask_strict, trit, bdones, eye, lvl0):
    neg_inf = float("-inf")
    heads = range(st * DN_STACK, (st + 1) * DN_STACK)
    stack = lambda off: jnp.concatenate([qkv[:, off + h * DK: off + (h + 1) * DK] for h in heads], axis=0)
    q_s = stack(0)
    k_s = stack(QK_W)
    v_s = stack(2 * QK_W)
    grow = grow[:, st * HS:(st + 1) * HS]
    g3 = _split3(grow)
    sums = sum(_dot(t, jnp.concatenate([trit, bdones], axis=1)) for t in g3)
    cg_row = sums[:, :HS]
    tot_row = sums[:, HS:]
    rows = jnp.concatenate([cg_row, tot_row, grow, jnp.zeros((LANES - 24, HS), F32)], axis=0)
    cols = rows.T
    cgc = cols[:, d:d + 1]
    totc = cols[:, 8 + d:9 + d]
    betac = cols[:, 18 + d:19 + d]
    cgr = cg_row[d:d + 1, :]
    dmat = jnp.exp(jnp.where(mask_incl > 0.5, cgc - cgr, neg_inf))
    kk = _dot_nt(k_s, k_s)
    qk = _dot_nt(q_s, k_s)
    a = jnp.where(mask_strict > 0.5, kk * dmat, 0.0) * betac
    ecg = jnp.exp(cgc)
    k32 = k_s.astype(F32)
    scale = DK ** -0.5
    return dict(
        d=d, st=st, a=a, x=eye.astype(F32) - a * lvl0, totc=totc,
        rhs=jnp.concatenate([(betac * v_s.astype(F32)).astype(BF16),
                             ((betac * ecg) * k32).astype(BF16)], axis=1),
        ktail=(k32 * jnp.exp(totc - cgc)).astype(BF16),
        qdec=(q_s.astype(F32) * (ecg * scale)).astype(BF16),
        qkd=(qk * dmat * scale).astype(BF16))


def _dn_body(qf_ref, qb_ref, gf_ref, gb_ref, mli_ref, mls_ref, mui_ref, mus_ref, bdl_ref, bdu_ref,
             bdo_ref, eye_ref, lvl_ref, of_ref, ob_ref, s_ref):
    @pl.when(pl.program_id(1) == 0)
    def _():
        s_ref[...] = jnp.zeros_like(s_ref)

    bdl = bdl_ref[...]
    bdu = bdu_ref[...]
    bdo = bdo_ref[...]
    eye = eye_ref[...]
    chains = []
    for g in range(qf_ref.shape[0]):
        for st in range(HEADS // DN_STACK):
            c = _dn_setup(0, st, qf_ref[g], gf_ref[g, 0], mli_ref[...], mls_ref[...], bdu, bdo, eye,
                          lvl_ref[0])
            c.update(g=g, out=of_ref)
            chains.append(c)
            c = _dn_setup(1, st, qb_ref[g], gb_ref[g, 0], mui_ref[...], mus_ref[...], bdl, bdo, eye,
                          lvl_ref[0])
            c.update(g=g, out=ob_ref)
            chains.append(c)
    for lvl in range(1, int(math.log2(CHUNK))):
        for c in chains:
            c["xb"] = c["x"].astype(BF16)
            c["y"] = _dot((c["a"] * lvl_ref[lvl]).astype(BF16), c["xb"]).astype(BF16)
        for c in chains:
            c["x"] = c["x"] - _dot(c["xb"], c["y"])
    for c in chains:
        uw = _dot(c["x"].astype(BF16), c["rhs"])
        c["u"] = uw[:, :DV]
        c["w"] = uw[:, DV:].astype(BF16)
    for c in chains:
        c["vn"] = []
        c["oi"] = []
    for hh in range(DN_STACK):
        rs = slice(hh * CHUNK, (hh + 1) * CHUNK)
        for c in chains:
            s = s_ref[c["g"], c["d"], c["st"] * DN_STACK + hh]
            sb = s.astype(BF16)
            c["s"] = s
            c["vn"].append(c["u"][rs] - _dot(c["w"][rs], sb))
            c["oi"].append(_dot(c["qdec"][rs], sb))
        for c in chains:
            cd = jnp.exp(c["totc"][hh * CHUNK:hh * CHUNK + 1, :])
            upd = lax.dot_general(c["ktail"][rs], c["vn"][hh].astype(BF16), (((0,), (0,)), ((), ())),
                                  preferred_element_type=F32)
            s_ref[c["g"], c["d"], c["st"] * DN_STACK + hh] = c["s"] * cd + upd
    for c in chains:
        vn_s = jnp.concatenate(c["vn"], axis=0).astype(BF16)
        o_s = jnp.concatenate(c["oi"], axis=0) + _dot(c["qkd"], vn_s)
        for hh in range(DN_STACK):
            h = c["st"] * DN_STACK + hh
            c["out"][c["g"], :, h * DV:(h + 1) * DV] = o_s[hh * CHUNK:(hh + 1) * CHUNK].astype(BF16)


def _dn_consts():
    r = np.arange(HS)
    same = (r[:, None] // CHUNK) == (r[None, :] // CHUNK)
    li = same & (r[:, None] >= r[None, :])
    ls = same & (r[:, None] > r[None, :])
    ui = same & (r[:, None] <= r[None, :])
    us = same & (r[:, None] < r[None, :])
    f = lambda m: jnp.asarray(m.astype(np.float32))
    h = lambda m: jnp.asarray(m.astype(np.float32), dtype=BF16)
    nlev = int(math.log2(CHUNK))
    lv = [((r[:, None] >> (l + 1)) == (r[None, :] >> (l + 1))) & ((r[:, None] >> l) != (r[None, :] >> l))
          for l in range(nlev)]
    return (f(li), f(ls), f(ui), f(us), h(li), h(ui), h(same), h(np.eye(HS, dtype=bool)),
            jnp.asarray(np.stack(lv).astype(np.float32)))


def _deltanet(qkv, grow, seq, ncc):
    bsz, lc, w = qkv.shape
    nchunks = lc // CHUNK
    nx = seq // CHUNK
    fwd = lambda s: s
    bwd = lambda s: jnp.where(s < ncc, ncc - 1 - s, ncc + nchunks - 1 - s)
    ofi = lambda b, s: (b, jnp.maximum(s - ncc, 0), 0)
    obi = lambda b, s: (b, jnp.minimum(nchunks - 1 - s, nx - 1), 0)
    cst = lambda shape: pl.BlockSpec(shape, lambda b, s: (0, 0))
    consts = _dn_consts()
    grp = math.gcd(bsz, DN_GROUP)
    return pl.pallas_call(
        _dn_body,
        out_shape=(jax.ShapeDtypeStruct((bsz, seq, V_W), BF16),
                   jax.ShapeDtypeStruct((bsz, seq, V_W), BF16)),
        grid=(bsz // grp, nchunks),
        in_specs=[
            pl.BlockSpec((grp, CHUNK, w), lambda b, s: (b, fwd(s), 0)),
            pl.BlockSpec((grp, CHUNK, w), lambda b, s: (b, bwd(s), 0)),
            pl.BlockSpec((grp, 1, 8, HB), lambda b, s: (b, fwd(s), 0, 0)),
            pl.BlockSpec((grp, 1, 8, HB), lambda b, s: (b, bwd(s), 0, 0)),
        ] + [cst((HS, HS))] * 8 + [pl.BlockSpec((int(math.log2(CHUNK)), HS, HS), lambda b, s: (0, 0, 0))],
        out_specs=(pl.BlockSpec((grp, CHUNK, V_W), ofi), pl.BlockSpec((grp, CHUNK, V_W), obi)),
        scratch_shapes=[pltpu.VMEM((grp, 2, HEADS, DK, DV), F32)],
        compiler_params=_cparams(("arbitrary", "arbitrary")),
        name="deltanet",
    )(qkv, qkv, grow, grow, *consts)


def _fft1_body(ns2, fc_ref, fs_ref, m_ref, cos_ref, sin_ref, y_ref):
    n1 = fc_ref.shape[1]
    xin = jnp.concatenate([fc_ref[0], fs_ref[0]], axis=0)
    y = _dot(m_ref[...], xin)
    yr, yi = y[:n1], y[n1:]
    for q in range(ns2):
        cs = slice(q * FN_W, (q + 1) * FN_W)
        c = jnp.concatenate([cos_ref[q]] * (FN_W // LANES), axis=1)
        s = jnp.concatenate([sin_ref[q]] * (FN_W // LANES), axis=1)
        y_ref[0, 0, :, cs] = (yr[:, cs] * c + yi[:, cs] * s).astype(BF16)
        y_ref[0, 1, :, cs] = (yi[:, cs] * c - yr[:, cs] * s).astype(BF16)


def _fft2_body(ns1, y_ref, cs_ref, o_ref):
    for q in range(ns1):
        rhs = jnp.concatenate([y_ref[0, 0, q], y_ref[0, 1, q]], axis=0)
        o_ref[0, :, q * FN_W:(q + 1) * FN_W] = _dot(cs_ref[...], rhs).astype(BF16)


def _seq_dft(fc, fs):
    bsz, seq, w = fc.shape
    n2 = GRID_W
    n1 = seq // n2
    k1 = np.arange(n1)
    ang1 = 2 * np.pi * np.outer(k1, k1) / n1
    c1, s1 = np.cos(ang1), np.sin(ang1)
    m1 = jnp.asarray(np.block([[c1, -s1], [-s1, -c1]]), dtype=BF16)
    tw = 2 * np.pi * np.outer(np.arange(n2), k1) / seq
    cos_t = jnp.asarray(np.repeat(np.cos(tw)[:, :, None], LANES, axis=2), dtype=F32)
    sin_t = jnp.asarray(np.repeat(np.sin(tw)[:, :, None], LANES, axis=2), dtype=F32)
    k2 = np.arange(n2)
    ang2 = 2 * np.pi * np.outer(k2, k2) / n2
    cs2 = jnp.asarray(np.concatenate([np.cos(ang2), np.sin(ang2)], axis=1) / math.sqrt(seq), dtype=BF16)

    ns2 = 8
    y = pl.pallas_call(
        functools.partial(_fft1_body, ns2),
        out_shape=jax.ShapeDtypeStruct((bsz, 2, n1, n2 * w), BF16),
        grid=(bsz, n2 // ns2),
        in_specs=[
            pl.BlockSpec((1, n1, ns2 * w), lambda b, j: (b, 0, j)),
            pl.BlockSpec((1, n1, ns2 * w), lambda b, j: (b, 0, j)),
            pl.BlockSpec((2 * n1, 2 * n1), lambda b, j: (0, 0)),
            pl.BlockSpec((ns2, n1, LANES), lambda b, j: (j, 0, 0)),
            pl.BlockSpec((ns2, n1, LANES), lambda b, j: (j, 0, 0)),
        ],
        out_specs=pl.BlockSpec((1, 2, n1, ns2 * w), lambda b, j: (b, 0, 0, j)),
        compiler_params=_cparams(("arbitrary", "arbitrary")),
        name="seqdft_stage1",
    )(fc.reshape(bsz, n1, n2 * w), fs.reshape(bsz, n1, n2 * w), m1, cos_t, sin_t)

    ns1 = min(16, n1)
    out = pl.pallas_call(
        functools.partial(_fft2_body, ns1),
        out_shape=jax.ShapeDtypeStruct((bsz, n2, n1 * w), BF16),
        grid=(bsz, n1 // ns1),
        in_specs=[
            pl.BlockSpec((1, 2, ns1, n2, w), lambda b, j: (b, 0, j, 0, 0)),
            pl.BlockSpec((n2, 2 * n2), lambda b, j: (0, 0)),
        ],
        out_specs=pl.BlockSpec((1, n2, ns1 * w), lambda b, j: (b, 0, j)),
        compiler_params=_cparams(("arbitrary", "arbitrary")),
        name="seqdft_stage2",
    )(y.reshape(bsz, 2, n1, n2, w), cs2)
    return out.reshape(bsz, seq, w)


def _route_tail(x_new, pv_ref, rwt_ref, rb_ref, upper_ref, x_out_ref, h2_ref, mi_ref, mf_ref, cnt_ref,
                carry_ref):
    first = jnp.logical_and(pl.program_id(0) == 0, pl.program_id(1) == 0)

    @pl.when(first)
    def _():
        carry_ref[...] = jnp.zeros_like(carry_ref)

    x_out_ref[0] = x_new
    h2 = _rms(x_new) * pv_ref[0, 1:2, :] + pv_ref[0, 2:3, :]
    h2_ref[...] = _pack_pairs(h2)
    logits = _dot_nt(rwt_ref[...], h2.astype(BF16))
    scores = _sigmoid(logits)
    sel = scores + rb_ref[:, 0:1]
    a = [sel[N_GROUPS * j:N_GROUPS * (j + 1)] for j in range(EXP_PER_GROUP)]
    sc = [scores[N_GROUPS * j:N_GROUPS * (j + 1)] for j in range(EXP_PER_GROUP)]
    gs = None
    for j0 in range(EXP_PER_GROUP):
        for j1 in range(j0 + 1, EXP_PER_GROUP):
            pair = a[j0] + a[j1]
            gs = pair if gs is None else jnp.maximum(gs, pair)
    gidx = lax.broadcasted_iota(I32, gs.shape, 0).astype(F32)
    gmax = jnp.max(gs, axis=0, keepdims=True)
    best = jnp.min(jnp.where(gs == gmax, gidx, float(N_GROUPS)), axis=0, keepdims=True)
    inb = gidx == best
    v = [jnp.sum(jnp.where(inb, a[j], 0.0), axis=0, keepdims=True) for j in range(EXP_PER_GROUP)]
    s = [jnp.sum(jnp.where(inb, sc[j], 0.0), axis=0, keepdims=True) for j in range(EXP_PER_GROUP)]
    v1, i1, s1 = v[0], jnp.zeros_like(v[0]), s[0]
    for j in range(1, EXP_PER_GROUP):
        better = v[j] > v1
        v1 = jnp.where(better, v[j], v1)
        i1 = jnp.where(better, float(j), i1)
        s1 = jnp.where(better, s[j], s1)
    v2 = jnp.full_like(v1, float("-inf"))
    i2 = jnp.zeros_like(v1)
    s2 = jnp.zeros_like(v1)
    for j in range(EXP_PER_GROUP):
        better = jnp.logical_and(i1 != float(j), v[j] > v2)
        v2 = jnp.where(better, v[j], v2)
        i2 = jnp.where(better, float(j), i2)
        s2 = jnp.where(better, s[j], s2)
    denom = s1 + s2
    rid = lax.broadcasted_iota(I32, logits.shape, 0).astype(F32)
    oh1 = rid == (i1 * N_GROUPS + best)
    oh2 = rid == (i2 * N_GROUPS + best)
    oh = jnp.where(jnp.logical_or(oh1, oh2), 1.0, 0.0)
    prefix = _dot(oh.astype(BF16), upper_ref[...])
    base = carry_ref[:, 0:1] + prefix
    rank1 = jnp.sum(jnp.where(oh1, base, 0.0), axis=0, keepdims=True)
    rank2 = jnp.sum(jnp.where(oh2, base, 0.0), axis=0, keepdims=True)
    carry_new = carry_ref[...] + jnp.sum(oh, axis=1, keepdims=True)
    carry_ref[...] = carry_new
    cnt_ref[...] = carry_new
    mi_ref[...] = jnp.zeros_like(mi_ref)
    mi_ref[0:1, :] = (best * EXP_PER_GROUP + i1).astype(I32)
    mi_ref[1:2, :] = (best * EXP_PER_GROUP + i2).astype(I32)
    mi_ref[2:3, :] = rank1.astype(I32)
    mi_ref[3:4, :] = rank2.astype(I32)
    mf_ref[...] = jnp.zeros_like(mf_ref)
    mf_ref[0:1, :] = s1 / denom
    mf_ref[1:2, :] = s2 / denom


def _route_out_shapes(bsz, seq):
    t = bsz * seq
    return (
        jax.ShapeDtypeStruct((bsz, seq, D), F32),
        jax.ShapeDtypeStruct((t, D // 2), U32),
        jax.ShapeDtypeStruct((8, t), I32),
        jax.ShapeDtypeStruct((8, t), F32),
        jax.ShapeDtypeStruct((N_EXP, LANES), F32),
    )


def _route_out_specs(nt):
    flat = lambda b, i: (b * nt + i, 0)
    return (
        pl.BlockSpec((1, TMR, D), lambda b, i: (b, i, 0)),
        pl.BlockSpec((TMR, D // 2), flat),
        pl.BlockSpec((8, TMR), lambda b, i: (0, b * nt + i)),
        pl.BlockSpec((8, TMR), lambda b, i: (0, b * nt + i)),
        pl.BlockSpec((N_EXP, LANES), lambda b, i: (0, 0)),
    )


def _route_in_specs():
    return [
        pl.BlockSpec((N_EXP, D), lambda b, i: (0, 0)),
        pl.BlockSpec((N_EXP, LANES), lambda b, i: (0, 0)),
        pl.BlockSpec((TMR, TMR), lambda b, i: (0, 0)),
    ]


def _mix0_body(of_ref, ob_ref, z_ref, fn_ref, x_ref, pos_ref, wdn_ref, wfn_ref, on_ref, pv_ref,
               rwt_ref, rb_ref, upper_ref, x_out_ref, h2_ref, mi_ref, mf_ref, cnt_ref, carry_ref):
    o = of_ref[0].astype(F32) + ob_ref[0].astype(F32)
    z = z_ref[0].astype(F32)
    parts = []
    for h in range(HEADS):
        cs = slice(h * DV, (h + 1) * DV)
        parts.append(_rms(o[:, cs]) * on_ref[0:1, cs] * _silu(z[:, cs]))
    dn = jnp.concatenate(parts, axis=1).astype(BF16)
    mix = _dot(dn, wdn_ref[...]) + _dot(fn_ref[0], wfn_ref[...])
    x_new = x_ref[0] + pos_ref[...] + pv_ref[0, 0:1, :] * mix
    _route_tail(x_new, pv_ref, rwt_ref, rb_ref, upper_ref, x_out_ref, h2_ref, mi_ref, mf_ref, cnt_ref,
                carry_ref)


def _mix0(o_f, o_b, z, fn, x, pos, wdn, wfn, onorm8, pvec, rwt, rb, upper):
    bsz, seq, d = x.shape
    nt = seq // TMR
    tok = lambda w: pl.BlockSpec((1, TMR, w), lambda b, i: (b, i, 0))
    cst = lambda shape: pl.BlockSpec(shape, lambda b, i: (0,) * len(shape))
    return pl.pallas_call(
        _mix0_body,
        out_shape=_route_out_shapes(bsz, seq),
        grid=(bsz, nt),
        in_specs=[tok(V_W), tok(V_W), tok(Z_W), tok(FN_W), tok(d),
                  pl.BlockSpec((TMR, d), lambda b, i: (i, 0)),
                  cst((V_W, d)), cst((FN_W, d)), cst((8, V_W)),
                  pl.BlockSpec((1, 8, d), lambda b, i: (b, 0, 0))] + _route_in_specs(),
        out_specs=_route_out_specs(nt),
        scratch_shapes=[pltpu.VMEM((N_EXP, LANES), F32)],
        compiler_params=_cparams(("arbitrary", "arbitrary")),
        name="mix0_router",
    )(o_f, o_b, z, fn, x, pos, wdn, wfn, onorm8, pvec, rwt, rb, upper)


TD = 512


def _dispatch_body(pad_ref, dest_ref, h2_ref, xs_ref, zero_ref, sem):
    ntiles = xs_ref.shape[0] // TME

    @pl.when(pl.program_id(0) == 0)
    def _():
        zero_ref[...] = jnp.zeros_like(zero_ref)

        def zero_tile(k):
            return pltpu.make_async_copy(zero_ref, xs_ref.at[pl.ds(pl.multiple_of(k * TME, TME), TME)], sem)

        def fill_tail(k, carry):
            zero_tile(k).start()
            return carry

        def drain_tail(k, carry):
            zero_tile(k).wait()
            return carry

        for e in range(N_EXP):
            @pl.when(pad_ref[e] >= 0)
            def _():
                zero_tile(pad_ref[e]).start()
        lax.fori_loop(pad_ref[N_EXP], ntiles, fill_tail, 0)
        for e in range(N_EXP):
            @pl.when(pad_ref[e] >= 0)
            def _():
                zero_tile(pad_ref[e]).wait()
        lax.fori_loop(pad_ref[N_EXP], ntiles, drain_tail, 0)

    for t in range(TD):
        for sl in range(2):
            dst = dest_ref[0, 0, sl * TD + t]
            pltpu.make_async_copy(h2_ref.at[pl.ds(t, 1)], xs_ref.at[pl.ds(dst, 1)], sem).start(priority=sl)
    for sl in range(2):
        pltpu.make_async_copy(h2_ref, xs_ref.at[pl.ds(0, TD)], sem).wait()


def _dispatch(pad_start, dest_tiles, h2, rows):
    t, d = h2.shape
    return pl.pallas_call(
        _dispatch_body,
        out_shape=jax.ShapeDtypeStruct((rows, d), h2.dtype),
        grid_spec=pltpu.PrefetchScalarGridSpec(
            num_scalar_prefetch=1,
            grid=(t // TD,),
            in_specs=[
                pl.BlockSpec((1, 1, 2 * TD), lambda i, pad: (i, 0, 0), memory_space=pltpu.SMEM),
                pl.BlockSpec((TD, d), lambda i, pad: (i, 0)),
            ],
            out_specs=pl.BlockSpec(memory_space=pl.ANY),
            scratch_shapes=[pltpu.VMEM((TME, d), h2.dtype), pltpu.SemaphoreType.DMA],
        ),
        compiler_params=_cparams(("arbitrary",)),
        name="moe_dispatch",
    )(pad_start, dest_tiles, h2)


def _expert_body(te_ref, nu_ref, xs_ref, wg_ref, wu_ref, wd_ref, ys_ref, wgb_ref, wub_ref, wdb_ref):
    i = pl.program_id(0)
    used = i < nu_ref[0]
    new_expert = jnp.logical_or(i == 0, te_ref[i] != te_ref[jnp.maximum(i - 1, 0)])

    @pl.when(jnp.logical_and(used, new_expert))
    def _():
        wgb_ref[...] = wg_ref[0, 0].astype(BF16)
        wub_ref[...] = wu_ref[0, 0].astype(BF16)
        wdb_ref[...] = wd_ref[0, 0].astype(BF16)

    @pl.when(used)
    def _():
        nblk = 2
        rb = TME // nblk
        xb = [_unpack_pairs(xs_ref[k * rb:(k + 1) * rb, :]).astype(BF16) for k in range(nblk)]
        hg = [_dot(v, wgb_ref[...]) for v in xb]
        hu = [_dot(v, wub_ref[...]) for v in xb]
        hid = [(_silu(g) * u).astype(BF16) for g, u in zip(hg, hu)]
        for k in range(nblk):
            ys_ref[k * rb:(k + 1) * rb, :] = _pack_pairs(_dot(hid[k], wdb_ref[...]))

    @pl.when(jnp.logical_not(used))
    def _():
        ys_ref[...] = jnp.zeros_like(ys_ref)


def _experts(layer, tile_expert, n_used, xs, rows, w_gate, w_up, w_down):
    dp = xs.shape[1]
    d = 2 * dp
    ntiles = rows // TME
    return pl.pallas_call(
        _expert_body,
        out_shape=jax.ShapeDtypeStruct((rows, dp), xs.dtype),
        grid_spec=pltpu.PrefetchScalarGridSpec(
            num_scalar_prefetch=2,
            grid=(ntiles,),
            in_specs=[
                pl.BlockSpec((TME, dp), lambda i, te, nu: (jnp.minimum(i, nu[0] - 1), 0)),
                pl.BlockSpec((1, 1, d, FF), lambda i, te, nu: (layer, te[i], 0, 0)),
                pl.BlockSpec((1, 1, d, FF), lambda i, te, nu: (layer, te[i], 0, 0)),
                pl.BlockSpec((1, 1, FF, d), lambda i, te, nu: (layer, te[i], 0, 0)),
            ],
            out_specs=pl.BlockSpec((TME, dp), lambda i, te, nu: (i, 0)),
            scratch_shapes=[pltpu.VMEM((d, FF), BF16), pltpu.VMEM((d, FF), BF16), pltpu.VMEM((FF, d), BF16)],
        ),
        compiler_params=_cparams(("arbitrary",)),
        name="moe_experts",
    )(tile_expert, n_used, xs, w_gate, w_up, w_down)


def _gather_pairs(dest_ref, next_ref, ys_ref, buf_ref, sem):
    step = pl.program_id(0) * pl.num_programs(1) + pl.program_id(1)
    nsteps = pl.num_programs(0) * pl.num_programs(1)
    cur = lax.rem(step, 2)

    def start_tile(idx_ref, b):
        for t in range(TMG):
            for sl in range(2):
                src = idx_ref[0, 0, sl * TMG + t]
                pltpu.make_async_copy(ys_ref.at[pl.ds(src, 1)], buf_ref.at[b, sl, pl.ds(t, 1)],
                                      sem.at[b]).start(priority=sl)

    @pl.when(step == 0)
    def _():
        start_tile(dest_ref, 0)

    for b in range(2):
        @pl.when(jnp.logical_and(step + 1 < nsteps, cur == 1 - b))
        def _():
            start_tile(next_ref, b)

    for sl in range(2):
        pltpu.make_async_copy(ys_ref.at[pl.ds(0, TMG)], buf_ref.at[cur, sl], sem.at[cur]).wait()
    return buf_ref.at[cur]


def _moe_combine(wc_ref, rows_ref):
    return wc_ref[:, 0:1] * _unpack_pairs(rows_ref[0]) + wc_ref[:, 1:2] * _unpack_pairs(rows_ref[1])


def _gather_specs(nt, ntiles):
    return [
        pl.BlockSpec((1, 1, 2 * TMG), lambda b, i: (b * nt + i, 0, 0), memory_space=pltpu.SMEM),
        pl.BlockSpec((1, 1, 2 * TMG), lambda b, i: (jnp.minimum(b * nt + i + 1, ntiles - 1), 0, 0),
                     memory_space=pltpu.SMEM),
    ]


GATHER_SCRATCH = [pltpu.VMEM((2, 2, TMG, D // 2), U32), pltpu.SemaphoreType.DMA((2,))]


def _routing_plan(meta_i, counts, tile_tokens):
    t = meta_i.shape[1]
    cnt_perm = counts[:, 0].astype(I32)
    e = jnp.arange(N_EXP)
    cnt = cnt_perm[(e % EXP_PER_GROUP) * N_GROUPS + e // EXP_PER_GROUP]
    tiles_e = (cnt + TME - 1) // TME
    cum_tiles = jnp.cumsum(tiles_e)
    offs = (cum_tiles - tiles_e) * TME
    sel = meta_i[0:2][..., None] == e
    dest = jnp.sum(jnp.where(sel, offs, 0), axis=-1) + meta_i[2:4]
    ntiles = (2 * t) // TME + N_EXP
    n_used = cum_tiles[-1]
    te = jnp.sum(cum_tiles[None, :] <= jnp.arange(ntiles, dtype=I32)[:, None], axis=-1).astype(I32)
    last_e = jnp.max(jnp.where(tiles_e > 0, e, 0)).astype(I32)
    te = jnp.minimum(te, last_e)

    def tiles(n):
        return dest.reshape(2, t // n, n).transpose(1, 0, 2).reshape(t // n, 1, 2 * n)

    pad_start = jnp.concatenate([jnp.where(tiles_e > 0, cum_tiles - 1, -1), n_used[None]]).astype(I32)
    return pad_start, tiles(TD), tiles(tile_tokens), te, n_used.reshape(1).astype(I32), ntiles * TME


def _moe(layer, h2, meta_i, counts, w_gate, w_up, w_down):
    pad_start, dest_td, dest_tm, te, n_used, rows = _routing_plan(meta_i, counts, TMG)
    xs = _dispatch(pad_start, dest_td, h2, rows)
    ys = _experts(layer, te, n_used, xs, rows, w_gate, w_up, w_down)
    return ys, dest_tm


def _conf_in_body(x_ref, dest_ref, next_ref, wc_ref, ys_ref, pv_ref, w1_ref, b1_ref, x_out_ref, u_ref,
                  buf_ref, sem):
    rows_ref = _gather_pairs(dest_ref, next_ref, ys_ref, buf_ref, sem)
    x2 = x_ref[0] + pv_ref[0, 0:1, :] * _moe_combine(wc_ref, rows_ref)
    x_out_ref[0] = x2
    h = (_rms(x2) * pv_ref[0, 1:2, :] + pv_ref[0, 2:3, :]).astype(BF16)
    u = _dot(h, w1_ref[...]) + b1_ref[0:1, :]
    u_ref[0] = (u[:, :D] * _sigmoid(u[:, D:])).astype(BF16)


def _conf_in(x1, dest_tm, wcol, ys, pvec, w1, b1):
    bsz, seq, d = x1.shape
    nt = seq // TMG
    return pl.pallas_call(
        _conf_in_body,
        out_shape=(jax.ShapeDtypeStruct((bsz, seq, d), F32), jax.ShapeDtypeStruct((bsz, seq, d), BF16)),
        grid=(bsz, nt),
        in_specs=[pl.BlockSpec((1, TMG, d), lambda b, i: (b, i, 0))] + _gather_specs(nt, bsz * nt) + [
            pl.BlockSpec((TMG, 8), lambda b, i: (b * nt + i, 0)),
            pl.BlockSpec(memory_space=pl.ANY),
            pl.BlockSpec((1, 8, d), lambda b, i: (b, 0, 0)),
            pl.BlockSpec((d, 2 * d), lambda b, i: (0, 0)),
            pl.BlockSpec((8, 2 * d), lambda b, i: (0, 0)),
        ],
        out_specs=(pl.BlockSpec((1, TMG, d), lambda b, i: (b, i, 0)),
                   pl.BlockSpec((1, TMG, d), lambda b, i: (b, i, 0))),
        scratch_shapes=GATHER_SCRATCH,
        compiler_params=_cparams(("arbitrary", "arbitrary")),
        name="combine_conf_in",
    )(x1, dest_tm, dest_tm, wcol, ys, pvec, w1, b1)


def _conf_mix_body(nt, cur_ref, prev_ref, next_ref, dw_ref, cv_ref, x_ref, w2_ref, pv_ref, rwt_ref,
                   rb_ref, upper_ref, x_out_ref, h2_ref, mi_ref, mf_ref, cnt_ref, ext_ref, sh_ref,
                   conv_ref, carry_ref):
    i = pl.program_id(1)
    ext_ref[0:HALO, :] = jnp.where(i == 0, 0.0, prev_ref[0].astype(F32))
    ext_ref[HALO:HALO + TMR, :] = cur_ref[0].astype(F32)
    ext_ref[HALO + TMR:2 * HALO + TMR, :] = jnp.where(i == nt - 1, 0.0, next_ref[0].astype(F32))
    pad = CONF_K // 2
    span = TMR + CONF_SPAN
    for cb in range(D // LANES):
        cs = slice(cb * LANES, (cb + 1) * LANES)
        for r in range(1, 8):
            sh_ref[r - 1, :, cs] = ext_ref[r:r + span, cs]
        for rb in range(0, TMR, CONF_ROWS):
            acc = None
            for j in range(CONF_K):
                lo = HALO - pad + j
                r, base = lo % 8, lo - lo % 8 + rb
                src = ext_ref[base:base + CONF_ROWS, cs] if r == 0 else sh_ref[r - 1, base:base + CONF_ROWS, cs]
                term = src * dw_ref[j:j + 1, cs]
                acc = term if acc is None else acc + term
            conv_ref[rb:rb + CONF_ROWS, cs] = acc + cv_ref[0:1, cs]
    u = conv_ref[...]
    mu = jnp.mean(u, axis=-1, keepdims=True)
    uc = u - mu
    var = jnp.mean(uc * uc, axis=-1, keepdims=True)
    y = uc * lax.rsqrt(var + EPS) * cv_ref[1:2, :] + cv_ref[2:3, :]
    mix = _dot(_silu(y).astype(BF16), w2_ref[...]) + cv_ref[3:4, :]
    x_new = x_ref[0] + pv_ref[0, 0:1, :] * mix
    _route_tail(x_new, pv_ref, rwt_ref, rb_ref, upper_ref, x_out_ref, h2_ref, mi_ref, mf_ref, cnt_ref,
                carry_ref)


def _conf_mix(u, dw32, cvec, x2, w2, pvec, rwt, rb, upper):
    bsz, seq, d = x2.shape
    nt = seq // TMR
    r = TMR // HALO
    cst = lambda shape: pl.BlockSpec(shape, lambda b, i: (0,) * len(shape))
    return pl.pallas_call(
        functools.partial(_conf_mix_body, nt),
        out_shape=_route_out_shapes(bsz, seq),
        grid=(bsz, nt),
        in_specs=[
            pl.BlockSpec((1, TMR, d), lambda b, i: (b, i, 0)),
            pl.BlockSpec((1, HALO, d), lambda b, i: (b, jnp.maximum(i * r - 1, 0), 0)),
            pl.BlockSpec((1, HALO, d), lambda b, i: (b, jnp.minimum((i + 1) * r, nt * r - 1), 0)),
            cst((32, d)), cst((8, d)),
            pl.BlockSpec((1, TMR, d), lambda b, i: (b, i, 0)),
            cst((d, d)),
            pl.BlockSpec((1, 8, d), lambda b, i: (b, 0, 0)),
        ] + _route_in_specs(),
        out_specs=_route_out_specs(nt),
        scratch_shapes=[pltpu.VMEM((TMR + 2 * HALO, d), F32), pltpu.VMEM((7, TMR + CONF_SPAN, d), F32),
                        pltpu.VMEM((TMR, d), F32),
                        pltpu.VMEM((N_EXP, LANES), F32)],
        compiler_params=_cparams(("arbitrary", "arbitrary")),
        name="conf_mix_router",
    )(u, u, u, dw32, cvec, x2, w2, pvec, rwt, rb, upper)


def _final_body(x_ref, dest_ref, next_ref, wc_ref, ys_ref, pv_ref, o_ref, buf_ref, sem):
    rows_ref = _gather_pairs(dest_ref, next_ref, ys_ref, buf_ref, sem)
    x4 = x_ref[0] + pv_ref[0, 0:1, :] * _moe_combine(wc_ref, rows_ref)
    o_ref[0] = _rms(x4) * pv_ref[0, 1:2, :]


def _final(x3, dest_tm, wcol, ys, pvec):
    bsz, seq, d = x3.shape
    nt = seq // TMG
    return pl.pallas_call(
        _final_body,
        out_shape=jax.ShapeDtypeStruct((bsz, seq, d), F32),
        grid=(bsz, nt),
        in_specs=[pl.BlockSpec((1, TMG, d), lambda b, i: (b, i, 0))] + _gather_specs(nt, bsz * nt) + [
            pl.BlockSpec((TMG, 8), lambda b, i: (b * nt + i, 0)),
            pl.BlockSpec(memory_space=pl.ANY),
            pl.BlockSpec((1, 8, d), lambda b, i: (b, 0, 0)),
        ],
        out_specs=pl.BlockSpec((1, TMG, d), lambda b, i: (b, i, 0)),
        scratch_shapes=GATHER_SCRATCH,
        compiler_params=_cparams(("arbitrary", "arbitrary")),
        name="combine_final",
    )(x3, dest_tm, dest_tm, wcol, ys, pvec)


def _grid_sincos(rows, cols, dim):
    quarter = dim // 4
    omega = 1.0 / jnp.power(POS_BASE, jnp.arange(quarter, dtype=F32) / quarter)

    def axis_emb(n):
        ang = jnp.arange(n, dtype=F32)[:, None] * omega[None, :]
        return jnp.concatenate([jnp.sin(ang), jnp.cos(ang)], axis=-1)

    er = jnp.broadcast_to(axis_emb(rows)[:, None, :], (rows, cols, dim // 2))
    ec = jnp.broadcast_to(axis_emb(cols)[None, :, :], (rows, cols, dim // 2))
    return jnp.concatenate([er, ec], axis=-1).reshape(rows * cols, dim)


def _pad_rows(a, rows):
    return jnp.pad(a, ((0, rows - a.shape[0]),) + ((0, 0),) * (a.ndim - 1))


def _per_batch(*vecs):
    st = jnp.stack(vecs, axis=1)
    return jnp.pad(st, ((0, 0), (0, 8 - st.shape[1]), (0, 0)))


def _width_dft():
    k = np.arange(LANES)
    ang = 2 * np.pi * np.outer(k, k) / LANES
    groups = FN_W // LANES
    c = np.kron(np.eye(groups), np.cos(ang)) / math.sqrt(LANES)
    s = np.kron(np.eye(groups), np.sin(ang)) / math.sqrt(LANES)
    return jnp.asarray(np.concatenate([c, s], axis=1), dtype=BF16)


def kernel(x, c, ctx, c_ctx, ada_w, ada_b, norm1_g, norm2_g, hyb_w_in, dn_conv_w, dn_a_log, dn_dt_bias,
           dn_onorm_g, hyb_w_out, conf_w1, conf_b1, conf_dw_w, conf_dw_b, conf_ln_g, conf_ln_b, conf_w2,
           conf_b2, router_w, router_bias, moe_w_gate, moe_w_up, moe_w_down, final_g):
    bsz, seq, d = x.shape
    ctx_len = ctx.shape[1]
    assert d == D and seq % max(TM, TMR, TMG) == 0 and ctx_len % TM == 0 and (bsz * seq) % TD == 0
    nct = ctx_len // TM
    t = bsz * seq

    pos = _grid_sincos(seq // GRID_W, GRID_W, d)

    cpad = _pad_rows(jnp.concatenate([c, c_ctx[None, :]], axis=0), 16)
    mods = _adaln(cpad, ada_w, ada_b)
    m = lambda l, k: mods[l, :, k * d:(k + 1) * d]
    a1 = norm1_g[0][None, :] * (1.0 + m(0, 1))
    mod_a = jnp.stack([jnp.broadcast_to(a1[bsz], (bsz, d)), a1[:bsz]], axis=1)
    mod_b = jnp.stack([jnp.broadcast_to(m(0, 0)[bsz], (bsz, d)), m(0, 0)[:bsz]], axis=1)

    w_in = hyb_w_in[0]
    z_lo = QKV_W + AB_W
    f_lo = z_lo + Z_W
    wqkv = w_in[:, :QKV_W].astype(BF16)
    wab = jnp.pad(w_in[:, QKV_W:z_lo], ((0, 0), (0, LANES - AB_W))).astype(BF16)
    wz = w_in[:, z_lo:f_lo].astype(BF16)
    wf = w_in[:, f_lo:].astype(BF16)
    gpar = jnp.zeros((8, LANES), F32)
    gpar = gpar.at[0, :2 * HEADS].set(-jnp.exp(dn_a_log[0].reshape(-1)))
    gpar = gpar.at[1, :2 * HEADS].set(dn_dt_bias[0].reshape(-1))
    qkv_pre, z, fc, fs, gates = _inproj(ctx, x, pos, mod_a, mod_b, wqkv, wz, wf, wab, _width_dft(), gpar)

    qkv = _dnconv(qkv_pre, _pad_rows(dn_conv_w[0], 8), nct)
    lc = qkv.shape[1]
    nchunks = lc // CHUNK
    grow = gates[:, :, :AB_W].reshape(bsz, nchunks, CHUNK, 4, HEADS)
    grow = grow.transpose(0, 1, 3, 4, 2).reshape(bsz, nchunks, 4, HB)
    grow = jnp.pad(grow, ((0, 0), (0, 0), (0, 4), (0, 0)))
    o_f, o_b = _deltanet(qkv, grow, seq, ctx_len // CHUNK)

    fn = _seq_dft(fc, fs)

    perm = (np.arange(N_EXP) % N_GROUPS) * EXP_PER_GROUP + np.arange(N_EXP) // N_GROUPS
    rwt = router_w.T[perm].astype(BF16)
    rb = jnp.broadcast_to(router_bias[perm][:, None], (N_EXP, LANES)).astype(F32)
    upper = jnp.asarray(np.triu(np.ones((TMR, TMR), np.float32), 1), dtype=BF16)
    w_out = hyb_w_out[0].astype(BF16)
    onorm8 = _pad_rows(jnp.tile(dn_onorm_g[0], HEADS)[None, :], 8)
    pv0 = _per_batch(m(0, 2)[:bsz], norm2_g[0][None, :] * (1.0 + m(0, 4)[:bsz]), m(0, 3)[:bsz])
    x1, h2, meta_i, meta_f, counts = _mix0(o_f, o_b, z, fn, x, pos, w_out[:V_W], w_out[V_W:], onorm8, pv0,
                                           rwt, rb, upper)
    ys, dest_tm = _moe(0, h2, meta_i, counts, moe_w_gate, moe_w_up, moe_w_down)
    wcol = meta_f.T

    pv1 = _per_batch(m(0, 5)[:bsz], norm1_g[1][None, :] * (1.0 + m(1, 1)[:bsz]), m(1, 0)[:bsz])
    x2, u = _conf_in(x1, dest_tm, wcol, ys, pv1, conf_w1[0].astype(BF16), _pad_rows(conf_b1[0][None, :], 8))
    cvec = _pad_rows(jnp.stack([conf_dw_b[0], conf_ln_g[0], conf_ln_b[0], conf_b2[0]]), 8)
    pv2 = _per_batch(m(1, 2)[:bsz], norm2_g[1][None, :] * (1.0 + m(1, 4)[:bsz]), m(1, 3)[:bsz])
    x3, h2, meta_i, meta_f, counts = _conf_mix(u, _pad_rows(conf_dw_w[0], 32), cvec, x2,
                                               conf_w2[0].astype(BF16), pv2, rwt, rb, upper)
    ys, dest_tm = _moe(1, h2, meta_i, counts, moe_w_gate, moe_w_up, moe_w_down)
    pv3 = _per_batch(m(1, 5)[:bsz], jnp.broadcast_to(final_g[None, :], (bsz, d)))
    return _final(x3, dest_tm, meta_f.T, ys, pv3)
```

```python
import functools
import math

import numpy as np
import jax
import jax.numpy as jnp
from jax import lax
from jax.experimental import pallas as pl
from jax.experimental.pallas import tpu as pltpu

F32 = jnp.float32
BF16 = jnp.bfloat16
I32 = jnp.int32
U32 = jnp.uint32

D = 1024
GRID_W = 64
HEADS = 4
DK = 128
DV = 128
CHUNK = 64
QK_W = HEADS * DK
V_W = HEADS * DV
QKV_W = 2 * QK_W + V_W
AB_W = 16
Z_W = V_W
FN_W = 512
DN_CONV_K = 5
CONF_K = 31
N_EXP = 32
N_GROUPS = 8
EXP_PER_GROUP = 4
FF = 512
EPS = 1e-6
POS_BASE = 10000.0

LANES = 128
HALO = 16
TM = 256
TMR = 512
TMG = 512
TME = 512
HB = HEADS * CHUNK
DN_STACK = 2
HS = DN_STACK * CHUNK
DN_GROUP = 8
CONF_SPAN = 24
CONF_ROWS = 64
VMEM_LIMIT = 56 * 1024 * 1024


def _cparams(sem):
    return pltpu.CompilerParams(dimension_semantics=sem, vmem_limit_bytes=VMEM_LIMIT)


def _sigmoid(v):
    return jax.nn.sigmoid(v)


def _silu(v):
    return v * jax.nn.sigmoid(v)


def _dot(a, b):
    return jnp.dot(a, b, preferred_element_type=F32)


def _dot_nt(a, b):
    return lax.dot_general(a, b, (((1,), (1,)), ((), ())), preferred_element_type=F32)


def _split3(a):
    a1 = a.astype(BF16)
    r1 = a - a1.astype(F32)
    a2 = r1.astype(BF16)
    a3 = (r1 - a2.astype(F32)).astype(BF16)
    return a1, a2, a3


def _pack_pairs(v):
    n = v.shape[1] // 2
    lo = lax.bitcast_convert_type(v[:, :n].astype(BF16).astype(F32), U32) >> 16
    hi = lax.bitcast_convert_type(v[:, n:].astype(BF16).astype(F32), U32) & jnp.uint32(0xFFFF0000)
    return hi | lo


def _unpack_pairs(p):
    lo = lax.bitcast_convert_type(p << 16, F32)
    hi = lax.bitcast_convert_type(p & jnp.uint32(0xFFFF0000), F32)
    return jnp.concatenate([lo, hi], axis=1)


def _rms(v, eps=EPS):
    return v * lax.rsqrt(jnp.mean(v * v, axis=-1, keepdims=True) + eps)


def _adaln_body(c_ref, w_ref, b_ref, o_ref):
    s = _silu(c_ref[...])
    o_ref[0] = jnp.dot(s, w_ref[0], preferred_element_type=F32,
                       precision=lax.Precision.HIGHEST) + b_ref[0]


def _adaln(cpad, ada_w, ada_b):
    depth, d, n = ada_w.shape
    tn = 1536
    rows = cpad.shape[0]
    return pl.pallas_call(
        _adaln_body,
        out_shape=jax.ShapeDtypeStruct((depth, rows, n), F32),
        grid=(depth, n // tn),
        in_specs=[
            pl.BlockSpec((rows, d), lambda l, j: (0, 0)),
            pl.BlockSpec((1, d, tn), lambda l, j: (l, 0, j)),
            pl.BlockSpec((1, 1, tn), lambda l, j: (l, 0, j)),
        ],
        out_specs=pl.BlockSpec((1, rows, tn), lambda l, j: (l, 0, j)),
        compiler_params=_cparams(("arbitrary", "arbitrary")),
        name="adaln",
    )(cpad, ada_w, ada_b.reshape(depth, 1, n))


def _inproj_body(nct, ctx_ref, x_ref, pos_ref, ma_ref, mb_ref, wqkv_ref, wz_ref, wf_ref, wab_ref,
                 dftw_ref, gpar_ref, qkv_ref, z_ref, fc_ref, fs_ref, gate_ref):
    i = pl.program_id(1)
    is_ctx = i < nct
    xin = jnp.where(is_ctx, ctx_ref[0], x_ref[0] + pos_ref[...])
    a = jnp.where(is_ctx, ma_ref[0, 0:1, :], ma_ref[0, 1:2, :])
    b = jnp.where(is_ctx, mb_ref[0, 0:1, :], mb_ref[0, 1:2, :])
    h = (_rms(xin) * a + b).astype(BF16)
    qkv_ref[0] = _dot(h, wqkv_ref[...]).astype(BF16)
    z_ref[0] = _dot(h, wz_ref[...]).astype(BF16)
    f = _dot(h, wf_ref[...]).astype(BF16)
    fcs = _dot(f, dftw_ref[...])
    fc_ref[0] = fcs[:, :FN_W].astype(BF16)
    fs_ref[0] = fcs[:, FN_W:].astype(BF16)
    ab = _dot(h, wab_ref[...])
    lane = lax.broadcasted_iota(I32, ab.shape, 1)
    pre = ab + gpar_ref[1:2, :]
    softplus = jnp.maximum(pre, 0.0) + jnp.log1p(jnp.exp(-jnp.abs(pre)))
    gate_ref[0] = jnp.where(lane < 2 * HEADS, gpar_ref[0:1, :] * softplus, _sigmoid(ab))


def _inproj(ctx, x, pos, mod_a, mod_b, wqkv, wz, wf, wab, dftw, gpar):
    bsz, seq, d = x.shape
    nct = ctx.shape[1] // TM
    nt = nct + seq // TM
    lc = nt * TM
    xi = lambda b, i: (b, jnp.maximum(i - nct, 0), 0)
    full = lambda shape: pl.BlockSpec(shape, lambda b, i: (0,) * len(shape))
    return pl.pallas_call(
        functools.partial(_inproj_body, nct),
        out_shape=(
            jax.ShapeDtypeStruct((bsz, lc, QKV_W), BF16),
            jax.ShapeDtypeStruct((bsz, seq, Z_W), BF16),
            jax.ShapeDtypeStruct((bsz, seq, FN_W), BF16),
            jax.ShapeDtypeStruct((bsz, seq, FN_W), BF16),
            jax.ShapeDtypeStruct((bsz, lc, LANES), F32),
        ),
        grid=(bsz, nt),
        in_specs=[
            pl.BlockSpec((1, TM, d), lambda b, i: (b, jnp.minimum(i, nct - 1), 0)),
            pl.BlockSpec((1, TM, d), xi),
            pl.BlockSpec((TM, d), lambda b, i: (jnp.maximum(i - nct, 0), 0)),
            pl.BlockSpec((1, 2, d), lambda b, i: (b, 0, 0)),
            pl.BlockSpec((1, 2, d), lambda b, i: (b, 0, 0)),
            full((d, QKV_W)), full((d, Z_W)), full((d, FN_W)), full((d, LANES)),
            full((FN_W, 2 * FN_W)), full((8, LANES)),
        ],
        out_specs=(
            pl.BlockSpec((1, TM, QKV_W), lambda b, i: (b, i, 0)),
            pl.BlockSpec((1, TM, Z_W), xi),
            pl.BlockSpec((1, TM, FN_W), xi),
            pl.BlockSpec((1, TM, FN_W), xi),
            pl.BlockSpec((1, TM, LANES), lambda b, i: (b, i, 0)),
        ),
        compiler_params=_cparams(("arbitrary", "arbitrary")),
        name="inproj",
    )(ctx, x, pos, mod_a, mod_b, wqkv, wz, wf, wab, dftw, gpar)


def _dnconv_body(nct, nt, cur_ref, prev_ref, next_ref, w_ref, o_ref, ext_ref):
    i = pl.program_id(1)
    first = jnp.logical_or(i == 0, i == nct)
    last = jnp.logical_or(i == nct - 1, i == nt - 1)
    ext_ref[0:HALO, :] = jnp.where(first, 0.0, prev_ref[0].astype(F32))
    ext_ref[HALO:HALO + TM, :] = cur_ref[0].astype(F32)
    ext_ref[HALO + TM:2 * HALO + TM, :] = jnp.where(last, 0.0, next_ref[0].astype(F32))
    pad = DN_CONV_K // 2
    for cb in range(QKV_W // LANES):
        cs = slice(cb * LANES, (cb + 1) * LANES)
        acc = None
        for j in range(DN_CONV_K):
            lo = HALO - pad + j
            term = ext_ref[lo:lo + TM, cs] * w_ref[j:j + 1, cs]
            acc = term if acc is None else acc + term
        y = _silu(acc)
        if cb < 2 * HEADS:
            y = y * lax.rsqrt(jnp.sum(y * y, axis=-1, keepdims=True) + EPS)
        o_ref[0, :, cs] = y.astype(BF16)


def _dnconv(qkv_pre, conv_w8, nct):
    bsz, lc, w = qkv_pre.shape
    nt = lc // TM
    r = TM // HALO
    return pl.pallas_call(
        functools.partial(_dnconv_body, nct, nt),
        out_shape=jax.ShapeDtypeStruct((bsz, lc, w), BF16),
        grid=(bsz, nt),
        in_specs=[
            pl.BlockSpec((1, TM, w), lambda b, i: (b, i, 0)),
            pl.BlockSpec((1, HALO, w), lambda b, i: (b, jnp.maximum(i * r - 1, 0), 0)),
            pl.BlockSpec((1, HALO, w), lambda b, i: (b, jnp.minimum((i + 1) * r, nt * r - 1), 0)),
            pl.BlockSpec((8, w), lambda b, i: (0, 0)),
        ],
        out_specs=pl.BlockSpec((1, TM, w), lambda b, i: (b, i, 0)),
        scratch_shapes=[pltpu.VMEM((TM + 2 * HALO, w), F32)],
        compiler_params=_cparams(("arbitrary", "arbitrary")),
        name="dnconv",
    )(qkv_pre, qkv_pre, qkv_pre, conv_w8)


def _dn_setup(d, st, qkv, grow, mask_incl, mask_strict, trit, bdones, eye, lvl0):
    neg_inf = float("-inf")
    heads = range(st * DN_STACK, (st + 1) * DN_STACK)
    stack = lambda off: jnp.concatenate([qkv[:, off + h * DK: off + (h + 1) * DK] for h in heads], axis=0)
    q_s = stack(0)
    k_s = stack(QK_W)
    v_s = stack(2 * QK_W)
    grow = grow[:, st * HS:(st + 1) * HS]
    g3 = _split3(grow)
    sums = sum(_dot(t, jnp.concatenate([trit, bdones], axis=1)) for t in g3)
    cg_row = sums[:, :HS]
    tot_row = sums[:, HS:]
    rows = jnp.concatenate([cg_row, tot_row, grow, jnp.zeros((LANES - 24, HS), F32)], axis=0)
    cols = rows.T
    cgc = cols[:, d:d + 1]
    totc = cols[:, 8 + d:9 + d]
    betac = cols[:, 18 + d:19 + d]
    cgr = cg_row[d:d + 1, :]
    dmat = jnp.exp(jnp.where(mask_incl > 0.5, cgc - cgr, neg_inf))
    kk = _dot_nt(k_s, k_s)
    qk = _dot_nt(q_s, k_s)
    a = jnp.where(mask_strict > 0.5, kk * dmat, 0.0) * betac
    ecg = jnp.exp(cgc)
    k32 = k_s.astype(F32)
    scale = DK ** -0.5
    return dict(
        d=d, st=st, a=a, x=eye.astype(F32) - a * lvl0, totc=totc,
        rhs=jnp.concatenate([(betac * v_s.astype(F32)).astype(BF16),
                             ((betac * ecg) * k32).astype(BF16)], axis=1),
        ktail=(k32 * jnp.exp(totc - cgc)).astype(BF16),
        qdec=(q_s.astype(F32) * (ecg * scale)).astype(BF16),
        qkd=(qk * dmat * scale).astype(BF16))


def _dn_body(qf_ref, qb_ref, gf_ref, gb_ref, mli_ref, mls_ref, mui_ref, mus_ref, bdl_ref, bdu_ref,
             bdo_ref, eye_ref, lvl_ref, of_ref, ob_ref, s_ref):
    @pl.when(pl.program_id(1) == 0)
    def _():
        s_ref[...] = jnp.zeros_like(s_ref)

    bdl = bdl_ref[...]
    bdu = bdu_ref[...]
    bdo = bdo_ref[...]
    eye = eye_ref[...]
    chains = []
    for g in range(qf_ref.shape[0]):
        for st in range(HEADS // DN_STACK):
            c = _dn_setup(0, st, qf_ref[g], gf_ref[g, 0], mli_ref[...], mls_ref[...], bdu, bdo, eye,
                          lvl_ref[0])
            c.update(g=g, out=of_ref)
            chains.append(c)
            c = _dn_setup(1, st, qb_ref[g], gb_ref[g, 0], mui_ref[...], mus_ref[...], bdl, bdo, eye,
                          lvl_ref[0])
            c.update(g=g, out=ob_ref)
            chains.append(c)
    for lvl in range(1, int(math.log2(CHUNK))):
        for c in chains:
            c["xb"] = c["x"].astype(BF16)
            c["y"] = _dot((c["a"] * lvl_ref[lvl]).astype(BF16), c["xb"]).astype(BF16)
        for c in chains:
            c["x"] = c["x"] - _dot(c["xb"], c["y"])
    for c in chains:
        uw = _dot(c["x"].astype(BF16), c["rhs"])
        c["u"] = uw[:, :DV]
        c["w"] = uw[:, DV:].astype(BF16)
    for c in chains:
        c["vn"] = []
        c["oi"] = []
    for hh in range(DN_STACK):
        rs = slice(hh * CHUNK, (hh + 1) * CHUNK)
        for c in chains:
            s = s_ref[c["g"], c["d"], c["st"] * DN_STACK + hh]
            sb = s.astype(BF16)
            c["s"] = s
            c["vn"].append(c["u"][rs] - _dot(c["w"][rs], sb))
            c["oi"].append(_dot(c["qdec"][rs], sb))
        for c in chains:
            cd = jnp.exp(c["totc"][hh * CHUNK:hh * CHUNK + 1, :])
            upd = lax.dot_general(c["ktail"][rs], c["vn"][hh].astype(BF16), (((0,), (0,)), ((), ())),
                                  preferred_element_type=F32)
            s_ref[c["g"], c["d"], c["st"] * DN_STACK + hh] = c["s"] * cd + upd
    for c in chains:
        vn_s = jnp.concatenate(c["vn"], axis=0).astype(BF16)
        o_s = jnp.concatenate(c["oi"], axis=0) + _dot(c["qkd"], vn_s)
        for hh in range(DN_STACK):
            h = c["st"] * DN_STACK + hh
            c["out"][c["g"], :, h * DV:(h + 1) * DV] = o_s[hh * CHUNK:(hh + 1) * CHUNK].astype(BF16)


def _dn_consts():
    r = np.arange(HS)
    same = (r[:, None] // CHUNK) == (r[None, :] // CHUNK)
    li = same & (r[:, None] >= r[None, :])
    ls = same & (r[:, None] > r[None, :])
    ui = same & (r[:, None] <= r[None, :])
    us = same & (r[:, None] < r[None, :])
    f = lambda m: jnp.asarray(m.astype(np.float32))
    h = lambda m: jnp.asarray(m.astype(np.float32), dtype=BF16)
    nlev = int(math.log2(CHUNK))
    lv = [((r[:, None] >> (l + 1)) == (r[None, :] >> (l + 1))) & ((r[:, None] >> l) != (r[None, :] >> l))
          for l in range(nlev)]
    return (f(li), f(ls), f(ui), f(us), h(li), h(ui), h(same), h(np.eye(HS, dtype=bool)),
            jnp.asarray(np.stack(lv).astype(np.float32)))


def _deltanet(qkv, grow, seq, ncc):
    bsz, lc, w = qkv.shape
    nchunks = lc // CHUNK
    nx = seq // CHUNK
    fwd = lambda s: s
    bwd = lambda s: jnp.where(s < ncc, ncc - 1 - s, ncc + nchunks - 1 - s)
    ofi = lambda b, s: (b, jnp.maximum(s - ncc, 0), 0)
    obi = lambda b, s: (b, jnp.minimum(nchunks - 1 - s, nx - 1), 0)
    cst = lambda shape: pl.BlockSpec(shape, lambda b, s: (0, 0))
    consts = _dn_consts()
    grp = math.gcd(bsz, DN_GROUP)
    return pl.pallas_call(
        _dn_body,
        out_shape=(jax.ShapeDtypeStruct((bsz, seq, V_W), BF16),
                   jax.ShapeDtypeStruct((bsz, seq, V_W), BF16)),
        grid=(bsz // grp, nchunks),
        in_specs=[
            pl.BlockSpec((grp, CHUNK, w), lambda b, s: (b, fwd(s), 0)),
            pl.BlockSpec((grp, CHUNK, w), lambda b, s: (b, bwd(s), 0)),
            pl.BlockSpec((grp, 1, 8, HB), lambda b, s: (b, fwd(s), 0, 0)),
            pl.BlockSpec((grp, 1, 8, HB), lambda b, s: (b, bwd(s), 0, 0)),
        ] + [cst((HS, HS))] * 8 + [pl.BlockSpec((int(math.log2(CHUNK)), HS, HS), lambda b, s: (0, 0, 0))],
        out_specs=(pl.BlockSpec((grp, CHUNK, V_W), ofi), pl.BlockSpec((grp, CHUNK, V_W), obi)),
        scratch_shapes=[pltpu.VMEM((grp, 2, HEADS, DK, DV), F32)],
        compiler_params=_cparams(("arbitrary", "arbitrary")),
        name="deltanet",
    )(qkv, qkv, grow, grow, *consts)


def _fft1_body(ns2, fc_ref, fs_ref, m_ref, cos_ref, sin_ref, y_ref):
    n1 = fc_ref.shape[1]
    xin = jnp.concatenate([fc_ref[0], fs_ref[0]], axis=0)
    y = _dot(m_ref[...], xin)
    yr, yi = y[:n1], y[n1:]
    for q in range(ns2):
        cs = slice(q * FN_W, (q + 1) * FN_W)
        c = jnp.concatenate([cos_ref[q]] * (FN_W // LANES), axis=1)
        s = jnp.concatenate([sin_ref[q]] * (FN_W // LANES), axis=1)
        y_ref[0, 0, :, cs] = (yr[:, cs] * c + yi[:, cs] * s).astype(BF16)
        y_ref[0, 1, :, cs] = (yi[:, cs] * c - yr[:, cs] * s).astype(BF16)


def _fft2_body(ns1, y_ref, cs_ref, o_ref):
    for q in range(ns1):
        rhs = jnp.concatenate([y_ref[0, 0, q], y_ref[0, 1, q]], axis=0)
        o_ref[0, :, q * FN_W:(q + 1) * FN_W] = _dot(cs_ref[...], rhs).astype(BF16)


def _seq_dft(fc, fs):
    bsz, seq, w = fc.shape
    n2 = GRID_W
    n1 = seq // n2
    k1 = np.arange(n1)
    ang1 = 2 * np.pi * np.outer(k1, k1) / n1
    c1, s1 = np.cos(ang1), np.sin(ang1)
    m1 = jnp.asarray(np.block([[c1, -s1], [-s1, -c1]]), dtype=BF16)
    tw = 2 * np.pi * np.outer(np.arange(n2), k1) / seq
    cos_t = jnp.asarray(np.repeat(np.cos(tw)[:, :, None], LANES, axis=2), dtype=F32)
    sin_t = jnp.asarray(np.repeat(np.sin(tw)[:, :, None], LANES, axis=2), dtype=F32)
    k2 = np.arange(n2)
    ang2 = 2 * np.pi * np.outer(k2, k2) / n2
    cs2 = jnp.asarray(np.concatenate([np.cos(ang2), np.sin(ang2)], axis=1) / math.sqrt(seq), dtype=BF16)

    ns2 = 16
    y = pl.pallas_call(
        functools.partial(_fft1_body, ns2),
        out_shape=jax.ShapeDtypeStruct((bsz, 2, n1, n2 * w), BF16),
        grid=(bsz, n2 // ns2),
        in_specs=[
            pl.BlockSpec((1, n1, ns2 * w), lambda b, j: (b, 0, j)),
            pl.BlockSpec((1, n1, ns2 * w), lambda b, j: (b, 0, j)),
            pl.BlockSpec((2 * n1, 2 * n1), lambda b, j: (0, 0)),
            pl.BlockSpec((ns2, n1, LANES), lambda b, j: (j, 0, 0)),
            pl.BlockSpec((ns2, n1, LANES), lambda b, j: (j, 0, 0)),
        ],
        out_specs=pl.BlockSpec((1, 2, n1, ns2 * w), lambda b, j: (b, 0, 0, j)),
        compiler_params=_cparams(("arbitrary", "arbitrary")),
        name="seqdft_stage1",
    )(fc.reshape(bsz, n1, n2 * w), fs.reshape(bsz, n1, n2 * w), m1, cos_t, sin_t)

    ns1 = min(32, n1)
    out = pl.pallas_call(
        functools.partial(_fft2_body, ns1),
        out_shape=jax.ShapeDtypeStruct((bsz, n2, n1 * w), BF16),
        grid=(bsz, n1 // ns1),
        in_specs=[
            pl.BlockSpec((1, 2, ns1, n2, w), lambda b, j: (b, 0, j, 0, 0)),
            pl.BlockSpec((n2, 2 * n2), lambda b, j: (0, 0)),
        ],
        out_specs=pl.BlockSpec((1, n2, ns1 * w), lambda b, j: (b, 0, j)),
        compiler_params=_cparams(("arbitrary", "arbitrary")),
        name="seqdft_stage2",
    )(y.reshape(bsz, 2, n1, n2, w), cs2)
    return out.reshape(bsz, seq, w)


def _route_tail(x_new, pv_ref, rwt_ref, rb_ref, upper_ref, x_out_ref, h2_ref, mi_ref, mf_ref, cnt_ref,
                carry_ref):
    first = jnp.logical_and(pl.program_id(0) == 0, pl.program_id(1) == 0)

    @pl.when(first)
    def _():
        carry_ref[...] = jnp.zeros_like(carry_ref)

    x_out_ref[0] = x_new
    h2 = _rms(x_new) * pv_ref[0, 1:2, :] + pv_ref[0, 2:3, :]
    h2_ref[...] = _pack_pairs(h2)
    logits = _dot_nt(rwt_ref[...], h2.astype(BF16))
    scores = _sigmoid(logits)
    sel = scores + rb_ref[:, 0:1]
    a = [sel[N_GROUPS * j:N_GROUPS * (j + 1)] for j in range(EXP_PER_GROUP)]
    sc = [scores[N_GROUPS * j:N_GROUPS * (j + 1)] for j in range(EXP_PER_GROUP)]
    gs = None
    for j0 in range(EXP_PER_GROUP):
        for j1 in range(j0 + 1, EXP_PER_GROUP):
            pair = a[j0] + a[j1]
            gs = pair if gs is None else jnp.maximum(gs, pair)
    gidx = lax.broadcasted_iota(I32, gs.shape, 0).astype(F32)
    gmax = jnp.max(gs, axis=0, keepdims=True)
    best = jnp.min(jnp.where(gs == gmax, gidx, float(N_GROUPS)), axis=0, keepdims=True)
    inb = gidx == best
    v = [jnp.sum(jnp.where(inb, a[j], 0.0), axis=0, keepdims=True) for j in range(EXP_PER_GROUP)]
    s = [jnp.sum(jnp.where(inb, sc[j], 0.0), axis=0, keepdims=True) for j in range(EXP_PER_GROUP)]
    v1, i1, s1 = v[0], jnp.zeros_like(v[0]), s[0]
    for j in range(1, EXP_PER_GROUP):
        better = v[j] > v1
        v1 = jnp.where(better, v[j], v1)
        i1 = jnp.where(better, float(j), i1)
        s1 = jnp.where(better, s[j], s1)
    v2 = jnp.full_like(v1, float("-inf"))
    i2 = jnp.zeros_like(v1)
    s2 = jnp.zeros_like(v1)
    for j in range(EXP_PER_GROUP):
        better = jnp.logical_and(i1 != float(j), v[j] > v2)
        v2 = jnp.where(better, v[j], v2)
        i2 = jnp.where(better, float(j), i2)
        s2 = jnp.where(better, s[j], s2)
    denom = s1 + s2
    rid = lax.broadcasted_iota(I32, logits.shape, 0).astype(F32)
    oh1 = rid == (i1 * N_GROUPS + best)
    oh2 = rid == (i2 * N_GROUPS + best)
    oh = jnp.where(jnp.logical_or(oh1, oh2), 1.0, 0.0)
    prefix = _dot(oh.astype(BF16), upper_ref[...])
    base = carry_ref[:, 0:1] + prefix
    rank1 = jnp.sum(jnp.where(oh1, base, 0.0), axis=0, keepdims=True)
    rank2 = jnp.sum(jnp.where(oh2, base, 0.0), axis=0, keepdims=True)
    carry_new = carry_ref[...] + jnp.sum(oh, axis=1, keepdims=True)
    carry_ref[...] = carry_new
    cnt_ref[...] = carry_new
    mi_ref[...] = jnp.zeros_like(mi_ref)
    mi_ref[0:1, :] = (best * EXP_PER_GROUP + i1).astype(I32)
    mi_ref[1:2, :] = (best * EXP_PER_GROUP + i2).astype(I32)
    mi_ref[2:3, :] = rank1.astype(I32)
    mi_ref[3:4, :] = rank2.astype(I32)
    mf_ref[...] = jnp.zeros_like(mf_ref)
    mf_ref[0:1, :] = s1 / denom
    mf_ref[1:2, :] = s2 / denom


def _route_out_shapes(bsz, seq):
    t = bsz * seq
    return (
        jax.ShapeDtypeStruct((bsz, seq, D), F32),
        jax.ShapeDtypeStruct((t, D // 2), U32),
        jax.ShapeDtypeStruct((8, t), I32),
        jax.ShapeDtypeStruct((8, t), F32),
        jax.ShapeDtypeStruct((N_EXP, LANES), F32),
    )


def _route_out_specs(nt):
    flat = lambda b, i: (b * nt + i, 0)
    return (
        pl.BlockSpec((1, TMR, D), lambda b, i: (b, i, 0)),
        pl.BlockSpec((TMR, D // 2), flat),
        pl.BlockSpec((8, TMR), lambda b, i: (0, b * nt + i)),
        pl.BlockSpec((8, TMR), lambda b, i: (0, b * nt + i)),
        pl.BlockSpec((N_EXP, LANES), lambda b, i: (0, 0)),
    )


def _route_in_specs():
    return [
        pl.BlockSpec((N_EXP, D), lambda b, i: (0, 0)),
        pl.BlockSpec((N_EXP, LANES), lambda b, i: (0, 0)),
        pl.BlockSpec((TMR, TMR), lambda b, i: (0, 0)),
    ]


def _mix0_body(of_ref, ob_ref, z_ref, fn_ref, x_ref, pos_ref, wdn_ref, wfn_ref, on_ref, pv_ref,
               rwt_ref, rb_ref, upper_ref, x_out_ref, h2_ref, mi_ref, mf_ref, cnt_ref, carry_ref):
    o = of_ref[0].astype(F32) + ob_ref[0].astype(F32)
    z = z_ref[0].astype(F32)
    parts = []
    for h in range(HEADS):
        cs = slice(h * DV, (h + 1) * DV)
        parts.append(_rms(o[:, cs]) * on_ref[0:1, cs] * _silu(z[:, cs]))
    dn = jnp.concatenate(parts, axis=1).astype(BF16)
    mix = _dot(dn, wdn_ref[...]) + _dot(fn_ref[0], wfn_ref[...])
    x_new = x_ref[0] + pos_ref[...] + pv_ref[0, 0:1, :] * mix
    _route_tail(x_new, pv_ref, rwt_ref, rb_ref, upper_ref, x_out_ref, h2_ref, mi_ref, mf_ref, cnt_ref,
                carry_ref)


def _mix0(o_f, o_b, z, fn, x, pos, wdn, wfn, onorm8, pvec, rwt, rb, upper):
    bsz, seq, d = x.shape
    nt = seq // TMR
    tok = lambda w: pl.BlockSpec((1, TMR, w), lambda b, i: (b, i, 0))
    cst = lambda shape: pl.BlockSpec(shape, lambda b, i: (0,) * len(shape))
    return pl.pallas_call(
        _mix0_body,
        out_shape=_route_out_shapes(bsz, seq),
        grid=(bsz, nt),
        in_specs=[tok(V_W), tok(V_W), tok(Z_W), tok(FN_W), tok(d),
                  pl.BlockSpec((TMR, d), lambda b, i: (i, 0)),
                  cst((V_W, d)), cst((FN_W, d)), cst((8, V_W)),
                  pl.BlockSpec((1, 8, d), lambda b, i: (b, 0, 0))] + _route_in_specs(),
        out_specs=_route_out_specs(nt),
        scratch_shapes=[pltpu.VMEM((N_EXP, LANES), F32)],
        compiler_params=_cparams(("arbitrary", "arbitrary")),
        name="mix0_router",
    )(o_f, o_b, z, fn, x, pos, wdn, wfn, onorm8, pvec, rwt, rb, upper)


TD = 512


def _dispatch_body(pad_ref, dest_ref, h2_ref, xs_ref, zero_ref, sem):
    ntiles = xs_ref.shape[0] // TME

    @pl.when(pl.program_id(0) == 0)
    def _():
        zero_ref[...] = jnp.zeros_like(zero_ref)

        def zero_tile(k):
            return pltpu.make_async_copy(zero_ref, xs_ref.at[pl.ds(pl.multiple_of(k * TME, TME), TME)], sem)

        def fill_tail(k, carry):
            zero_tile(k).start()
            return carry

        def drain_tail(k, carry):
            zero_tile(k).wait()
            return carry

        for e in range(N_EXP):
            @pl.when(pad_ref[e] >= 0)
            def _():
                zero_tile(pad_ref[e]).start()
        lax.fori_loop(pad_ref[N_EXP], ntiles, fill_tail, 0)
        for e in range(N_EXP):
            @pl.when(pad_ref[e] >= 0)
            def _():
                zero_tile(pad_ref[e]).wait()
        lax.fori_loop(pad_ref[N_EXP], ntiles, drain_tail, 0)

    for t in range(TD):
        for sl in range(2):
            dst = dest_ref[0, 0, sl * TD + t]
            pltpu.make_async_copy(h2_ref.at[pl.ds(t, 1)], xs_ref.at[pl.ds(dst, 1)], sem).start(priority=sl)
    for sl in range(2):
        pltpu.make_async_copy(h2_ref, xs_ref.at[pl.ds(0, TD)], sem).wait()


def _dispatch(pad_start, dest_tiles, h2, rows):
    t, d = h2.shape
    return pl.pallas_call(
        _dispatch_body,
        out_shape=jax.ShapeDtypeStruct((rows, d), h2.dtype),
        grid_spec=pltpu.PrefetchScalarGridSpec(
            num_scalar_prefetch=1,
            grid=(t // TD,),
            in_specs=[
                pl.BlockSpec((1, 1, 2 * TD), lambda i, pad: (i, 0, 0), memory_space=pltpu.SMEM),
                pl.BlockSpec((TD, d), lambda i, pad: (i, 0)),
            ],
            out_specs=pl.BlockSpec(memory_space=pl.ANY),
            scratch_shapes=[pltpu.VMEM((TME, d), h2.dtype), pltpu.SemaphoreType.DMA],
        ),
        compiler_params=_cparams(("arbitrary",)),
        name="moe_dispatch",
    )(pad_start, dest_tiles, h2)


def _expert_body(te_ref, nu_ref, xs_ref, wg_ref, wu_ref, wd_ref, ys_ref, wgb_ref, wub_ref, wdb_ref):
    i = pl.program_id(0)
    used = i < nu_ref[0]
    new_expert = jnp.logical_or(i == 0, te_ref[i] != te_ref[jnp.maximum(i - 1, 0)])

    @pl.when(jnp.logical_and(used, new_expert))
    def _():
        wgb_ref[...] = wg_ref[0, 0].astype(BF16)
        wub_ref[...] = wu_ref[0, 0].astype(BF16)
        wdb_ref[...] = wd_ref[0, 0].astype(BF16)

    @pl.when(used)
    def _():
        nblk = 2
        rb = TME // nblk
        xb = [_unpack_pairs(xs_ref[k * rb:(k + 1) * rb, :]).astype(BF16) for k in range(nblk)]
        hg = [_dot(v, wgb_ref[...]) for v in xb]
        hu = [_dot(v, wub_ref[...]) for v in xb]
        hid = [(_silu(g) * u).astype(BF16) for g, u in zip(hg, hu)]
        for k in range(nblk):
            ys_ref[k * rb:(k + 1) * rb, :] = _pack_pairs(_dot(hid[k], wdb_ref[...]))

    @pl.when(jnp.logical_not(used))
    def _():
        ys_ref[...] = jnp.zeros_like(ys_ref)


def _experts(layer, tile_expert, n_used, xs, rows, w_gate, w_up, w_down):
    dp = xs.shape[1]
    d = 2 * dp
    ntiles = rows // TME
    return pl.pallas_call(
        _expert_body,
        out_shape=jax.ShapeDtypeStruct((rows, dp), xs.dtype),
        grid_spec=pltpu.PrefetchScalarGridSpec(
            num_scalar_prefetch=2,
            grid=(ntiles,),
            in_specs=[
                pl.BlockSpec((TME, dp), lambda i, te, nu: (jnp.minimum(i, nu[0] - 1), 0)),
                pl.BlockSpec((1, 1, d, FF), lambda i, te, nu: (layer, te[i], 0, 0)),
                pl.BlockSpec((1, 1, d, FF), lambda i, te, nu: (layer, te[i], 0, 0)),
                pl.BlockSpec((1, 1, FF, d), lambda i, te, nu: (layer, te[i], 0, 0)),
            ],
            out_specs=pl.BlockSpec((TME, dp), lambda i, te, nu: (i, 0)),
            scratch_shapes=[pltpu.VMEM((d, FF), BF16), pltpu.VMEM((d, FF), BF16), pltpu.VMEM((FF, d), BF16)],
        ),
        compiler_params=_cparams(("arbitrary",)),
        name="moe_experts",
    )(tile_expert, n_used, xs, w_gate, w_up, w_down)


def _gather_pairs(dest_ref, next_ref, ys_ref, buf_ref, sem):
    step = pl.program_id(0) * pl.num_programs(1) + pl.program_id(1)
    nsteps = pl.num_programs(0) * pl.num_programs(1)
    cur = lax.rem(step, 2)

    def start_tile(idx_ref, b):
        for t in range(TMG):
            for sl in range(2):
                src = idx_ref[0, 0, sl * TMG + t]
                pltpu.make_async_copy(ys_ref.at[pl.ds(src, 1)], buf_ref.at[b, sl, pl.ds(t, 1)],
                                      sem.at[b]).start(priority=sl)

    @pl.when(step == 0)
    def _():
        start_tile(dest_ref, 0)

    for b in range(2):
        @pl.when(jnp.logical_and(step + 1 < nsteps, cur == 1 - b))
        def _():
            start_tile(next_ref, b)

    for sl in range(2):
        pltpu.make_async_copy(ys_ref.at[pl.ds(0, TMG)], buf_ref.at[cur, sl], sem.at[cur]).wait()
    return buf_ref.at[cur]


def _moe_combine(wc_ref, rows_ref):
    return wc_ref[:, 0:1] * _unpack_pairs(rows_ref[0]) + wc_ref[:, 1:2] * _unpack_pairs(rows_ref[1])


def _gather_specs(nt, ntiles):
    return [
        pl.BlockSpec((1, 1, 2 * TMG), lambda b, i: (b * nt + i, 0, 0), memory_space=pltpu.SMEM),
        pl.BlockSpec((1, 1, 2 * TMG), lambda b, i: (jnp.minimum(b * nt + i + 1, ntiles - 1), 0, 0),
                     memory_space=pltpu.SMEM),
    ]


GATHER_SCRATCH = [pltpu.VMEM((2, 2, TMG, D // 2), U32), pltpu.SemaphoreType.DMA((2,))]


def _routing_plan(meta_i, counts, tile_tokens):
    t = meta_i.shape[1]
    cnt_perm = counts[:, 0].astype(I32)
    e = jnp.arange(N_EXP)
    cnt = cnt_perm[(e % EXP_PER_GROUP) * N_GROUPS + e // EXP_PER_GROUP]
    tiles_e = (cnt + TME - 1) // TME
    cum_tiles = jnp.cumsum(tiles_e)
    offs = (cum_tiles - tiles_e) * TME
    sel = meta_i[0:2][..., None] == e
    dest = jnp.sum(jnp.where(sel, offs, 0), axis=-1) + meta_i[2:4]
    ntiles = (2 * t) // TME + N_EXP
    n_used = cum_tiles[-1]
    te = jnp.sum(cum_tiles[None, :] <= jnp.arange(ntiles, dtype=I32)[:, None], axis=-1).astype(I32)
    last_e = jnp.max(jnp.where(tiles_e > 0, e, 0)).astype(I32)
    te = jnp.minimum(te, last_e)

    def tiles(n):
        return dest.reshape(2, t // n, n).transpose(1, 0, 2).reshape(t // n, 1, 2 * n)

    pad_start = jnp.concatenate([jnp.where(tiles_e > 0, cum_tiles - 1, -1), n_used[None]]).astype(I32)
    return pad_start, tiles(TD), tiles(tile_tokens), te, n_used.reshape(1).astype(I32), ntiles * TME


def _moe(layer, h2, meta_i, counts, w_gate, w_up, w_down):
    pad_start, dest_td, dest_tm, te, n_used, rows = _routing_plan(meta_i, counts, TMG)
    xs = _dispatch(pad_start, dest_td, h2, rows)
    ys = _experts(layer, te, n_used, xs, rows, w_gate, w_up, w_down)
    return ys, dest_tm


def _conf_in_body(x_ref, dest_ref, next_ref, wc_ref, ys_ref, pv_ref, w1_ref, b1_ref, x_out_ref, u_ref,
                  buf_ref, sem):
    rows_ref = _gather_pairs(dest_ref, next_ref, ys_ref, buf_ref, sem)
    x2 = x_ref[0] + pv_ref[0, 0:1, :] * _moe_combine(wc_ref, rows_ref)
    x_out_ref[0] = x2
    h = (_rms(x2) * pv_ref[0, 1:2, :] + pv_ref[0, 2:3, :]).astype(BF16)
    u = _dot(h, w1_ref[...]) + b1_ref[0:1, :]
    u_ref[0] = (u[:, :D] * _sigmoid(u[:, D:])).astype(BF16)


def _conf_in(x1, dest_tm, wcol, ys, pvec, w1, b1):
    bsz, seq, d = x1.shape
    nt = seq // TMG
    return pl.pallas_call(
        _conf_in_body,
        out_shape=(jax.ShapeDtypeStruct((bsz, seq, d), F32), jax.ShapeDtypeStruct((bsz, seq, d), BF16)),
        grid=(bsz, nt),
        in_specs=[pl.BlockSpec((1, TMG, d), lambda b, i: (b, i, 0))] + _gather_specs(nt, bsz * nt) + [
            pl.BlockSpec((TMG, 8), lambda b, i: (b * nt + i, 0)),
            pl.BlockSpec(memory_space=pl.ANY),
            pl.BlockSpec((1, 8, d), lambda b, i: (b, 0, 0)),
            pl.BlockSpec((d, 2 * d), lambda b, i: (0, 0)),
            pl.BlockSpec((8, 2 * d), lambda b, i: (0, 0)),
        ],
        out_specs=(pl.BlockSpec((1, TMG, d), lambda b, i: (b, i, 0)),
                   pl.BlockSpec((1, TMG, d), lambda b, i: (b, i, 0))),
        scratch_shapes=GATHER_SCRATCH,
        compiler_params=_cparams(("arbitrary", "arbitrary")),
        name="combine_conf_in",
    )(x1, dest_tm, dest_tm, wcol, ys, pvec, w1, b1)


def _conf_mix_body(nt, cur_ref, prev_ref, next_ref, dw_ref, cv_ref, x_ref, w2_ref, pv_ref, rwt_ref,
                   rb_ref, upper_ref, x_out_ref, h2_ref, mi_ref, mf_ref, cnt_ref, ext_ref, sh_ref,
                   conv_ref, carry_ref):
    i = pl.program_id(1)
    ext_ref[0:HALO, :] = jnp.where(i == 0, 0.0, prev_ref[0].astype(F32))
    ext_ref[HALO:HALO + TMR, :] = cur_ref[0].astype(F32)
    ext_ref[HALO + TMR:2 * HALO + TMR, :] = jnp.where(i == nt - 1, 0.0, next_ref[0].astype(F32))
    pad = CONF_K // 2
    span = TMR + CONF_SPAN
    for cb in range(D // LANES):
        cs = slice(cb * LANES, (cb + 1) * LANES)
        for r in range(1, 8):
            sh_ref[r - 1, :, cs] = ext_ref[r:r + span, cs]
        for rb in range(0, TMR, CONF_ROWS):
            acc = None
            for j in range(CONF_K):
                lo = HALO - pad + j
                r, base = lo % 8, lo - lo % 8 + rb
                src = ext_ref[base:base + CONF_ROWS, cs] if r == 0 else sh_ref[r - 1, base:base + CONF_ROWS, cs]
                term = src * dw_ref[j:j + 1, cs]
                acc = term if acc is None else acc + term
            conv_ref[rb:rb + CONF_ROWS, cs] = acc + cv_ref[0:1, cs]
    u = conv_ref[...]
    mu = jnp.mean(u, axis=-1, keepdims=True)
    uc = u - mu
    var = jnp.mean(uc * uc, axis=-1, keepdims=True)
    y = uc * lax.rsqrt(var + EPS) * cv_ref[1:2, :] + cv_ref[2:3, :]
    mix = _dot(_silu(y).astype(BF16), w2_ref[...]) + cv_ref[3:4, :]
    x_new = x_ref[0] + pv_ref[0, 0:1, :] * mix
    _route_tail(x_new, pv_ref, rwt_ref, rb_ref, upper_ref, x_out_ref, h2_ref, mi_ref, mf_ref, cnt_ref,
                carry_ref)


def _conf_mix(u, dw32, cvec, x2, w2, pvec, rwt, rb, upper):
    bsz, seq, d = x2.shape
    nt = seq // TMR
    r = TMR // HALO
    cst = lambda shape: pl.BlockSpec(shape, lambda b, i: (0,) * len(shape))
    return pl.pallas_call(
        functools.partial(_conf_mix_body, nt),
        out_shape=_route_out_shapes(bsz, seq),
        grid=(bsz, nt),
        in_specs=[
            pl.BlockSpec((1, TMR, d), lambda b, i: (b, i, 0)),
            pl.BlockSpec((1, HALO, d), lambda b, i: (b, jnp.maximum(i * r - 1, 0), 0)),
            pl.BlockSpec((1, HALO, d), lambda b, i: (b, jnp.minimum((i + 1) * r, nt * r - 1), 0)),
            cst((32, d)), cst((8, d)),
            pl.BlockSpec((1, TMR, d), lambda b, i: (b, i, 0)),
            cst((d, d)),
            pl.BlockSpec((1, 8, d), lambda b, i: (b, 0, 0)),
        ] + _route_in_specs(),
        out_specs=_route_out_specs(nt),
        scratch_shapes=[pltpu.VMEM((TMR + 2 * HALO, d), F32), pltpu.VMEM((7, TMR + CONF_SPAN, d), F32),
                        pltpu.VMEM((TMR, d), F32),
                        pltpu.VMEM((N_EXP, LANES), F32)],
        compiler_params=_cparams(("arbitrary", "arbitrary")),
        name="conf_mix_router",
    )(u, u, u, dw32, cvec, x2, w2, pvec, rwt, rb, upper)


def _final_body(x_ref, dest_ref, next_ref, wc_ref, ys_ref, pv_ref, o_ref, buf_ref, sem):
    rows_ref = _gather_pairs(dest_ref, next_ref, ys_ref, buf_ref, sem)
    x4 = x_ref[0] + pv_ref[0, 0:1, :] * _moe_combine(wc_ref, rows_ref)
    o_ref[0] = _rms(x4) * pv_ref[0, 1:2, :]


def _final(x3, dest_tm, wcol, ys, pvec):
    bsz, seq, d = x3.shape
    nt = seq // TMG
    return pl.pallas_call(
        _final_body,
        out_shape=jax.ShapeDtypeStruct((bsz, seq, d), F32),
        grid=(bsz, nt),
        in_specs=[pl.BlockSpec((1, TMG, d), lambda b, i: (b, i, 0))] + _gather_specs(nt, bsz * nt) + [
            pl.BlockSpec((TMG, 8), lambda b, i: (b * nt + i, 0)),
            pl.BlockSpec(memory_space=pl.ANY),
            pl.BlockSpec((1, 8, d), lambda b, i: (b, 0, 0)),
        ],
        out_specs=pl.BlockSpec((1, TMG, d), lambda b, i: (b, i, 0)),
        scratch_shapes=GATHER_SCRATCH,
        compiler_params=_cparams(("arbitrary", "arbitrary")),
        name="combine_final",
    )(x3, dest_tm, dest_tm, wcol, ys, pvec)


def _grid_sincos(rows, cols, dim):
    quarter = dim // 4
    omega = 1.0 / jnp.power(POS_BASE, jnp.arange(quarter, dtype=F32) / quarter)

    def axis_emb(n):
        ang = jnp.arange(n, dtype=F32)[:, None] * omega[None, :]
        return jnp.concatenate([jnp.sin(ang), jnp.cos(ang)], axis=-1)

    er = jnp.broadcast_to(axis_emb(rows)[:, None, :], (rows, cols, dim // 2))
    ec = jnp.broadcast_to(axis_emb(cols)[None, :, :], (rows, cols, dim // 2))
    return jnp.concatenate([er, ec], axis=-1).reshape(rows * cols, dim)


def _pad_rows(a, rows):
    return jnp.pad(a, ((0, rows - a.shape[0]),) + ((0, 0),) * (a.ndim - 1))


def _per_batch(*vecs):
    st = jnp.stack(vecs, axis=1)
    return jnp.pad(st, ((0, 0), (0, 8 - st.shape[1]), (0, 0)))


def _width_dft():
    k = np.arange(LANES)
    ang = 2 * np.pi * np.outer(k, k) / LANES
    groups = FN_W // LANES
    c = np.kron(np.eye(groups), np.cos(ang)) / math.sqrt(LANES)
    s = np.kron(np.eye(groups), np.sin(ang)) / math.sqrt(LANES)
    return jnp.asarray(np.concatenate([c, s], axis=1), dtype=BF16)


def kernel(x, c, ctx, c_ctx, ada_w, ada_b, norm1_g, norm2_g, hyb_w_in, dn_conv_w, dn_a_log, dn_dt_bias,
           dn_onorm_g, hyb_w_out, conf_w1, conf_b1, conf_dw_w, conf_dw_b, conf_ln_g, conf_ln_b, conf_w2,
           conf_b2, router_w, router_bias, moe_w_gate, moe_w_up, moe_w_down, final_g):
    bsz, seq, d = x.shape
    ctx_len = ctx.shape[1]
    assert d == D and seq % max(TM, TMR, TMG) == 0 and ctx_len % TM == 0 and (bsz * seq) % TD == 0
    nct = ctx_len // TM
    t = bsz * seq

    pos = _grid_sincos(seq // GRID_W, GRID_W, d)

    cpad = _pad_rows(jnp.concatenate([c, c_ctx[None, :]], axis=0), 16)
    mods = _adaln(cpad, ada_w, ada_b)
    m = lambda l, k: mods[l, :, k * d:(k + 1) * d]
    a1 = norm1_g[0][None, :] * (1.0 + m(0, 1))
    mod_a = jnp.stack([jnp.broadcast_to(a1[bsz], (bsz, d)), a1[:bsz]], axis=1)
    mod_b = jnp.stack([jnp.broadcast_to(m(0, 0)[bsz], (bsz, d)), m(0, 0)[:bsz]], axis=1)

    w_in = hyb_w_in[0]
    z_lo = QKV_W + AB_W
    f_lo = z_lo + Z_W
    wqkv = w_in[:, :QKV_W].astype(BF16)
    wab = jnp.pad(w_in[:, QKV_W:z_lo], ((0, 0), (0, LANES - AB_W))).astype(BF16)
    wz = w_in[:, z_lo:f_lo].astype(BF16)
    wf = w_in[:, f_lo:].astype(BF16)
    gpar = jnp.zeros((8, LANES), F32)
    gpar = gpar.at[0, :2 * HEADS].set(-jnp.exp(dn_a_log[0].reshape(-1)))
    gpar = gpar.at[1, :2 * HEADS].set(dn_dt_bias[0].reshape(-1))
    qkv_pre, z, fc, fs, gates = _inproj(ctx, x, pos, mod_a, mod_b, wqkv, wz, wf, wab, _width_dft(), gpar)

    qkv = _dnconv(qkv_pre, _pad_rows(dn_conv_w[0], 8), nct)
    lc = qkv.shape[1]
    nchunks = lc // CHUNK
    grow = gates[:, :, :AB_W].reshape(bsz, nchunks, CHUNK, 4, HEADS)
    grow = grow.transpose(0, 1, 3, 4, 2).reshape(bsz, nchunks, 4, HB)
    grow = jnp.pad(grow, ((0, 0), (0, 0), (0, 4), (0, 0)))
    o_f, o_b = _deltanet(qkv, grow, seq, ctx_len // CHUNK)

    fn = _seq_dft(fc, fs)

    perm = (np.arange(N_EXP) % N_GROUPS) * EXP_PER_GROUP + np.arange(N_EXP) // N_GROUPS
    rwt = router_w.T[perm].astype(BF16)
    rb = jnp.broadcast_to(router_bias[perm][:, None], (N_EXP, LANES)).astype(F32)
    upper = jnp.asarray(np.triu(np.ones((TMR, TMR), np.float32), 1), dtype=BF16)
    w_out = hyb_w_out[0].astype(BF16)
    onorm8 = _pad_rows(jnp.tile(dn_onorm_g[0], HEADS)[None, :], 8)
    pv0 = _per_batch(m(0, 2)[:bsz], norm2_g[0][None, :] * (1.0 + m(0, 4)[:bsz]), m(0, 3)[:bsz])
    x1, h2, meta_i, meta_f, counts = _mix0(o_f, o_b, z, fn, x, pos, w_out[:V_W], w_out[V_W:], onorm8, pv0,
                                           rwt, rb, upper)
    ys, dest_tm = _moe(0, h2, meta_i, counts, moe_w_gate, moe_w_up, moe_w_down)
    wcol = meta_f.T

    pv1 = _per_batch(m(0, 5)[:bsz], norm1_g[1][None, :] * (1.0 + m(1, 1)[:bsz]), m(1, 0)[:bsz])
    x2, u = _conf_in(x1, dest_tm, wcol, ys, pv1, conf_w1[0].astype(BF16), _pad_rows(conf_b1[0][None, :], 8))
    cvec = _pad_rows(jnp.stack([conf_dw_b[0], conf_ln_g[0], conf_ln_b[0], conf_b2[0]]), 8)
    pv2 = _per_batch(m(1, 2)[:bsz], norm2_g[1][None, :] * (1.0 + m(1, 4)[:bsz]), m(1, 3)[:bsz])
    x3, h2, meta_i, meta_f, counts = _conf_mix(u, _pad_rows(conf_dw_w[0], 32), cvec, x2,
                                               conf_w2[0].astype(BF16), pv2, rwt, rb, upper)
    ys, dest_tm = _moe(1, h2, meta_i, counts, moe_w_gate, moe_w_up, moe_w_down)
    pv3 = _per_batch(m(1, 5)[:bsz], jnp.broadcast_to(final_g[None, :], (bsz, d)))
    return _final(x3, dest_tm, meta_f.T, ys, pv3)
```
